```python
import math
import jax, jax.numpy as jnp
from jax import lax
import numpy as np

D_MODEL = 4096
BATCH = 8
SEQ = 2048
DEPTH = 2

GRID_W = 64
CTX_LEN = 256
N_MIXERS = 2
HEAD_DIM = 128
DIFF_HEADS = D_MODEL // (2 * HEAD_DIM)
DIFF_V_DIM = 2 * HEAD_DIM
GQA_HEADS = D_MODEL // HEAD_DIM
GQA_KV_HEADS = GQA_HEADS // 4
GQA_GROUP = GQA_HEADS // GQA_KV_HEADS
GQA_KV_DIM = GQA_KV_HEADS * HEAD_DIM
N_EXPERTS = 16
EC_CAPACITY_FACTOR = 2
EXPERT_FF = D_MODEL // 4
ROPE_THETA = 10000.0
ROPE_AXIS_DIM = HEAD_DIM // 2
Q_BLOCK = 128
NORM_EPS = 1e-6
N_MOD = 6

kernel_name = "hybrid_diffattn_gqa_ecmoe_dit"


def rms_norm(x, g):
    xf = x.astype(jnp.float32)
    y = xf * lax.rsqrt(jnp.mean(xf * xf, axis=-1, keepdims=True) + NORM_EPS)
    return (y * g.astype(jnp.float32)).astype(x.dtype)


def adaln(cond, w, b):
    m = jax.nn.silu(cond) @ w + b
    return m.reshape(cond.shape[0], N_MOD, D_MODEL)


def modulate(h, shift, scale):
    return h * (1 + scale) + shift


def axial_rope(n):
    rows = n // GRID_W
    row = jnp.repeat(jnp.arange(rows, dtype=jnp.float32), GRID_W)
    col = jnp.tile(jnp.arange(GRID_W, dtype=jnp.float32), rows)
    inv_freq = ROPE_THETA ** (-jnp.arange(0, ROPE_AXIS_DIM, 2, dtype=jnp.float32) / ROPE_AXIS_DIM)
    ang = jnp.concatenate([row[:, None] * inv_freq, col[:, None] * inv_freq], axis=-1)
    return jnp.cos(ang), jnp.sin(ang)


def apply_rope(x, cos, sin):
    shape = (cos.shape[0],) + (1,) * (x.ndim - 3) + (cos.shape[1],)
    cos = cos.reshape(shape)
    sin = sin.reshape(shape)
    xf = x.astype(jnp.float32)
    half = HEAD_DIM // 2
    x1, x2 = xf[..., :half], xf[..., half:]
    out = jnp.concatenate([x1 * cos - x2 * sin, x2 * cos + x1 * sin], axis=-1)
    return out.astype(x.dtype)


def sweep_query_blocks(core, q):
    B, n = q.shape[:2]
    nb = n // Q_BLOCK
    qb = jnp.moveaxis(q.reshape((B, nb, Q_BLOCK) + q.shape[2:]), 1, 0)
    ob = lax.map(core, qb)
    return jnp.moveaxis(ob, 0, 1).reshape((B, n) + ob.shape[3:])


def diff_core(q, k, v, lam):
    s = jnp.einsum('bqhtd,bkhtd->bhtqk', q, k, preferred_element_type=jnp.float32) / math.sqrt(HEAD_DIM)
    p = jax.nn.softmax(s, axis=-1)
    a = p[:, :, 0] - lam * p[:, :, 1]
    return jnp.einsum('bhqk,bkhe->bqhe', a.astype(v.dtype), v)


def diff_attention(h_lat, h_ctx, wqkv, wo, qn, kn, lq1, lk1, lq2, lk2, subln, lam_init, cos, sin, need_ctx):
    def project(h):
        B, n, _ = h.shape
        q, k, v = jnp.split(h @ wqkv, 3, axis=-1)
        q = rms_norm(q.reshape(B, n, DIFF_HEADS, 2, HEAD_DIM), qn)
        k = rms_norm(k.reshape(B, n, DIFF_HEADS, 2, HEAD_DIM), kn)
        return q, k, v.reshape(B, n, DIFF_HEADS, DIFF_V_DIM)

    q_l, k_l, v_l = project(h_lat)
    q_l, k_l = apply_rope(q_l, cos, sin), apply_rope(k_l, cos, sin)
    q_c, k_c, v_c = project(h_ctx)
    lam = (jnp.exp(jnp.sum(lq1.astype(jnp.float32) * lk1.astype(jnp.float32)))
           - jnp.exp(jnp.sum(lq2.astype(jnp.float32) * lk2.astype(jnp.float32))) + lam_init)
    k_all = jnp.concatenate([k_l, k_c], axis=1)
    v_all = jnp.concatenate([v_l, v_c], axis=1)

    def finish(o):
        B, n = o.shape[:2]
        o = rms_norm(o, subln) * (1 - lam_init)
        return o.reshape(B, n, D_MODEL) @ wo

    out_lat = finish(sweep_query_blocks(lambda qb: diff_core(qb, k_all, v_all, lam), q_l))
    out_ctx = finish(diff_core(q_c, k_c, v_c, lam)) if need_ctx else None
    return out_lat, out_ctx


def gqa_core(q, k, v):
    s = jnp.einsum('bqhgd,bkhd->bhgqk', q, k, preferred_element_type=jnp.float32) / math.sqrt(HEAD_DIM)
    p = jax.nn.softmax(s, axis=-1)
    return jnp.einsum('bhgqk,bkhd->bqhgd', p.astype(v.dtype), v)


def gqa_attention(h_lat, h_ctx, wqkv, wo, qn, kn, cos, sin, need_ctx):
    def project(h):
        B, n, _ = h.shape
        qkv = h @ wqkv
        q = rms_norm(qkv[..., :D_MODEL].reshape(B, n, GQA_KV_HEADS, GQA_GROUP, HEAD_DIM), qn)
        k = rms_norm(qkv[..., D_MODEL:D_MODEL + GQA_KV_DIM].reshape(B, n, GQA_KV_HEADS, HEAD_DIM), kn)
        v = qkv[..., D_MODEL + GQA_KV_DIM:].reshape(B, n, GQA_KV_HEADS, HEAD_DIM)
        return q, k, v

    q_l, k_l, v_l = project(h_lat)
    q_l, k_l = apply_rope(q_l, cos, sin), apply_rope(k_l, cos, sin)
    q_c, k_c, v_c = project(h_ctx)
    k_all = jnp.concatenate([k_l, k_c], axis=1)
    v_all = jnp.concatenate([v_l, v_c], axis=1)

    def finish(o):
        B, n = o.shape[:2]
        return o.reshape(B, n, D_MODEL) @ wo

    out_lat = finish(sweep_query_blocks(lambda qb: gqa_core(qb, k_all, v_all), q_l))
    out_ctx = finish(gqa_core(q_c, k_c, v_c)) if need_ctx else None
    return out_lat, out_ctx


def ec_moe(h, router, wg, wu, wd):
    B, n, D = h.shape
    cap = EC_CAPACITY_FACTOR * n // N_EXPERTS
    aff = jax.nn.softmax((h @ router).astype(jnp.float32), axis=-1)
    gates, idx = lax.top_k(jnp.swapaxes(aff, 1, 2), cap)
    xs = jax.vmap(lambda hb, ib: hb[ib])(h, idx)
    hid = jax.nn.silu(jnp.einsum('becd,edf->becf', xs, wg)) * jnp.einsum('becd,edf->becf', xs, wu)
    y = jnp.einsum('becf,efd->becd', hid, wd) * gates[..., None].astype(h.dtype)
    return jax.vmap(lambda yb, ib: jnp.zeros((n, D), h.dtype).at[ib.reshape(-1)].add(yb.reshape(-1, D)))(y, idx)


def setup_inputs(seed: int = 0) -> dict:
    key = jax.random.key(seed)
    ks = iter(jax.random.split(key, 48))
    f32 = jnp.float32

    def nrm(shape, scale):
        return jax.random.normal(next(ks), shape, f32) * scale

    def gain(n):
        return 1.0 + nrm((n,), 0.02)

    D = D_MODEL
    inp = {}
    inp["x"] = nrm((BATCH, SEQ, D), 1.0)
    inp["c"] = nrm((BATCH, D), 1.0)
    inp["ctx"] = nrm((BATCH, CTX_LEN, D), 1.0)
    inp["c_ctx"] = nrm((D,), 1.0)
    inp["mod_w_0"] = nrm((D, N_MOD * D), 0.2 * D ** -0.5)
    inp["mod_b_0"] = nrm((N_MOD * D,), 0.02)
    inp["norm1_0"] = gain(D)
    inp["norm2_0"] = gain(D)
    inp["wqkv_0"] = nrm((D, 3 * D), D ** -0.5)
    inp["wo_0"] = nrm((D, D), D ** -0.5)
    inp["qnorm_0"] = gain(HEAD_DIM)
    inp["knorm_0"] = gain(HEAD_DIM)
    inp["lam_q1_0"] = nrm((HEAD_DIM,), 0.1)
    inp["lam_k1_0"] = nrm((HEAD_DIM,), 0.1)
    inp["lam_q2_0"] = nrm((HEAD_DIM,), 0.1)
    inp["lam_k2_0"] = nrm((HEAD_DIM,), 0.1)
    inp["subln_0"] = gain(DIFF_V_DIM)
    inp["router_0"] = nrm((D, N_EXPERTS), D ** -0.5)
    inp["we_gate_0"] = nrm((N_EXPERTS, D, EXPERT_FF), D ** -0.5)
    inp["we_up_0"] = nrm((N_EXPERTS, D, EXPERT_FF), D ** -0.5)
    inp["we_down_0"] = nrm((N_EXPERTS, EXPERT_FF, D), EXPERT_FF ** -0.5)
    inp["mod_w_1"] = nrm((D, N_MOD * D), 0.2 * D ** -0.5)
    inp["mod_b_1"] = nrm((N_MOD * D,), 0.02)
    inp["norm1_1"] = gain(D)
    inp["norm2_1"] = gain(D)
    inp["wqkv_1"] = nrm((D, D + 2 * GQA_KV_DIM), D ** -0.5)
    inp["wo_1"] = nrm((D, D), D ** -0.5)
    inp["qnorm_1"] = gain(HEAD_DIM)
    inp["knorm_1"] = gain(HEAD_DIM)
    inp["router_1"] = nrm((D, N_EXPERTS), D ** -0.5)
    inp["we_gate_1"] = nrm((N_EXPERTS, D, EXPERT_FF), D ** -0.5)
    inp["we_up_1"] = nrm((N_EXPERTS, D, EXPERT_FF), D ** -0.5)
    inp["we_down_1"] = nrm((N_EXPERTS, EXPERT_FF, D), EXPERT_FF ** -0.5)
    return inp


def reference(x, c, ctx, c_ctx,
              mod_w_0, mod_b_0, norm1_0, norm2_0, wqkv_0, wo_0, qnorm_0, knorm_0,
              lam_q1_0, lam_k1_0, lam_q2_0, lam_k2_0, subln_0, router_0, we_gate_0, we_up_0, we_down_0,
              mod_w_1, mod_b_1, norm1_1, norm2_1, wqkv_1, wo_1, qnorm_1, knorm_1,
              router_1, we_gate_1, we_up_1, we_down_1):
    layers = [
        dict(mod_w=mod_w_0, mod_b=mod_b_0, norm1=norm1_0, norm2=norm2_0, wqkv=wqkv_0, wo=wo_0,
             qn=qnorm_0, kn=knorm_0, lq1=lam_q1_0, lk1=lam_k1_0, lq2=lam_q2_0, lk2=lam_k2_0,
             subln=subln_0, router=router_0, wg=we_gate_0, wu=we_up_0, wd=we_down_0),
        dict(mod_w=mod_w_1, mod_b=mod_b_1, norm1=norm1_1, norm2=norm2_1, wqkv=wqkv_1, wo=wo_1,
             qn=qnorm_1, kn=knorm_1, router=router_1, wg=we_gate_1, wu=we_up_1, wd=we_down_1),
    ]
    cos, sin = axial_rope(x.shape[1])
    x_lat, x_ctx = x, ctx
    for i in range(DEPTH):
        p = layers[i]
        need_ctx = i < DEPTH - 1
        m_lat = adaln(c, p["mod_w"], p["mod_b"])
        m_ctx = adaln(c_ctx[None], p["mod_w"], p["mod_b"])
        sh1, sc1, g1, sh2, sc2, g2 = [m_lat[:, j, None, :] for j in range(N_MOD)]
        csh1, csc1, cg1, csh2, csc2, cg2 = [m_ctx[:, j, None, :] for j in range(N_MOD)]

        h_lat = modulate(rms_norm(x_lat, p["norm1"]), sh1, sc1)
        h_ctx = modulate(rms_norm(x_ctx, p["norm1"]), csh1, csc1)
        if i % N_MIXERS == 0:
            lam_init = 0.8 - 0.6 * math.exp(-0.3 * i)
            o_lat, o_ctx = diff_attention(h_lat, h_ctx, p["wqkv"], p["wo"], p["qn"], p["kn"],
                                          p["lq1"], p["lk1"], p["lq2"], p["lk2"], p["subln"],
                                          lam_init, cos, sin, need_ctx)
        else:
            o_lat, o_ctx = gqa_attention(h_lat, h_ctx, p["wqkv"], p["wo"], p["qn"], p["kn"],
                                         cos, sin, need_ctx)
        x_lat = x_lat + g1 * o_lat

        h2_lat = modulate(rms_norm(x_lat, p["norm2"]), sh2, sc2)
        x_lat = x_lat + g2 * ec_moe(h2_lat, p["router"], p["wg"], p["wu"], p["wd"])
        if need_ctx:
            x_ctx = x_ctx + cg1 * o_ctx
            h2_ctx = modulate(rms_norm(x_ctx, p["norm2"]), csh2, csc2)
            x_ctx = x_ctx + cg2 * ec_moe(h2_ctx, p["router"], p["wg"], p["wu"], p["wd"])
    return x_lat
```

```python
import functools
import math

import jax
import jax.numpy as jnp
from jax import lax
from jax.experimental import pallas as pl
from jax.experimental.pallas import tpu as pltpu

F32 = jnp.float32
BF16 = jnp.bfloat16

D_MODEL = 4096
GRID_W = 64
HEAD_DIM = 128
N_EXPERTS = 16
EC_CAPACITY_FACTOR = 2
EXPERT_FF = D_MODEL // 4
ROPE_THETA = 10000.0
NORM_EPS = 1e-6
N_MOD = 6
MOD_ROWS = 16
LANES = 128
VMEM_LIMIT = 58 * 1024 * 1024


def _cparams(sem):
    return pltpu.CompilerParams(dimension_semantics=sem, vmem_limit_bytes=VMEM_LIMIT)


def _bdot(a, b):
    return jnp.dot(a, b, preferred_element_type=F32)


def _dot_nt(a, b):
    return lax.dot_general(a, b, (((1,), (1,)), ((), ())), preferred_element_type=F32)


def _silu(x):
    return x * (1.0 / (1.0 + jnp.exp(-x)))


def _split_bf16(x):
    hi = x.astype(BF16)
    lo = (x - hi.astype(F32)).astype(BF16)
    return hi, lo


def _adaln_kernel(c_ref, w_ref, b_ref, o_ref):
    s = _silu(c_ref[...]).astype(BF16)
    o_ref[...] = _bdot(s, w_ref[...].astype(BF16)) + b_ref[...]


def _adaln(cond, w, b):
    n = w.shape[1]
    tn = 512
    out = pl.pallas_call(
        _adaln_kernel,
        grid=(n // tn,),
        in_specs=[
            pl.BlockSpec((MOD_ROWS, D_MODEL), lambda j: (0, 0)),
            pl.BlockSpec((D_MODEL, tn), lambda j: (0, j)),
            pl.BlockSpec((1, tn), lambda j: (0, j)),
        ],
        out_specs=pl.BlockSpec((MOD_ROWS, tn), lambda j: (0, j)),
        out_shape=jax.ShapeDtypeStruct((MOD_ROWS, n), F32),
        compiler_params=_cparams(("arbitrary",)),
    )(cond, w, b.reshape(1, n))
    return out.reshape(MOD_ROWS, N_MOD, 1, D_MODEL)


def _norm_mod(xv, g, shift, scale):
    ms = jnp.mean(xv * xv, axis=-1, keepdims=True)
    y = xv * lax.rsqrt(ms + NORM_EPS) * g
    return y * (1.0 + scale) + shift


def _qkv_kernel(x_ref, g_ref, sh_ref, sc_ref, w_ref, cg_ref, cos_ref, sin_ref, o_ref, h_scr,
                *, tm, tn, n_qk_tiles, chunk):
    j = pl.program_id(1)

    @pl.when(j == 0)
    def _():
        def body(r, carry):
            rows = pl.ds(pl.multiple_of(r * chunk, chunk), chunk)
            h = _norm_mod(x_ref[rows, :], g_ref[...], sh_ref[...], sc_ref[...])
            h_scr[rows, :] = h.astype(BF16)
            return carry
        lax.fori_loop(0, tm // chunk, body, 0)

    acc = _bdot(h_scr[...], w_ref[...].astype(BF16))

    @pl.when(j < n_qk_tiles)
    def _():
        cos = cos_ref[...]
        sin = sin_ref[...]
        for gi in range(tn // HEAD_DIM):
            cols = slice(gi * HEAD_DIM, (gi + 1) * HEAD_DIM)
            a = acc[:, cols]
            ms = jnp.mean(a * a, axis=-1, keepdims=True)
            a = a * lax.rsqrt(ms + NORM_EPS) * cg_ref[:, cols]
            a = a * cos + pltpu.roll(a, HEAD_DIM // 2, axis=1) * sin
            o_ref[:, cols] = a.astype(BF16)

    @pl.when(j >= n_qk_tiles)
    def _():
        o_ref[...] = acc.astype(BF16)


def _qkv_proj(x, gain, mod, mod_row_fn, w, colgain, cos, sin, n_qk_cols):
    r, k = x.shape
    n = w.shape[1]
    tm, tn = 1024, 512
    npos = cos.shape[0]
    kern = functools.partial(_qkv_kernel, tm=tm, tn=tn, n_qk_tiles=n_qk_cols // tn, chunk=32)
    return pl.pallas_call(
        kern,
        grid=(r // tm, n // tn),
        in_specs=[
            pl.BlockSpec((tm, k), lambda i, j: (i, 0), pipeline_mode=pl.Buffered(1)),
            pl.BlockSpec((1, k), lambda i, j: (0, 0)),
            pl.BlockSpec((None, None, 1, k), lambda i, j: (mod_row_fn(i), 0, 0, 0)),
            pl.BlockSpec((None, None, 1, k), lambda i, j: (mod_row_fn(i), 1, 0, 0)),
            pl.BlockSpec((k, tn), lambda i, j: (0, j)),
            pl.BlockSpec((1, tn), lambda i, j: (0, j)),
            pl.BlockSpec((None, tm, HEAD_DIM), lambda i, j: (i % npos, 0, 0)),
            pl.BlockSpec((None, tm, HEAD_DIM), lambda i, j: (i % npos, 0, 0)),
        ],
        out_specs=pl.BlockSpec((tm, tn), lambda i, j: (i, j)),
        out_shape=jax.ShapeDtypeStruct((r, n), BF16),
        scratch_shapes=[pltpu.VMEM((tm, k), BF16)],
        compiler_params=_cparams(("arbitrary", "arbitrary")),
    )(x, gain.reshape(1, k), mod, mod, w, colgain, cos, sin)


def _attend(q, kvs):
    ss = [_dot_nt(q, k) for k, _ in kvs]
    m = functools.reduce(jnp.maximum, [jnp.max(s, axis=-1, keepdims=True) for s in ss])
    ps = [jnp.exp(s - m) for s in ss]
    l = functools.reduce(lambda a, b: a + b, [jnp.sum(p, axis=-1, keepdims=True) for p in ps])
    o = functools.reduce(lambda a, b: a + b,
                         [_bdot(p.astype(BF16), v) for p, (_, v) in zip(ps, kvs)])
    return o / l


def _diff_lambda(lam_ref, lam_init):
    lp = lam_ref[...]
    s1 = jnp.sum(lp[0:1, :] * lp[1:2, :], axis=-1, keepdims=True)
    s2 = jnp.sum(lp[2:3, :] * lp[3:4, :], axis=-1, keepdims=True)
    return jnp.exp(s1) - jnp.exp(s2) + lam_init


def _diff_attn_kernel(*refs, lam_init, n_kv):
    lam_ref, subln_ref, q_ref = refs[:3]
    k_refs = refs[3:3 + n_kv]
    v_refs = refs[3 + n_kv:3 + 2 * n_kv]
    o_ref = refs[3 + 2 * n_kv]
    lam = _diff_lambda(lam_ref, lam_init)
    outs = []
    for t in range(2):
        cols = slice(t * HEAD_DIM, (t + 1) * HEAD_DIM)
        kvs = [(kr[:, cols], vr[...]) for kr, vr in zip(k_refs, v_refs)]
        outs.append(_attend(q_ref[:, cols], kvs))
    o = outs[0] - lam * outs[1]
    ms = jnp.mean(o * o, axis=-1, keepdims=True)
    o = o * lax.rsqrt(ms + NORM_EPS) * subln_ref[...] * (1.0 - lam_init)
    o_ref[...] = o.astype(BF16)


def _diff_attention(lam_params, subln, qkv_q, kv_sources, nb, nq, lam_init):
    heads = D_MODEL // (2 * HEAD_DIM)
    hw = 2 * HEAD_DIM
    tq = min(nq, 512)
    nqb = nq // tq
    n_kv = len(kv_sources)
    in_specs = [
        pl.BlockSpec((4, HEAD_DIM), lambda b, h, i: (0, 0)),
        pl.BlockSpec((1, hw), lambda b, h, i: (0, 0)),
        pl.BlockSpec((tq, hw), lambda b, h, i: (b * nqb + i, h)),
    ]
    args = [lam_params, subln.reshape(1, hw), qkv_q]
    for arr, nk in kv_sources:
        in_specs.append(pl.BlockSpec((nk, hw), lambda b, h, i: (b, heads + h)))
        args.append(arr)
    for arr, nk in kv_sources:
        in_specs.append(pl.BlockSpec((nk, hw), lambda b, h, i: (b, 2 * heads + h)))
        args.append(arr)
    kern = functools.partial(_diff_attn_kernel, lam_init=lam_init, n_kv=n_kv)
    return pl.pallas_call(
        kern,
        grid=(nb, heads, nqb),
        in_specs=in_specs,
        out_specs=pl.BlockSpec((tq, hw), lambda b, h, i: (b * nqb + i, h)),
        out_shape=jax.ShapeDtypeStruct((nb * nq, D_MODEL), BF16),
        compiler_params=_cparams(("arbitrary", "arbitrary", "arbitrary")),
    )(*args)


def _gqa_attn_kernel(q_ref, kl_ref, kc_ref, vl_ref, vc_ref, o_ref, *, group):
    kvs = [(kl_ref[...], vl_ref[...]), (kc_ref[...], vc_ref[...])]
    for g in range(group):
        cols = slice(g * HEAD_DIM, (g + 1) * HEAD_DIM)
        o_ref[:, cols] = _attend(q_ref[:, cols], kvs).astype(BF16)


def _gqa_attention(qkv_lat, qkv_ctx, nb, nq, nc):
    q_heads = D_MODEL // HEAD_DIM
    kv_heads = q_heads // 4
    group = q_heads // kv_heads
    gw = group * HEAD_DIM
    tq = 256
    nqb = nq // tq
    k_off = D_MODEL // HEAD_DIM
    v_off = k_off + kv_heads
    kern = functools.partial(_gqa_attn_kernel, group=group)
    return pl.pallas_call(
        kern,
        grid=(nb, kv_heads, nqb),
        in_specs=[
            pl.BlockSpec((tq, gw), lambda b, h, i: (b * nqb + i, h)),
            pl.BlockSpec((nq, HEAD_DIM), lambda b, h, i: (b, k_off + h)),
            pl.BlockSpec((nc, HEAD_DIM), lambda b, h, i: (b, k_off + h)),
            pl.BlockSpec((nq, HEAD_DIM), lambda b, h, i: (b, v_off + h)),
            pl.BlockSpec((nc, HEAD_DIM), lambda b, h, i: (b, v_off + h)),
        ],
        out_specs=pl.BlockSpec((tq, gw), lambda b, h, i: (b * nqb + i, h)),
        out_shape=jax.ShapeDtypeStruct((nb * nq, D_MODEL), BF16),
        compiler_params=_cparams(("arbitrary", "arbitrary", "arbitrary")),
    )(qkv_lat, qkv_lat, qkv_ctx, qkv_lat, qkv_ctx)


def _proj_res_kernel(o_ref, w_ref, res_ref, gate_ref, out_ref):
    acc = _bdot(o_ref[...], w_ref[...].astype(BF16))
    out_ref[...] = res_ref[...] + gate_ref[...] * acc


def _proj_residual(o, w, res, mod, mod_row_fn, gate_idx):
    r, k = o.shape
    n = w.shape[1]
    tm, tn = 1024, 512
    return pl.pallas_call(
        _proj_res_kernel,
        grid=(r // tm, n // tn),
        in_specs=[
            pl.BlockSpec((tm, k), lambda i, j: (i, 0)),
            pl.BlockSpec((k, tn), lambda i, j: (0, j)),
            pl.BlockSpec((tm, tn), lambda i, j: (i, j)),
            pl.BlockSpec((None, None, 1, tn), lambda i, j: (mod_row_fn(i), gate_idx, 0, j)),
        ],
        out_specs=pl.BlockSpec((tm, tn), lambda i, j: (i, j)),
        out_shape=jax.ShapeDtypeStruct((r, n), F32),
        compiler_params=_cparams(("arbitrary", "arbitrary")),
    )(o, w, res, mod)


def _route_kernel(x_ref, g_ref, sh_ref, sc_ref, rt_ref, tri_ref, h_ref, gate_ref, slot_ref, lg_scr,
                  *, tc, n, cap):
    c = pl.program_id(1)
    h = _norm_mod(x_ref[...], g_ref[...], sh_ref[...], sc_ref[...])
    h_ref[...] = h.astype(BF16)
    h_hi, h_lo = _split_bf16(h)
    r_hi, r_lo = _split_bf16(rt_ref[...])
    lg_scr[c] = _dot_nt(r_hi, h_hi) + _dot_nt(r_hi, h_lo) + _dot_nt(r_lo, h_hi)

    @pl.when(c == n // tc - 1)
    def _():
        lg = jnp.concatenate([lg_scr[i] for i in range(n // tc)], axis=1)
        mx = jnp.max(lg, axis=0, keepdims=True)
        ex = jnp.exp(lg - mx)
        aff = ex / jnp.sum(ex, axis=0, keepdims=True)
        bits = pltpu.bitcast(aff, jnp.int32)
        thr = jnp.zeros((N_EXPERTS, 1), jnp.int32)
        for bit in range(30, -1, -1):
            cand = thr | jnp.int32(1 << bit)
            cnt = jnp.sum(jnp.where(bits >= cand, 1.0, 0.0), axis=1, keepdims=True)
            thr = jnp.where(cnt >= cap, cand, thr)
        gt = bits > thr
        eq = bits == thr
        need = cap - jnp.sum(jnp.where(gt, 1.0, 0.0), axis=1, keepdims=True)
        tri = tri_ref[...]
        eq_rank = _bdot(jnp.where(eq, 1.0, 0.0).astype(BF16), tri)
        sel = gt | (eq & (eq_rank < need))
        slot = _bdot(jnp.where(sel, 1.0, 0.0).astype(BF16), tri)
        gate_ref[...] = jnp.where(sel, aff, 0.0)
        slot_ref[...] = jnp.where(sel, slot.astype(jnp.int32), -1)


def _route(x, gain, mod, mod_row_fn, router_t, tri, nb, n, cap):
    k = x.shape[1]
    tc = 256
    nch = n // tc
    kern = functools.partial(_route_kernel, tc=tc, n=n, cap=cap)
    return pl.pallas_call(
        kern,
        grid=(nb, nch),
        in_specs=[
            pl.BlockSpec((tc, k), lambda b, c: (b * nch + c, 0)),
            pl.BlockSpec((1, k), lambda b, c: (0, 0)),
            pl.BlockSpec((None, None, 1, k), lambda b, c: (mod_row_fn(b), 3, 0, 0)),
            pl.BlockSpec((None, None, 1, k), lambda b, c: (mod_row_fn(b), 4, 0, 0)),
            pl.BlockSpec((N_EXPERTS, k), lambda b, c: (0, 0)),
            pl.BlockSpec((n, n), lambda b, c: (0, 0)),
        ],
        out_specs=[
            pl.BlockSpec((tc, k), lambda b, c: (b * nch + c, 0)),
            pl.BlockSpec((None, N_EXPERTS, n), lambda b, c: (b, 0, 0)),
            pl.BlockSpec((None, N_EXPERTS, n), lambda b, c: (b, 0, 0)),
        ],
        out_shape=[
            jax.ShapeDtypeStruct((nb * n, k), BF16),
            jax.ShapeDtypeStruct((nb, N_EXPERTS, n), F32),
            jax.ShapeDtypeStruct((nb, N_EXPERTS, n), jnp.int32),
        ],
        scratch_shapes=[pltpu.VMEM((nch, N_EXPERTS, tc), F32)],
        compiler_params=_cparams(("arbitrary", "arbitrary")),
    )(x, gain.reshape(1, k), mod, mod, router_t, tri)


def _gather_kernel(slot_ref, h_ref, xs_ref, *, cap, n):
    onehot = lax.broadcasted_iota(jnp.int32, (cap, n), 0) == slot_ref[...]
    xs_ref[...] = _bdot(jnp.where(onehot, 1.0, 0.0).astype(BF16), h_ref[...]).astype(BF16)


def _gather(slot_t, h, nb, n, cap):
    k = h.shape[1]
    kern = functools.partial(_gather_kernel, cap=cap, n=n)
    return pl.pallas_call(
        kern,
        grid=(nb, N_EXPERTS),
        in_specs=[
            pl.BlockSpec((None, None, 1, n), lambda b, e: (b, e, 0, 0)),
            pl.BlockSpec((n, k), lambda b, e: (b, 0)),
        ],
        out_specs=pl.BlockSpec((None, cap, k), lambda b, e: (e, b, 0)),
        out_shape=jax.ShapeDtypeStruct((N_EXPERTS, nb * cap, k), BF16),
        compiler_params=_cparams(("arbitrary", "arbitrary")),
    )(slot_t.reshape(nb, N_EXPERTS, 1, n), h)


def _ffn_kernel(xs_ref, wg_ref, wu_ref, wd_ref, y_ref, hid_scr, *, nf, tf):
    s = pl.program_id(2)

    @pl.when(s < nf)
    def _():
        xs = xs_ref[...]
        a = _bdot(xs, wg_ref[...].astype(BF16))
        u = _bdot(xs, wu_ref[...].astype(BF16))
        hid_scr[s] = (_silu(a) * u).astype(BF16)

    @pl.when(s >= nf)
    def _():
        acc = _bdot(hid_scr[0], wd_ref[0:tf, :].astype(BF16))
        for f in range(1, nf):
            acc = acc + _bdot(hid_scr[f], wd_ref[f * tf:(f + 1) * tf, :].astype(BF16))
        y_ref[...] = acc.astype(BF16)


def _expert_ffn(xs, wg, wu, wd):
    e, m, k = xs.shape
    ff = wg.shape[2]
    tm = min(m, 1024)
    tf, td = 256, 512
    nf, nd = ff // tf, k // td
    kern = functools.partial(_ffn_kernel, nf=nf, tf=tf)
    return pl.pallas_call(
        kern,
        grid=(e, m // tm, nf + nd),
        in_specs=[
            pl.BlockSpec((None, tm, k), lambda ei, mi, s: (ei, mi, 0)),
            pl.BlockSpec((None, k, tf), lambda ei, mi, s: (ei, 0, jnp.minimum(s, nf - 1))),
            pl.BlockSpec((None, k, tf), lambda ei, mi, s: (ei, 0, jnp.minimum(s, nf - 1))),
            pl.BlockSpec((None, ff, td), lambda ei, mi, s: (ei, 0, jnp.maximum(s - nf, 0))),
        ],
        out_specs=pl.BlockSpec((None, tm, td), lambda ei, mi, s: (ei, mi, jnp.maximum(s - nf, 0))),
        out_shape=jax.ShapeDtypeStruct((e, m, k), BF16),
        scratch_shapes=[pltpu.VMEM((nf, tm, tf), BF16)],
        compiler_params=_cparams(("arbitrary", "arbitrary", "arbitrary")),
    )(xs, wg, wu, wd)


def _combine_kernel(slot_ref, gate_ref, y_ref, x_ref, g2_ref, o_ref, acc_scr, *, tt, cap):
    e = pl.program_id(2)

    @pl.when(e == 0)
    def _():
        acc_scr[...] = jnp.zeros_like(acc_scr)

    onehot = lax.broadcasted_iota(jnp.int32, (tt, cap), 1) == slot_ref[...]
    contrib = _bdot(jnp.where(onehot, 1.0, 0.0).astype(BF16), y_ref[...])
    acc_scr[...] += gate_ref[...] * contrib

    @pl.when(e == N_EXPERTS - 1)
    def _():
        o_ref[...] = x_ref[...] + g2_ref[...] * acc_scr[...]


def _combine(slot_c, gate_c, y, x, mod, mod_row_fn, nb, n, cap):
    k = x.shape[1]
    tt = min(n, 512)
    ntb = n // tt
    kern = functools.partial(_combine_kernel, tt=tt, cap=cap)
    return pl.pallas_call(
        kern,
        grid=(nb, ntb, N_EXPERTS),
        in_specs=[
            pl.BlockSpec((None, None, tt, 1), lambda b, t, e: (b, e, t, 0)),
            pl.BlockSpec((None, None, tt, 1), lambda b, t, e: (b, e, t, 0)),
            pl.BlockSpec((None, cap, k), lambda b, t, e: (e, b, 0)),
            pl.BlockSpec((tt, k), lambda b, t, e: (b * ntb + t, 0)),
            pl.BlockSpec((None, None, 1, k), lambda b, t, e: (mod_row_fn(b), 5, 0, 0)),
        ],
        out_specs=pl.BlockSpec((tt, k), lambda b, t, e: (b * ntb + t, 0)),
        out_shape=jax.ShapeDtypeStruct((nb * n, k), F32),
        scratch_shapes=[pltpu.VMEM((tt, k), F32)],
        compiler_params=_cparams(("arbitrary", "arbitrary", "arbitrary")),
    )(slot_c, gate_c, y, x, mod)


def _ec_moe_residual(x, gain, mod, mod_row_fn, router, wg, wu, wd, nb, n):
    cap = EC_CAPACITY_FACTOR * n // N_EXPERTS
    idx = jnp.arange(n, dtype=jnp.int32)
    tri = (idx[:, None] < idx[None, :]).astype(BF16)
    h, gate_t, slot_t = _route(x, gain, mod, mod_row_fn, router.T, tri, nb, n, cap)
    xs = _gather(slot_t, h, nb, n, cap)
    y = _expert_ffn(xs, wg, wu, wd)
    return _combine(slot_t[..., None], gate_t[..., None], y, x, mod, mod_row_fn, nb, n, cap)


def _rope_tables(n, tm):
    rows = n // GRID_W
    row = jnp.repeat(jnp.arange(rows, dtype=F32), GRID_W)
    col = jnp.tile(jnp.arange(GRID_W, dtype=F32), rows)
    axis_dim = HEAD_DIM // 2
    inv_freq = ROPE_THETA ** (-jnp.arange(0, axis_dim, 2, dtype=F32) / axis_dim)
    ang = jnp.concatenate([row[:, None] * inv_freq, col[:, None] * inv_freq], axis=-1)
    cos, sin = jnp.cos(ang), jnp.sin(ang)
    cos_full = jnp.concatenate([cos, cos], axis=-1).reshape(n // tm, tm, HEAD_DIM)
    sin_signed = jnp.concatenate([-sin, sin], axis=-1).reshape(n // tm, tm, HEAD_DIM)
    return cos_full, sin_signed


def kernel(x, c, ctx, c_ctx,
           mod_w_0, mod_b_0, norm1_0, norm2_0, wqkv_0, wo_0, qnorm_0, knorm_0,
           lam_q1_0, lam_k1_0, lam_q2_0, lam_k2_0, subln_0, router_0, we_gate_0, we_up_0, we_down_0,
           mod_w_1, mod_b_1, norm1_1, norm2_1, wqkv_1, wo_1, qnorm_1, knorm_1,
           router_1, we_gate_1, we_up_1, we_down_1):
    nb, nq, d = x.shape
    nc = ctx.shape[1]
    tm = 1024
    x_lat = x.reshape(nb * nq, d)
    x_ctx = ctx.reshape(nb * nc, d)
    cond = jnp.zeros((MOD_ROWS, d), F32).at[:nb].set(c).at[nb].set(c_ctx)
    lat_row = lambda i: i // (nq // tm)
    ctx_row = lambda i: nb
    lat_b = lambda b: b
    cos, sin = _rope_tables(nq, tm)
    cos_id = jnp.ones((1, tm, HEAD_DIM), F32)
    sin_id = jnp.zeros((1, tm, HEAD_DIM), F32)
    q_scale = 1.0 / math.sqrt(HEAD_DIM)

    mod0 = _adaln(cond, mod_w_0, mod_b_0)
    n_heads2 = d // HEAD_DIM
    colgain0 = jnp.concatenate([jnp.tile(qnorm_0 * q_scale, n_heads2), jnp.tile(knorm_0, n_heads2),
                                jnp.ones((d,), F32)]).reshape(1, 3 * d)
    qkv_lat = _qkv_proj(x_lat, norm1_0, mod0, lat_row, wqkv_0, colgain0, cos, sin, 2 * d)
    qkv_ctx = _qkv_proj(x_ctx, norm1_0, mod0, ctx_row, wqkv_0, colgain0, cos_id, sin_id, 2 * d)
    lam_init = 0.8 - 0.6 * math.exp(-0.3 * 0)
    lam_params = jnp.stack([lam_q1_0, lam_k1_0, lam_q2_0, lam_k2_0])
    o_lat = _diff_attention(lam_params, subln_0, qkv_lat, [(qkv_lat, nq), (qkv_ctx, nc)], nb, nq, lam_init)
    o_ctx = _diff_attention(lam_params, subln_0, qkv_ctx, [(qkv_ctx, nc)], nb, nc, lam_init)
    x_lat = _proj_residual(o_lat, wo_0, x_lat, mod0, lat_row, 2)
    x_ctx = _proj_residual(o_ctx, wo_0, x_ctx, mod0, ctx_row, 2)
    x_lat = _ec_moe_residual(x_lat, norm2_0, mod0, lat_b, router_0, we_gate_0, we_up_0, we_down_0, nb, nq)
    x_ctx = _ec_moe_residual(x_ctx, norm2_0, mod0, ctx_row, router_0, we_gate_0, we_up_0, we_down_0, nb, nc)

    mod1 = _adaln(cond, mod_w_1, mod_b_1)
    kv_dim = (wqkv_1.shape[1] - d) // 2
    colgain1 = jnp.concatenate([jnp.tile(qnorm_1 * q_scale, d // HEAD_DIM),
                                jnp.tile(knorm_1, kv_dim // HEAD_DIM),
                                jnp.ones((kv_dim,), F32)]).reshape(1, d + 2 * kv_dim)
    qkv_lat = _qkv_proj(x_lat, norm1_1, mod1, lat_row, wqkv_1, colgain1, cos, sin, d + kv_dim)
    qkv_ctx = _qkv_proj(x_ctx, norm1_1, mod1, ctx_row, wqkv_1, colgain1, cos_id, sin_id, d + kv_dim)
    o_lat = _gqa_attention(qkv_lat, qkv_ctx, nb, nq, nc)
    x_lat = _proj_residual(o_lat, wo_1, x_lat, mod1, lat_row, 2)
    x_lat = _ec_moe_residual(x_lat, norm2_1, mod1, lat_b, router_1, we_gate_1, we_up_1, we_down_1, nb, nq)
    return x_lat.reshape(nb, nq, d)
```

```python
import functools
import math

import jax
import jax.numpy as jnp
from jax import lax
from jax.experimental import pallas as pl
from jax.experimental.pallas import tpu as pltpu

F32 = jnp.float32
BF16 = jnp.bfloat16

D_MODEL = 4096
GRID_W = 64
HEAD_DIM = 128
N_EXPERTS = 16
EC_CAPACITY_FACTOR = 2
EXPERT_FF = D_MODEL // 4
ROPE_THETA = 10000.0
NORM_EPS = 1e-6
N_MOD = 6
MOD_ROWS = 16
LANES = 128
VMEM_LIMIT = 58 * 1024 * 1024


def _cparams(sem):
    return pltpu.CompilerParams(dimension_semantics=sem, vmem_limit_bytes=VMEM_LIMIT)


def _bdot(a, b):
    return jnp.dot(a, b, preferred_element_type=F32)


def _dot_nt(a, b):
    return lax.dot_general(a, b, (((1,), (1,)), ((), ())), preferred_element_type=F32)


def _silu(x):
    return x * (1.0 / (1.0 + jnp.exp(-x)))


def _split_bf16(x):
    hi = x.astype(BF16)
    lo = (x - hi.astype(F32)).astype(BF16)
    return hi, lo


def _adaln_kernel(c_ref, w_ref, b_ref, o_ref):
    s = _silu(c_ref[...]).astype(BF16)
    o_ref[...] = _bdot(s, w_ref[...].astype(BF16)) + b_ref[...]


def _adaln(cond, w, b):
    n = w.shape[1]
    tn = 512
    out = pl.pallas_call(
        _adaln_kernel,
        grid=(n // tn,),
        in_specs=[
            pl.BlockSpec((MOD_ROWS, D_MODEL), lambda j: (0, 0)),
            pl.BlockSpec((D_MODEL, tn), lambda j: (0, j)),
            pl.BlockSpec((1, tn), lambda j: (0, j)),
        ],
        out_specs=pl.BlockSpec((MOD_ROWS, tn), lambda j: (0, j)),
        out_shape=jax.ShapeDtypeStruct((MOD_ROWS, n), F32),
        compiler_params=_cparams(("arbitrary",)),
        name="adaln",
    )(cond, w, b.reshape(1, n))
    return out.reshape(MOD_ROWS, N_MOD, 1, D_MODEL)


def _norm_mod(xv, g, shift, scale):
    ms = jnp.mean(xv * xv, axis=-1, keepdims=True)
    y = xv * lax.rsqrt(ms + NORM_EPS) * g
    return y * (1.0 + scale) + shift


def _qkv_kernel(x_ref, g_ref, sh_ref, sc_ref, w_ref, cg_ref, cos_ref, sin_ref, o_ref, h_scr,
                *, tm, tn, n_qk_tiles, chunk, row_group):
    j = pl.program_id(1)

    @pl.when(j == 0)
    def _():
        def body(r, carry):
            rows = pl.ds(pl.multiple_of(r * chunk, chunk), chunk)
            h = _norm_mod(x_ref[rows, :], g_ref[...], sh_ref[...], sc_ref[...])
            h_scr[rows, :] = h.astype(BF16)
            return carry
        lax.fori_loop(0, tm // chunk, body, 0)

    @pl.when(j < n_qk_tiles)
    def _():
        w = w_ref[...].astype(BF16)
        for r in range(tm // row_group):
            rows = slice(r * row_group, (r + 1) * row_group)
            acc = _bdot(h_scr[rows, :], w)
            cos = cos_ref[rows, :]
            sin = sin_ref[rows, :]
            for gi in range(tn // HEAD_DIM):
                cols = slice(gi * HEAD_DIM, (gi + 1) * HEAD_DIM)
                a = acc[:, cols]
                ms = jnp.mean(a * a, axis=-1, keepdims=True)
                a = a * lax.rsqrt(ms + NORM_EPS) * cg_ref[:, cols]
                a = a * cos + pltpu.roll(a, HEAD_DIM // 2, axis=1) * sin
                o_ref[rows, cols] = a.astype(BF16)

    @pl.when(j >= n_qk_tiles)
    def _():
        o_ref[...] = _bdot(h_scr[...], w_ref[...].astype(BF16)).astype(BF16)


def _qkv_proj(x, gain, mod, mod_row_fn, w, colgain, cos, sin, n_qk_cols):
    r, k = x.shape
    n = w.shape[1]
    tm, tn = 1024, 512
    npos = cos.shape[0]
    kern = functools.partial(_qkv_kernel, tm=tm, tn=tn, n_qk_tiles=n_qk_cols // tn, chunk=32,
                             row_group=256)
    return pl.pallas_call(
        kern,
        grid=(r // tm, n // tn),
        in_specs=[
            pl.BlockSpec((tm, k), lambda i, j: (i, 0), pipeline_mode=pl.Buffered(1)),
            pl.BlockSpec((1, k), lambda i, j: (0, 0)),
            pl.BlockSpec((None, None, 1, k), lambda i, j: (mod_row_fn(i), 0, 0, 0)),
            pl.BlockSpec((None, None, 1, k), lambda i, j: (mod_row_fn(i), 1, 0, 0)),
            pl.BlockSpec((k, tn), lambda i, j: (0, j)),
            pl.BlockSpec((1, tn), lambda i, j: (0, j)),
            pl.BlockSpec((None, tm, HEAD_DIM), lambda i, j: (i % npos, 0, 0)),
            pl.BlockSpec((None, tm, HEAD_DIM), lambda i, j: (i % npos, 0, 0)),
        ],
        out_specs=pl.BlockSpec((tm, tn), lambda i, j: (i, j)),
        out_shape=jax.ShapeDtypeStruct((r, n), BF16),
        scratch_shapes=[pltpu.VMEM((tm, k), BF16)],
        compiler_params=_cparams(("arbitrary", "arbitrary")),
        name="qkv_proj",
    )(x, gain.reshape(1, k), mod, mod, w, colgain, cos, sin)


def _attend(q, kvs):
    ss = [_dot_nt(q, k) for k, _ in kvs]
    m = functools.reduce(jnp.maximum, [jnp.max(s, axis=-1, keepdims=True) for s in ss])
    ps = [jnp.exp2(s - m) for s in ss]
    l = functools.reduce(lambda a, b: a + b, [jnp.sum(p, axis=-1, keepdims=True) for p in ps])
    o = functools.reduce(lambda a, b: a + b,
                         [_bdot(p.astype(BF16), v) for p, (_, v) in zip(ps, kvs)])
    return o / l


def _diff_lambda(lam_ref, lam_init):
    lp = lam_ref[...]
    s1 = jnp.sum(lp[0:1, :] * lp[1:2, :], axis=-1, keepdims=True)
    s2 = jnp.sum(lp[2:3, :] * lp[3:4, :], axis=-1, keepdims=True)
    return jnp.exp(s1) - jnp.exp(s2) + lam_init


def _diff_attn_kernel(*refs, lam_init, n_kv):
    lam_ref, subln_ref, q_ref = refs[:3]
    k_refs = refs[3:3 + n_kv]
    v_refs = refs[3 + n_kv:3 + 2 * n_kv]
    o_ref = refs[3 + 2 * n_kv]
    lam = _diff_lambda(lam_ref, lam_init)
    outs = []
    for t in range(2):
        cols = slice(t * HEAD_DIM, (t + 1) * HEAD_DIM)
        kvs = [(kr[:, cols], vr[...]) for kr, vr in zip(k_refs, v_refs)]
        outs.append(_attend(q_ref[:, cols], kvs))
    o = outs[0] - lam * outs[1]
    ms = jnp.mean(o * o, axis=-1, keepdims=True)
    o = o * lax.rsqrt(ms + NORM_EPS) * subln_ref[...] * (1.0 - lam_init)
    o_ref[...] = o.astype(BF16)


def _diff_attention(lam_params, subln, qkv_q, kv_sources, nb, nq, lam_init):
    heads = D_MODEL // (2 * HEAD_DIM)
    hw = 2 * HEAD_DIM
    tq = min(nq, 512)
    nqb = nq // tq
    n_kv = len(kv_sources)
    in_specs = [
        pl.BlockSpec((4, HEAD_DIM), lambda b, h, i: (0, 0)),
        pl.BlockSpec((1, hw), lambda b, h, i: (0, 0)),
        pl.BlockSpec((tq, hw), lambda b, h, i: (b * nqb + i, h)),
    ]
    args = [lam_params, subln.reshape(1, hw), qkv_q]
    for arr, nk in kv_sources:
        in_specs.append(pl.BlockSpec((nk, hw), lambda b, h, i: (b, heads + h)))
        args.append(arr)
    for arr, nk in kv_sources:
        in_specs.append(pl.BlockSpec((nk, hw), lambda b, h, i: (b, 2 * heads + h)))
        args.append(arr)
    kern = functools.partial(_diff_attn_kernel, lam_init=lam_init, n_kv=n_kv)
    return pl.pallas_call(
        kern,
        grid=(nb, heads, nqb),
        in_specs=in_specs,
        out_specs=pl.BlockSpec((tq, hw), lambda b, h, i: (b * nqb + i, h)),
        out_shape=jax.ShapeDtypeStruct((nb * nq, D_MODEL), BF16),
        compiler_params=_cparams(("arbitrary", "arbitrary", "arbitrary")),
        name="diff_attn",
    )(*args)


def _gqa_attn_kernel(q_ref, kl_ref, kc_ref, vl_ref, vc_ref, o_ref, *, group):
    kvs = [(kl_ref[...], vl_ref[...]), (kc_ref[...], vc_ref[...])]
    for g in range(group):
        cols = slice(g * HEAD_DIM, (g + 1) * HEAD_DIM)
        o_ref[:, cols] = _attend(q_ref[:, cols], kvs).astype(BF16)


def _gqa_attention(qkv_lat, qkv_ctx, nb, nq, nc):
    q_heads = D_MODEL // HEAD_DIM
    kv_heads = q_heads // 4
    group = q_heads // kv_heads
    gw = group * HEAD_DIM
    tq = 256
    nqb = nq // tq
    k_off = D_MODEL // HEAD_DIM
    v_off = k_off + kv_heads
    kern = functools.partial(_gqa_attn_kernel, group=group)
    return pl.pallas_call(
        kern,
        grid=(nb, kv_heads, nqb),
        in_specs=[
            pl.BlockSpec((tq, gw), lambda b, h, i: (b * nqb + i, h)),
            pl.BlockSpec((nq, HEAD_DIM), lambda b, h, i: (b, k_off + h)),
            pl.BlockSpec((nc, HEAD_DIM), lambda b, h, i: (b, k_off + h)),
            pl.BlockSpec((nq, HEAD_DIM), lambda b, h, i: (b, v_off + h)),
            pl.BlockSpec((nc, HEAD_DIM), lambda b, h, i: (b, v_off + h)),
        ],
        out_specs=pl.BlockSpec((tq, gw), lambda b, h, i: (b * nqb + i, h)),
        out_shape=jax.ShapeDtypeStruct((nb * nq, D_MODEL), BF16),
        compiler_params=_cparams(("arbitrary", "arbitrary", "arbitrary")),
        name="gqa_attn",
    )(qkv_lat, qkv_lat, qkv_ctx, qkv_lat, qkv_ctx)


def _proj_res_kernel(o_ref, w_ref, res_ref, gate_ref, out_ref):
    acc = _bdot(o_ref[...], w_ref[...].astype(BF16))
    out_ref[...] = res_ref[...] + gate_ref[...] * acc


def _proj_residual(o, w, res, mod, mod_row_fn, gate_idx):
    r, k = o.shape
    n = w.shape[1]
    tm, tn = 1024, 512
    return pl.pallas_call(
        _proj_res_kernel,
        grid=(r // tm, n // tn),
        in_specs=[
            pl.BlockSpec((tm, k), lambda i, j: (i, 0)),
            pl.BlockSpec((k, tn), lambda i, j: (0, j)),
            pl.BlockSpec((tm, tn), lambda i, j: (i, j)),
            pl.BlockSpec((None, None, 1, tn), lambda i, j: (mod_row_fn(i), gate_idx, 0, j)),
        ],
        out_specs=pl.BlockSpec((tm, tn), lambda i, j: (i, j)),
        out_shape=jax.ShapeDtypeStruct((r, n), F32),
        compiler_params=_cparams(("arbitrary", "arbitrary")),
        name="out_proj",
    )(o, w, res, mod)


def _route_kernel(x_ref, g_ref, sh_ref, sc_ref, rt_ref, tri_ref, h_ref, gate_ref, slot_ref, lg_scr,
                  *, tc, n, cap):
    c = pl.program_id(1)
    h = _norm_mod(x_ref[...], g_ref[...], sh_ref[...], sc_ref[...])
    h_ref[...] = h.astype(BF16)
    h_hi, h_lo = _split_bf16(h)
    r_hi, r_lo = _split_bf16(rt_ref[...])
    lg_scr[c] = _dot_nt(r_hi, h_hi) + _dot_nt(r_hi, h_lo) + _dot_nt(r_lo, h_hi)

    @pl.when(c == n // tc - 1)
    def _():
        lg = jnp.concatenate([lg_scr[i] for i in range(n // tc)], axis=1)
        mx = jnp.max(lg, axis=0, keepdims=True)
        ex = jnp.exp(lg - mx)
        aff = ex / jnp.sum(ex, axis=0, keepdims=True)
        bits = pltpu.bitcast(aff, jnp.int32)
        thr = jnp.zeros((N_EXPERTS, 1), jnp.int32)
        for bit in range(30, -1, -1):
            cand = thr | jnp.int32(1 << bit)
            cnt = jnp.sum(jnp.where(bits >= cand, 1.0, 0.0), axis=1, keepdims=True)
            thr = jnp.where(cnt >= cap, cand, thr)
        gt = bits > thr
        eq = bits == thr
        need = cap - jnp.sum(jnp.where(gt, 1.0, 0.0), axis=1, keepdims=True)
        tri = tri_ref[...]
        eq_rank = _bdot(jnp.where(eq, 1.0, 0.0).astype(BF16), tri)
        sel = gt | (eq & (eq_rank < need))
        slot = _bdot(jnp.where(sel, 1.0, 0.0).astype(BF16), tri)
        gate_ref[...] = jnp.where(sel, aff, 0.0)
        slot_ref[...] = jnp.where(sel, slot.astype(jnp.int32), -1)


def _route(x, gain, mod, mod_row_fn, router_t, tri, nb, n, cap):
    k = x.shape[1]
    tc = 256
    nch = n // tc
    kern = functools.partial(_route_kernel, tc=tc, n=n, cap=cap)
    return pl.pallas_call(
        kern,
        grid=(nb, nch),
        in_specs=[
            pl.BlockSpec((tc, k), lambda b, c: (b * nch + c, 0)),
            pl.BlockSpec((1, k), lambda b, c: (0, 0)),
            pl.BlockSpec((None, None, 1, k), lambda b, c: (mod_row_fn(b), 3, 0, 0)),
            pl.BlockSpec((None, None, 1, k), lambda b, c: (mod_row_fn(b), 4, 0, 0)),
            pl.BlockSpec((N_EXPERTS, k), lambda b, c: (0, 0)),
            pl.BlockSpec((n, n), lambda b, c: (0, 0)),
        ],
        out_specs=[
            pl.BlockSpec((tc, k), lambda b, c: (b * nch + c, 0)),
            pl.BlockSpec((None, N_EXPERTS, n), lambda b, c: (b, 0, 0)),
            pl.BlockSpec((None, N_EXPERTS, n), lambda b, c: (b, 0, 0)),
        ],
        out_shape=[
            jax.ShapeDtypeStruct((nb * n, k), BF16),
            jax.ShapeDtypeStruct((nb, N_EXPERTS, n), F32),
            jax.ShapeDtypeStruct((nb, N_EXPERTS, n), jnp.int32),
        ],
        scratch_shapes=[pltpu.VMEM((nch, N_EXPERTS, tc), F32)],
        compiler_params=_cparams(("arbitrary", "arbitrary")),
        name="route",
    )(x, gain.reshape(1, k), mod, mod, router_t, tri)


def _gather_kernel(slot_ref, h_ref, xs_ref, *, cap, n):
    onehot = lax.broadcasted_iota(jnp.int32, (cap, n), 0) == slot_ref[...]
    xs_ref[...] = _bdot(jnp.where(onehot, 1.0, 0.0).astype(BF16), h_ref[...]).astype(BF16)


def _gather(slot_t, h, nb, n, cap):
    k = h.shape[1]
    kern = functools.partial(_gather_kernel, cap=cap, n=n)
    return pl.pallas_call(
        kern,
        grid=(nb, N_EXPERTS),
        in_specs=[
            pl.BlockSpec((None, None, 1, n), lambda b, e: (b, e, 0, 0)),
            pl.BlockSpec((n, k), lambda b, e: (b, 0)),
        ],
        out_specs=pl.BlockSpec((None, cap, k), lambda b, e: (e, b, 0)),
        out_shape=jax.ShapeDtypeStruct((N_EXPERTS, nb * cap, k), BF16),
        compiler_params=_cparams(("arbitrary", "arbitrary")),
        name="gather",
    )(slot_t.reshape(nb, N_EXPERTS, 1, n), h)


def _ffn_kernel(xs_ref, wg_ref, wu_ref, wd_ref, y_ref, hid_scr, *, nf, tf):
    s = pl.program_id(2)

    @pl.when(s < nf)
    def _():
        xs = xs_ref[...]
        a = _bdot(xs, wg_ref[...].astype(BF16))
        u = _bdot(xs, wu_ref[...].astype(BF16))
        hid_scr[s] = (_silu(a) * u).astype(BF16)

    @pl.when(s >= nf)
    def _():
        acc = _bdot(hid_scr[0], wd_ref[0:tf, :].astype(BF16))
        for f in range(1, nf):
            acc = acc + _bdot(hid_scr[f], wd_ref[f * tf:(f + 1) * tf, :].astype(BF16))
        y_ref[...] = acc.astype(BF16)


def _expert_ffn(xs, wg, wu, wd):
    e, m, k = xs.shape
    ff = wg.shape[2]
    tm = min(m, 1024)
    tf, td = 256, 1024
    nf, nd = ff // tf, k // td
    kern = functools.partial(_ffn_kernel, nf=nf, tf=tf)
    return pl.pallas_call(
        kern,
        grid=(e, m // tm, nf + nd),
        in_specs=[
            pl.BlockSpec((None, tm, k), lambda ei, mi, s: (ei, mi, 0)),
            pl.BlockSpec((None, k, tf), lambda ei, mi, s: (ei, 0, jnp.minimum(s, nf - 1))),
            pl.BlockSpec((None, k, tf), lambda ei, mi, s: (ei, 0, jnp.minimum(s, nf - 1))),
            pl.BlockSpec((None, ff, td), lambda ei, mi, s: (ei, 0, jnp.maximum(s - nf, 0))),
        ],
        out_specs=pl.BlockSpec((None, tm, td), lambda ei, mi, s: (ei, mi, jnp.maximum(s - nf, 0))),
        out_shape=jax.ShapeDtypeStruct((e, m, k), BF16),
        scratch_shapes=[pltpu.VMEM((nf, tm, tf), BF16)],
        compiler_params=_cparams(("arbitrary", "arbitrary", "arbitrary")),
        name="expert_ffn",
    )(xs, wg, wu, wd)


def _combine_kernel(slot_ref, gate_ref, y_ref, x_ref, g2_ref, o_ref, acc_scr, *, tt, cap):
    e = pl.program_id(2)

    @pl.when(e == 0)
    def _():
        acc_scr[...] = jnp.zeros_like(acc_scr)

    onehot = lax.broadcasted_iota(jnp.int32, (tt, cap), 1) == slot_ref[...]
    contrib = _bdot(jnp.where(onehot, 1.0, 0.0).astype(BF16), y_ref[...])
    acc_scr[...] += gate_ref[...] * contrib

    @pl.when(e == N_EXPERTS - 1)
    def _():
        o_ref[...] = x_ref[...] + g2_ref[...] * acc_scr[...]


def _combine(slot_c, gate_c, y, x, mod, mod_row_fn, nb, n, cap):
    k = x.shape[1]
    tt = min(n, 512)
    ntb = n // tt
    kern = functools.partial(_combine_kernel, tt=tt, cap=cap)
    return pl.pallas_call(
        kern,
        grid=(nb, ntb, N_EXPERTS),
        in_specs=[
            pl.BlockSpec((None, None, tt, 1), lambda b, t, e: (b, e, t, 0)),
            pl.BlockSpec((None, None, tt, 1), lambda b, t, e: (b, e, t, 0)),
            pl.BlockSpec((None, cap, k), lambda b, t, e: (e, b, 0)),
            pl.BlockSpec((tt, k), lambda b, t, e: (b * ntb + t, 0)),
            pl.BlockSpec((None, None, 1, k), lambda b, t, e: (mod_row_fn(b), 5, 0, 0)),
        ],
        out_specs=pl.BlockSpec((tt, k), lambda b, t, e: (b * ntb + t, 0)),
        out_shape=jax.ShapeDtypeStruct((nb * n, k), F32),
        scratch_shapes=[pltpu.VMEM((tt, k), F32)],
        compiler_params=_cparams(("arbitrary", "arbitrary", "arbitrary")),
        name="combine",
    )(slot_c, gate_c, y, x, mod)


def _ec_moe_residual(x, gain, mod, mod_row_fn, router, wg, wu, wd, nb, n):
    cap = EC_CAPACITY_FACTOR * n // N_EXPERTS
    idx = jnp.arange(n, dtype=jnp.int32)
    tri = (idx[:, None] < idx[None, :]).astype(BF16)
    h, gate_t, slot_t = _route(x, gain, mod, mod_row_fn, router.T, tri, nb, n, cap)
    xs = _gather(slot_t, h, nb, n, cap)
    y = _expert_ffn(xs, wg, wu, wd)
    return _combine(slot_t[..., None], gate_t[..., None], y, x, mod, mod_row_fn, nb, n, cap)


def _rope_tables(n, tm):
    rows = n // GRID_W
    row = jnp.repeat(jnp.arange(rows, dtype=F32), GRID_W)
    col = jnp.tile(jnp.arange(GRID_W, dtype=F32), rows)
    axis_dim = HEAD_DIM // 2
    inv_freq = ROPE_THETA ** (-jnp.arange(0, axis_dim, 2, dtype=F32) / axis_dim)
    ang = jnp.concatenate([row[:, None] * inv_freq, col[:, None] * inv_freq], axis=-1)
    cos, sin = jnp.cos(ang), jnp.sin(ang)
    cos_full = jnp.concatenate([cos, cos], axis=-1).reshape(n // tm, tm, HEAD_DIM)
    sin_signed = jnp.concatenate([-sin, sin], axis=-1).reshape(n // tm, tm, HEAD_DIM)
    return cos_full, sin_signed


def kernel(x, c, ctx, c_ctx,
           mod_w_0, mod_b_0, norm1_0, norm2_0, wqkv_0, wo_0, qnorm_0, knorm_0,
           lam_q1_0, lam_k1_0, lam_q2_0, lam_k2_0, subln_0, router_0, we_gate_0, we_up_0, we_down_0,
           mod_w_1, mod_b_1, norm1_1, norm2_1, wqkv_1, wo_1, qnorm_1, knorm_1,
           router_1, we_gate_1, we_up_1, we_down_1):
    nb, nq, d = x.shape
    nc = ctx.shape[1]
    tm = 1024
    x_lat = x.reshape(nb * nq, d)
    x_ctx = ctx.reshape(nb * nc, d)
    cond = jnp.zeros((MOD_ROWS, d), F32).at[:nb].set(c).at[nb].set(c_ctx)
    lat_row = lambda i: i // (nq // tm)
    ctx_row = lambda i: nb
    lat_b = lambda b: b
    cos, sin = _rope_tables(nq, tm)
    cos_id = jnp.ones((1, tm, HEAD_DIM), F32)
    sin_id = jnp.zeros((1, tm, HEAD_DIM), F32)
    q_scale = math.log2(math.e) / math.sqrt(HEAD_DIM)

    mod0 = _adaln(cond, mod_w_0, mod_b_0)
    n_heads2 = d // HEAD_DIM
    colgain0 = jnp.concatenate([jnp.tile(qnorm_0 * q_scale, n_heads2), jnp.tile(knorm_0, n_heads2),
                                jnp.ones((d,), F32)]).reshape(1, 3 * d)
    qkv_lat = _qkv_proj(x_lat, norm1_0, mod0, lat_row, wqkv_0, colgain0, cos, sin, 2 * d)
    qkv_ctx = _qkv_proj(x_ctx, norm1_0, mod0, ctx_row, wqkv_0, colgain0, cos_id, sin_id, 2 * d)
    lam_init = 0.8 - 0.6 * math.exp(-0.3 * 0)
    lam_params = jnp.stack([lam_q1_0, lam_k1_0, lam_q2_0, lam_k2_0])
    o_lat = _diff_attention(lam_params, subln_0, qkv_lat, [(qkv_lat, nq), (qkv_ctx, nc)], nb, nq, lam_init)
    o_ctx = _diff_attention(lam_params, subln_0, qkv_ctx, [(qkv_ctx, nc)], nb, nc, lam_init)
    x_lat = _proj_residual(o_lat, wo_0, x_lat, mod0, lat_row, 2)
    x_ctx = _proj_residual(o_ctx, wo_0, x_ctx, mod0, ctx_row, 2)
    x_lat = _ec_moe_residual(x_lat, norm2_0, mod0, lat_b, router_0, we_gate_0, we_up_0, we_down_0, nb, nq)
    x_ctx = _ec_moe_residual(x_ctx, norm2_0, mod0, ctx_row, router_0, we_gate_0, we_up_0, we_down_0, nb, nc)

    mod1 = _adaln(cond, mod_w_1, mod_b_1)
    kv_dim = (wqkv_1.shape[1] - d) // 2
    colgain1 = jnp.concatenate([jnp.tile(qnorm_1 * q_scale, d // HEAD_DIM),
                                jnp.tile(knorm_1, kv_dim // HEAD_DIM),
                                jnp.ones((kv_dim,), F32)]).reshape(1, d + 2 * kv_dim)
    qkv_lat = _qkv_proj(x_lat, norm1_1, mod1, lat_row, wqkv_1, colgain1, cos, sin, d + kv_dim)
    qkv_ctx = _qkv_proj(x_ctx, norm1_1, mod1, ctx_row, wqkv_1, colgain1, cos_id, sin_id, d + kv_dim)
    o_lat = _gqa_attention(qkv_lat, qkv_ctx, nb, nq, nc)
    x_lat = _proj_residual(o_lat, wo_1, x_lat, mod1, lat_row, 2)
    x_lat = _ec_moe_residual(x_lat, norm2_1, mod1, lat_b, router_1, we_gate_1, we_up_1, we_down_1, nb, nq)
    return x_lat.reshape(nb, nq, d)
```

```python
import functools
import math

import jax
import jax.numpy as jnp
from jax import lax
from jax.experimental import pallas as pl
from jax.experimental.pallas import tpu as pltpu

F32 = jnp.float32
BF16 = jnp.bfloat16
U32 = jnp.uint32
I32 = jnp.int32

D_MODEL = 4096
D_HALF = D_MODEL // 2
GRID_W = 64
HEAD_DIM = 128
N_EXPERTS = 16
EC_CAPACITY_FACTOR = 2
EXPERT_FF = D_MODEL // 4
ROPE_THETA = 10000.0
NORM_EPS = 1e-6
N_MOD = 6
MOD_ROWS = 16
SUBLANES = 8
META_ROWS = 8
VMEM_LIMIT = 58 * 1024 * 1024


def _cparams(sem):
    return pltpu.CompilerParams(dimension_semantics=sem, vmem_limit_bytes=VMEM_LIMIT)


def _bdot(a, b):
    return jnp.dot(a, b, preferred_element_type=F32)


def _dot_nt(a, b):
    return lax.dot_general(a, b, (((1,), (1,)), ((), ())), preferred_element_type=F32)


def _silu(x):
    return x * (1.0 / (1.0 + jnp.exp(-x)))


def _split_bf16(x):
    hi = x.astype(BF16)
    lo = (x - hi.astype(F32)).astype(BF16)
    return hi, lo


def _pack_pair(hi, lo):
    hb = pltpu.bitcast(hi.astype(BF16).astype(F32), U32)
    lb = pltpu.bitcast(lo.astype(BF16).astype(F32), U32)
    return (hb & jnp.uint32(0xFFFF0000)) | (lb >> 16)


def _unpack_pair(w):
    hi = pltpu.bitcast(w & jnp.uint32(0xFFFF0000), F32).astype(BF16)
    lo = pltpu.bitcast(w << 16, F32).astype(BF16)
    return hi, lo


def _adaln_kernel(c_ref, w_ref, b_ref, o_ref):
    s = _silu(c_ref[...]).astype(BF16)
    o_ref[...] = _bdot(s, w_ref[...].astype(BF16)) + b_ref[...]


def _adaln(cond, w, b):
    n = w.shape[1]
    tn = 512
    out = pl.pallas_call(
        _adaln_kernel,
        grid=(n // tn,),
        in_specs=[
            pl.BlockSpec((MOD_ROWS, D_MODEL), lambda j: (0, 0)),
            pl.BlockSpec((D_MODEL, tn), lambda j: (0, j)),
            pl.BlockSpec((1, tn), lambda j: (0, j)),
        ],
        out_specs=pl.BlockSpec((MOD_ROWS, tn), lambda j: (0, j)),
        out_shape=jax.ShapeDtypeStruct((MOD_ROWS, n), F32),
        compiler_params=_cparams(("arbitrary",)),
        name="adaln",
    )(cond, w, b.reshape(1, n))
    return out.reshape(MOD_ROWS, N_MOD, 1, D_MODEL)


def _norm_mod(xv, g, shift, scale):
    ms = jnp.mean(xv * xv, axis=-1, keepdims=True)
    y = xv * lax.rsqrt(ms + NORM_EPS) * g
    return y * (1.0 + scale) + shift


def _qkv_kernel(x_ref, g_ref, sh_ref, sc_ref, w_ref, cg_ref, cos_ref, sin_ref, o_ref, h_scr,
                *, tm, tn, n_qk_tiles, chunk, row_group):
    j = pl.program_id(1)

    @pl.when(j == 0)
    def _():
        def body(r, carry):
            rows = pl.ds(pl.multiple_of(r * chunk, chunk), chunk)
            h = _norm_mod(x_ref[rows, :], g_ref[...], sh_ref[...], sc_ref[...])
            h_scr[rows, :] = h.astype(BF16)
            return carry
        lax.fori_loop(0, tm // chunk, body, 0)

    @pl.when(j < n_qk_tiles)
    def _():
        w = w_ref[...].astype(BF16)
        for r in range(tm // row_group):
            rows = slice(r * row_group, (r + 1) * row_group)
            acc = _bdot(h_scr[rows, :], w)
            cos = cos_ref[rows, :]
            sin = sin_ref[rows, :]
            for gi in range(tn // HEAD_DIM):
                cols = slice(gi * HEAD_DIM, (gi + 1) * HEAD_DIM)
                a = acc[:, cols]
                ms = jnp.mean(a * a, axis=-1, keepdims=True)
                a = a * lax.rsqrt(ms + NORM_EPS) * cg_ref[:, cols]
                a = a * cos + pltpu.roll(a, HEAD_DIM // 2, axis=1) * sin
                o_ref[rows, cols] = a.astype(BF16)

    @pl.when(j >= n_qk_tiles)
    def _():
        o_ref[...] = _bdot(h_scr[...], w_ref[...].astype(BF16)).astype(BF16)


def _qkv_proj(x, gain, mod, mod_row_fn, w, colgain, cos, sin, n_qk_cols):
    r, k = x.shape
    n = w.shape[1]
    tm, tn = 1024, 512
    npos = cos.shape[0]
    kern = functools.partial(_qkv_kernel, tm=tm, tn=tn, n_qk_tiles=n_qk_cols // tn, chunk=32,
                             row_group=256)
    return pl.pallas_call(
        kern,
        grid=(r // tm, n // tn),
        in_specs=[
            pl.BlockSpec((tm, k), lambda i, j: (i, 0), pipeline_mode=pl.Buffered(1)),
            pl.BlockSpec((1, k), lambda i, j: (0, 0)),
            pl.BlockSpec((None, None, 1, k), lambda i, j: (mod_row_fn(i), 0, 0, 0)),
            pl.BlockSpec((None, None, 1, k), lambda i, j: (mod_row_fn(i), 1, 0, 0)),
            pl.BlockSpec((k, tn), lambda i, j: (0, j)),
            pl.BlockSpec((1, tn), lambda i, j: (0, j)),
            pl.BlockSpec((None, tm, HEAD_DIM), lambda i, j: (i % npos, 0, 0)),
            pl.BlockSpec((None, tm, HEAD_DIM), lambda i, j: (i % npos, 0, 0)),
        ],
        out_specs=pl.BlockSpec((tm, tn), lambda i, j: (i, j)),
        out_shape=jax.ShapeDtypeStruct((r, n), BF16),
        scratch_shapes=[pltpu.VMEM((tm, k), BF16)],
        compiler_params=_cparams(("arbitrary", "arbitrary")),
        name="qkv_proj",
    )(x, gain.reshape(1, k), mod, mod, w, colgain, cos, sin)


def _attend(q, kvs):
    ss = [_dot_nt(q, k) for k, _ in kvs]
    m = functools.reduce(jnp.maximum, [jnp.max(s, axis=-1, keepdims=True) for s in ss])
    ps = [jnp.exp2(s - m) for s in ss]
    l = functools.reduce(lambda a, b: a + b, [jnp.sum(p, axis=-1, keepdims=True) for p in ps])
    o = functools.reduce(lambda a, b: a + b,
                         [_bdot(p.astype(BF16), v) for p, (_, v) in zip(ps, kvs)])
    return o / l


def _diff_lambda(lam_ref, lam_init):
    lp = lam_ref[...]
    s1 = jnp.sum(lp[0:1, :] * lp[1:2, :], axis=-1, keepdims=True)
    s2 = jnp.sum(lp[2:3, :] * lp[3:4, :], axis=-1, keepdims=True)
    return jnp.exp(s1) - jnp.exp(s2) + lam_init


def _diff_attn_kernel(*refs, lam_init, n_kv):
    lam_ref, subln_ref, q_ref = refs[:3]
    k_refs = refs[3:3 + n_kv]
    v_refs = refs[3 + n_kv:3 + 2 * n_kv]
    o_ref = refs[3 + 2 * n_kv]
    lam = _diff_lambda(lam_ref, lam_init)
    outs = []
    for t in range(2):
        cols = slice(t * HEAD_DIM, (t + 1) * HEAD_DIM)
        kvs = [(kr[:, cols], vr[...]) for kr, vr in zip(k_refs, v_refs)]
        outs.append(_attend(q_ref[:, cols], kvs))
    o = outs[0] - lam * outs[1]
    ms = jnp.mean(o * o, axis=-1, keepdims=True)
    o = o * lax.rsqrt(ms + NORM_EPS) * subln_ref[...] * (1.0 - lam_init)
    o_ref[...] = o.astype(BF16)


def _diff_attention(lam_params, subln, qkv_q, kv_sources, nb, nq, lam_init):
    heads = D_MODEL // (2 * HEAD_DIM)
    hw = 2 * HEAD_DIM
    tq = min(nq, 512)
    nqb = nq // tq
    n_kv = len(kv_sources)
    in_specs = [
        pl.BlockSpec((4, HEAD_DIM), lambda b, h, i: (0, 0)),
        pl.BlockSpec((1, hw), lambda b, h, i: (0, 0)),
        pl.BlockSpec((tq, hw), lambda b, h, i: (b * nqb + i, h)),
    ]
    args = [lam_params, subln.reshape(1, hw), qkv_q]
    for arr, nk in kv_sources:
        in_specs.append(pl.BlockSpec((nk, hw), lambda b, h, i: (b, heads + h)))
        args.append(arr)
    for arr, nk in kv_sources:
        in_specs.append(pl.BlockSpec((nk, hw), lambda b, h, i: (b, 2 * heads + h)))
        args.append(arr)
    kern = functools.partial(_diff_attn_kernel, lam_init=lam_init, n_kv=n_kv)
    return pl.pallas_call(
        kern,
        grid=(nb, heads, nqb),
        in_specs=in_specs,
        out_specs=pl.BlockSpec((tq, hw), lambda b, h, i: (b * nqb + i, h)),
        out_shape=jax.ShapeDtypeStruct((nb * nq, D_MODEL), BF16),
        compiler_params=_cparams(("arbitrary", "arbitrary", "arbitrary")),
        name="diff_attn",
    )(*args)


def _gqa_attn_kernel(q_ref, kl_ref, kc_ref, vl_ref, vc_ref, o_ref, *, group):
    kvs = [(kl_ref[...], vl_ref[...]), (kc_ref[...], vc_ref[...])]
    for g in range(group):
        cols = slice(g * HEAD_DIM, (g + 1) * HEAD_DIM)
        o_ref[:, cols] = _attend(q_ref[:, cols], kvs).astype(BF16)


def _gqa_attention(qkv_lat, qkv_ctx, nb, nq, nc):
    q_heads = D_MODEL // HEAD_DIM
    kv_heads = q_heads // 4
    group = q_heads // kv_heads
    gw = group * HEAD_DIM
    tq = 256
    nqb = nq // tq
    k_off = D_MODEL // HEAD_DIM
    v_off = k_off + kv_heads
    kern = functools.partial(_gqa_attn_kernel, group=group)
    return pl.pallas_call(
        kern,
        grid=(nb, kv_heads, nqb),
        in_specs=[
            pl.BlockSpec((tq, gw), lambda b, h, i: (b * nqb + i, h)),
            pl.BlockSpec((nq, HEAD_DIM), lambda b, h, i: (b, k_off + h)),
            pl.BlockSpec((nc, HEAD_DIM), lambda b, h, i: (b, k_off + h)),
            pl.BlockSpec((nq, HEAD_DIM), lambda b, h, i: (b, v_off + h)),
            pl.BlockSpec((nc, HEAD_DIM), lambda b, h, i: (b, v_off + h)),
        ],
        out_specs=pl.BlockSpec((tq, gw), lambda b, h, i: (b * nqb + i, h)),
        out_shape=jax.ShapeDtypeStruct((nb * nq, D_MODEL), BF16),
        compiler_params=_cparams(("arbitrary", "arbitrary", "arbitrary")),
        name="gqa_attn",
    )(qkv_lat, qkv_lat, qkv_ctx, qkv_lat, qkv_ctx)


def _proj_res_kernel(o_ref, w_ref, res_ref, gate_ref, out_ref):
    acc = _bdot(o_ref[...], w_ref[...].astype(BF16))
    out_ref[...] = res_ref[...] + gate_ref[...] * acc


def _proj_residual(o, w, res, mod, mod_row_fn, gate_idx):
    r, k = o.shape
    n = w.shape[1]
    tm, tn = 1024, 512
    return pl.pallas_call(
        _proj_res_kernel,
        grid=(r // tm, n // tn),
        in_specs=[
            pl.BlockSpec((tm, k), lambda i, j: (i, 0)),
            pl.BlockSpec((k, tn), lambda i, j: (0, j)),
            pl.BlockSpec((tm, tn), lambda i, j: (i, j)),
            pl.BlockSpec((None, None, 1, tn), lambda i, j: (mod_row_fn(i), gate_idx, 0, j)),
        ],
        out_specs=pl.BlockSpec((tm, tn), lambda i, j: (i, j)),
        out_shape=jax.ShapeDtypeStruct((r, n), F32),
        compiler_params=_cparams(("arbitrary", "arbitrary")),
        name="out_proj",
    )(o, w, res, mod)


def _route_kernel(x_ref, g_ref, sh_ref, sc_ref, rt_ref, tri_ref, etri_ref, hp_ref, meta_ref, oc_ref, lg_scr,
                  *, tc, n, cap):
    c = pl.program_id(1)
    h = _norm_mod(x_ref[...], g_ref[...], sh_ref[...], sc_ref[...])
    hp_ref[...] = _pack_pair(h[:, :D_HALF], h[:, D_HALF:])
    h_hi, h_lo = _split_bf16(h)
    r_hi, r_lo = _split_bf16(rt_ref[...])
    lg_scr[c] = _dot_nt(r_hi, h_hi) + _dot_nt(r_hi, h_lo) + _dot_nt(r_lo, h_hi)

    @pl.when(c == n // tc - 1)
    def _():
        lg = jnp.concatenate([lg_scr[i] for i in range(n // tc)], axis=1)
        mx = jnp.max(lg, axis=0, keepdims=True)
        ex = jnp.exp(lg - mx)
        aff = ex / jnp.sum(ex, axis=0, keepdims=True)
        bits = pltpu.bitcast(aff, I32)
        thr = jnp.zeros((N_EXPERTS, 1), I32)
        for bit in range(30, -1, -1):
            cand = thr | jnp.int32(1 << bit)
            cnt = jnp.sum(jnp.where(bits >= cand, 1.0, 0.0), axis=1, keepdims=True)
            thr = jnp.where(cnt >= cap, cand, thr)
        gt = bits > thr
        eq = bits == thr
        need = cap - jnp.sum(jnp.where(gt, 1.0, 0.0), axis=1, keepdims=True)
        tri = tri_ref[...]
        eq_rank = _bdot(jnp.where(eq, 1.0, 0.0).astype(BF16), tri)
        sel = gt | (eq & (eq_rank < need))
        self_ = jnp.where(sel, 1.0, 0.0)
        selb = self_.astype(BF16)
        slot = _bdot(selb, tri).astype(I32)
        cnt_tok = jnp.sum(self_, axis=0, keepdims=True)
        off_tok = _bdot(jnp.broadcast_to(cnt_tok, (SUBLANES, n)).astype(BF16), tri)[0:1]
        pos = off_tok + _bdot(etri_ref[...], selb)
        oc_ref[...] = jnp.concatenate(
            [off_tok, cnt_tok, jnp.zeros((SUBLANES - 2, n), F32)], axis=0)

        tok = lax.broadcasted_iota(I32, (1, n), 1).astype(F32)
        tok_hi = jnp.floor(tok * (1.0 / 256.0))
        tok_lo = tok - 256.0 * tok_hi
        pos_hi = jnp.floor(pos * (1.0 / 256.0))
        pos_lo = pos - 256.0 * pos_hi
        g_hi = aff.astype(BF16).astype(F32)
        g_mid = (aff - g_hi).astype(BF16).astype(F32)
        g_lo = aff - g_hi - g_mid
        slot_iota = lax.broadcasted_iota(I32, (cap, n), 0)
        zero_row = jnp.zeros((1, n), F32)
        for e in range(N_EXPERTS):
            row = slice(e, e + 1)
            onehot = jnp.where((slot_iota == slot[row]) & (self_[row] > 0.5), 1.0, 0.0).astype(BF16)
            vals = jnp.concatenate([tok_hi, tok_lo, pos_hi[row], pos_lo[row],
                                    g_hi[row], g_mid[row], g_lo[row], zero_row], axis=0)
            meta_ref[e] = _dot_nt(vals.astype(BF16), onehot)


def _route(x, gain, mod, mod_row_fn, router_t, tri, etri, nb, n, cap):
    k = x.shape[1]
    tc = 256
    nch = n // tc
    kern = functools.partial(_route_kernel, tc=tc, n=n, cap=cap)
    return pl.pallas_call(
        kern,
        grid=(nb, nch),
        in_specs=[
            pl.BlockSpec((tc, k), lambda b, c: (b * nch + c, 0)),
            pl.BlockSpec((1, k), lambda b, c: (0, 0)),
            pl.BlockSpec((None, None, 1, k), lambda b, c: (mod_row_fn(b), 3, 0, 0)),
            pl.BlockSpec((None, None, 1, k), lambda b, c: (mod_row_fn(b), 4, 0, 0)),
            pl.BlockSpec((N_EXPERTS, k), lambda b, c: (0, 0)),
            pl.BlockSpec((n, n), lambda b, c: (0, 0)),
            pl.BlockSpec((N_EXPERTS, N_EXPERTS), lambda b, c: (0, 0)),
        ],
        out_specs=[
            pl.BlockSpec((tc, D_HALF), lambda b, c: (b * nch + c, 0)),
            pl.BlockSpec((None, N_EXPERTS, META_ROWS, cap), lambda b, c: (b, 0, 0, 0)),
            pl.BlockSpec((None, SUBLANES, n), lambda b, c: (b, 0, 0)),
        ],
        out_shape=[
            jax.ShapeDtypeStruct((nb * n, D_HALF), U32),
            jax.ShapeDtypeStruct((nb, N_EXPERTS, META_ROWS, cap), F32),
            jax.ShapeDtypeStruct((nb, SUBLANES, n), F32),
        ],
        scratch_shapes=[pltpu.VMEM((nch, N_EXPERTS, tc), F32)],
        compiler_params=_cparams(("arbitrary", "arbitrary")),
        name="route",
    )(x, gain.reshape(1, k), mod, mod, router_t, tri, etri)


def _permute_kernel(*refs, rows, group_ends):
    sidx_ref, didx_ref = refs[:2]
    src_refs = refs[2:2 + len(group_ends)]
    out_ref, sem = refs[-2:]
    g = pl.program_id(0)

    lo = 0
    for which, hi in enumerate(group_ends):
        src_ref = src_refs[which]

        @pl.when((g >= lo) & (g < hi))
        def _(src_ref=src_ref):
            def issue(i, carry):
                pltpu.make_async_copy(src_ref.at[pl.ds(sidx_ref[0, i], 1)],
                                      out_ref.at[pl.ds(didx_ref[0, i], 1)], sem).start()
                return carry
            lax.fori_loop(0, rows, issue, 0, unroll=8)
        lo = hi

    def drain(i, carry):
        pltpu.make_async_copy(src_refs[0].at[pl.ds(0, 1)], out_ref.at[pl.ds(0, 1)], sem).wait()
        return carry
    lax.fori_loop(0, rows, drain, 0, unroll=8)


def _permute_rows(parts, out_rows, rows=2048):
    width = parts[0][2].shape[1]
    sidx = jnp.concatenate([p[0].reshape(-1, 1, rows) for p in parts], axis=0)
    didx = jnp.concatenate([p[1].reshape(-1, 1, rows) for p in parts], axis=0)
    group_ends, total = [], 0
    for p in parts:
        total += p[0].size // rows
        group_ends.append(total)
    idx_spec = pl.BlockSpec((None, 1, rows), lambda i: (i, 0, 0), memory_space=pltpu.SMEM)
    return pl.pallas_call(
        functools.partial(_permute_kernel, rows=rows, group_ends=tuple(group_ends)),
        grid=(total,),
        in_specs=[idx_spec, idx_spec] + [pl.BlockSpec(memory_space=pl.ANY)] * len(parts),
        out_specs=pl.BlockSpec(memory_space=pl.ANY),
        out_shape=jax.ShapeDtypeStruct((out_rows, width), parts[0][2].dtype),
        scratch_shapes=[pltpu.SemaphoreType.DMA(())],
        compiler_params=_cparams(("arbitrary",)),
        name="permute_rows",
    )(sidx, didx, *[p[2] for p in parts])


def _ffn_kernel(xs_ref, gate_ref, wg_ref, wu_ref, wd_hi_ref, wd_lo_ref, y_ref, xs_scr, hid_scr,
                *, nf, tf, tm, chunk):
    s = pl.program_id(2)

    @pl.when(s == 0)
    def _():
        def body(r, carry):
            rows = pl.ds(pl.multiple_of(r * chunk, chunk), chunk)
            hi, lo = _unpack_pair(xs_ref[rows, :])
            xs_scr[rows, 0:D_HALF] = hi
            xs_scr[rows, D_HALF:D_MODEL] = lo
            return carry
        lax.fori_loop(0, tm // chunk, body, 0)

    @pl.when(s < nf)
    def _():
        xs = xs_scr[...]
        a = _bdot(xs, wg_ref[...].astype(BF16))
        u = _bdot(xs, wu_ref[...].astype(BF16))
        hid_scr[s] = (_silu(a) * u * gate_ref[...]).astype(BF16)

    @pl.when(s >= nf)
    def _():
        def down(w_ref):
            acc = _bdot(hid_scr[0], w_ref[0:tf, :].astype(BF16))
            for f in range(1, nf):
                acc = acc + _bdot(hid_scr[f], w_ref[f * tf:(f + 1) * tf, :].astype(BF16))
            return acc
        y_ref[...] = _pack_pair(down(wd_hi_ref), down(wd_lo_ref))


def _expert_ffn(xs, gate_col, wg, wu, wd, tm):
    e, m, kh = xs.shape
    k = 2 * kh
    ff = wg.shape[2]
    tf, td = 256, 512
    nf, nd = ff // tf, kh // td
    kern = functools.partial(_ffn_kernel, nf=nf, tf=tf, tm=tm, chunk=128)
    up_idx = lambda ei, mi, s: (ei, 0, jnp.minimum(s, nf - 1))
    return pl.pallas_call(
        kern,
        grid=(e, m // tm, nf + nd),
        in_specs=[
            pl.BlockSpec((None, tm, kh), lambda ei, mi, s: (ei, mi, 0), pipeline_mode=pl.Buffered(1)),
            pl.BlockSpec((None, tm, 1), lambda ei, mi, s: (ei, mi, 0)),
            pl.BlockSpec((None, k, tf), up_idx),
            pl.BlockSpec((None, k, tf), up_idx),
            pl.BlockSpec((None, ff, td), lambda ei, mi, s: (ei, 0, jnp.maximum(s - nf, 0))),
            pl.BlockSpec((None, ff, td), lambda ei, mi, s: (ei, 0, nd + jnp.maximum(s - nf, 0))),
        ],
        out_specs=pl.BlockSpec((None, tm, td), lambda ei, mi, s: (ei, mi, jnp.maximum(s - nf, 0))),
        out_shape=jax.ShapeDtypeStruct((e, m, kh), U32),
        scratch_shapes=[pltpu.VMEM((tm, k), BF16), pltpu.VMEM((nf, tm, tf), BF16)],
        compiler_params=_cparams(("arbitrary", "arbitrary", "arbitrary")),
        name="expert_ffn",
    )(xs, gate_col, wg, wu, wd, wd)


def _combine_kernel(bs_ref, oc_ref, ys_ref, x_ref, g2_ref, o_ref, ybuf, sem, acc_scr, *, tt, ch, pairs):
    b = pl.program_id(0)
    t = pl.program_id(1)
    s0 = bs_ref[b, t]
    s1 = bs_ref[b, t + 1]
    a0 = s0 & jnp.int32(-SUBLANES)
    nchunks = lax.shift_right_logical(jnp.maximum(s1 - a0, 0) + (ch - 1), jnp.int32(int(math.log2(ch))))

    def chunk_copy(j, slot):
        a = jnp.minimum(a0 + j * ch, pairs - ch)
        start = pl.multiple_of(b * pairs + a, SUBLANES)
        return pltpu.make_async_copy(ys_ref.at[pl.ds(start, ch)], ybuf.at[slot], sem.at[slot])

    acc_scr[...] = jnp.zeros_like(acc_scr)

    @pl.when(nchunks > 0)
    def _():
        chunk_copy(0, 0).start()

    off = oc_ref[0:1, :]
    end = off + oc_ref[1:2, :]

    def body(j, carry):
        slot = lax.rem(j, 2)
        chunk_copy(j, slot).wait()

        @pl.when(j + 1 < nchunks)
        def _():
            chunk_copy(j + 1, 1 - slot).start()

        a_want = a0 + j * ch
        a = jnp.minimum(a_want, pairs - ch)
        gpos = (a + lax.broadcasted_iota(I32, (ch, 1), 0)).astype(F32)
        member = (gpos >= off) & (gpos < end) & (gpos >= a_want.astype(F32))
        w_t = jnp.where(member, 1.0, 0.0).T.astype(BF16)
        hi, lo = _unpack_pair(ybuf[slot])
        acc_scr[:, 0:D_HALF] += _bdot(w_t, hi)
        acc_scr[:, D_HALF:D_MODEL] += _bdot(w_t, lo)
        return carry
    lax.fori_loop(0, nchunks, body, 0)

    o_ref[...] = x_ref[...] + g2_ref[...] * acc_scr[...]


def _combine(blk_start, offcnt, ysorted, x, mod, mod_row_fn, nb, n, pairs):
    k = x.shape[1]
    tt, ch = 256, 256
    ntb = n // tt
    kern = functools.partial(_combine_kernel, tt=tt, ch=ch, pairs=pairs)
    grid_spec = pltpu.PrefetchScalarGridSpec(
        num_scalar_prefetch=1,
        grid=(nb, ntb),
        in_specs=[
            pl.BlockSpec((None, SUBLANES, tt), lambda b, t, bs: (b, 0, t)),
            pl.BlockSpec(memory_space=pl.ANY),
            pl.BlockSpec((tt, k), lambda b, t, bs: (b * ntb + t, 0)),
            pl.BlockSpec((None, None, 1, k), lambda b, t, bs: (mod_row_fn(b), 5, 0, 0)),
        ],
        out_specs=pl.BlockSpec((tt, k), lambda b, t, bs: (b * ntb + t, 0)),
        scratch_shapes=[pltpu.VMEM((2, ch, k // 2), U32), pltpu.SemaphoreType.DMA((2,)),
                        pltpu.VMEM((tt, k), F32)],
    )
    return pl.pallas_call(
        kern,
        grid_spec=grid_spec,
        out_shape=jax.ShapeDtypeStruct((nb * n, k), F32),
        compiler_params=_cparams(("arbitrary", "arbitrary")),
        name="combine",
    )(blk_start, offcnt, ysorted, x, mod)


def _route_stream(x, gain, mod, mod_row_fn, router_t, etri, nb, n):
    cap = EC_CAPACITY_FACTOR * n // N_EXPERTS
    idx = jnp.arange(n, dtype=I32)
    tri = (idx[:, None] < idx[None, :]).astype(BF16)
    hp, meta, offcnt = _route(x, gain, mod, mod_row_fn, router_t, tri, etri, nb, n, cap)
    tok = (meta[:, :, 0] * 256.0 + meta[:, :, 1]).astype(I32)
    pos = (meta[:, :, 2] * 256.0 + meta[:, :, 3]).astype(I32)
    gate = meta[:, :, 4] + meta[:, :, 5] + meta[:, :, 6]
    batch = jnp.arange(nb, dtype=I32)[:, None, None]
    return dict(hp=hp, offcnt=offcnt, cap=cap, n=n,
                src_tok=jnp.swapaxes(batch * n + tok, 0, 1).reshape(N_EXPERTS, nb * cap),
                dst_pos=jnp.swapaxes(batch * (N_EXPERTS * cap) + pos, 0, 1).reshape(N_EXPERTS, nb * cap),
                gate=jnp.swapaxes(gate, 0, 1).reshape(N_EXPERTS, nb * cap))


def _ec_moe_residual(streams, gain, mod, router, wg, wu, wd, nb, tm):
    eidx = jnp.arange(N_EXPERTS, dtype=I32)
    etri = (eidx[None, :] < eidx[:, None]).astype(BF16)
    routed = [_route_stream(x, gain, mod, fn, router.T, etri, nb, n) for x, n, fn in streams]
    m_tot = sum(nb * r["cap"] for r in routed)
    erow = eidx[:, None] * m_tot
    row0, slot_rows = 0, []
    for r in routed:
        rows = nb * r["cap"]
        slot_rows.append(erow + row0 + jnp.arange(rows, dtype=I32)[None, :])
        row0 += rows
    xs = _permute_rows([(r["src_tok"], sr, r["hp"]) for r, sr in zip(routed, slot_rows)], N_EXPERTS * m_tot)
    gate_col = jnp.concatenate([r["gate"] for r in routed], axis=1)[..., None]
    y = _expert_ffn(xs.reshape(N_EXPERTS, m_tot, D_HALF), gate_col, wg, wu, wd, tm)
    y = y.reshape(N_EXPERTS * m_tot, D_HALF)
    outs = []
    for (x, n, fn), r, slot_row in zip(streams, routed, slot_rows):
        pairs = N_EXPERTS * r["cap"]
        ysorted = _permute_rows([(slot_row, r["dst_pos"], y)], nb * pairs)
        off = r["offcnt"][:, 0, :].astype(I32)
        blk_start = jnp.concatenate([off[:, ::256], jnp.full((nb, 1), pairs, I32)], axis=1)
        outs.append(_combine(blk_start, r["offcnt"], ysorted, x, mod, fn, nb, n, pairs))
    return outs


def _rope_tables(n, tm):
    rows = n // GRID_W
    row = jnp.repeat(jnp.arange(rows, dtype=F32), GRID_W)
    col = jnp.tile(jnp.arange(GRID_W, dtype=F32), rows)
    axis_dim = HEAD_DIM // 2
    inv_freq = ROPE_THETA ** (-jnp.arange(0, axis_dim, 2, dtype=F32) / axis_dim)
    ang = jnp.concatenate([row[:, None] * inv_freq, col[:, None] * inv_freq], axis=-1)
    cos, sin = jnp.cos(ang), jnp.sin(ang)
    cos_full = jnp.concatenate([cos, cos], axis=-1).reshape(n // tm, tm, HEAD_DIM)
    sin_signed = jnp.concatenate([-sin, sin], axis=-1).reshape(n // tm, tm, HEAD_DIM)
    return cos_full, sin_signed


def kernel(x, c, ctx, c_ctx,
           mod_w_0, mod_b_0, norm1_0, norm2_0, wqkv_0, wo_0, qnorm_0, knorm_0,
           lam_q1_0, lam_k1_0, lam_q2_0, lam_k2_0, subln_0, router_0, we_gate_0, we_up_0, we_down_0,
           mod_w_1, mod_b_1, norm1_1, norm2_1, wqkv_1, wo_1, qnorm_1, knorm_1,
           router_1, we_gate_1, we_up_1, we_down_1):
    nb, nq, d = x.shape
    nc = ctx.shape[1]
    tm = 1024
    x_lat = x.reshape(nb * nq, d)
    x_ctx = ctx.reshape(nb * nc, d)
    cond = jnp.zeros((MOD_ROWS, d), F32).at[:nb].set(c).at[nb].set(c_ctx)
    lat_row = lambda i: i // (nq // tm)
    ctx_row = lambda i: nb
    lat_b = lambda b: b
    cos, sin = _rope_tables(nq, tm)
    cos_id = jnp.ones((1, tm, HEAD_DIM), F32)
    sin_id = jnp.zeros((1, tm, HEAD_DIM), F32)
    q_scale = math.log2(math.e) / math.sqrt(HEAD_DIM)

    mod0 = _adaln(cond, mod_w_0, mod_b_0)
    n_heads2 = d // HEAD_DIM
    colgain0 = jnp.concatenate([jnp.tile(qnorm_0 * q_scale, n_heads2), jnp.tile(knorm_0, n_heads2),
                                jnp.ones((d,), F32)]).reshape(1, 3 * d)
    qkv_lat = _qkv_proj(x_lat, norm1_0, mod0, lat_row, wqkv_0, colgain0, cos, sin, 2 * d)
    qkv_ctx = _qkv_proj(x_ctx, norm1_0, mod0, ctx_row, wqkv_0, colgain0, cos_id, sin_id, 2 * d)
    lam_init = 0.8 - 0.6 * math.exp(-0.3 * 0)
    lam_params = jnp.stack([lam_q1_0, lam_k1_0, lam_q2_0, lam_k2_0])
    o_lat = _diff_attention(lam_params, subln_0, qkv_lat, [(qkv_lat, nq), (qkv_ctx, nc)], nb, nq, lam_init)
    o_ctx = _diff_attention(lam_params, subln_0, qkv_ctx, [(qkv_ctx, nc)], nb, nc, lam_init)
    x_lat = _proj_residual(o_lat, wo_0, x_lat, mod0, lat_row, 2)
    x_ctx = _proj_residual(o_ctx, wo_0, x_ctx, mod0, ctx_row, 2)
    x_lat, x_ctx = _ec_moe_residual([(x_lat, nq, lat_b), (x_ctx, nc, ctx_row)], norm2_0, mod0,
                                    router_0, we_gate_0, we_up_0, we_down_0, nb, 768)

    mod1 = _adaln(cond, mod_w_1, mod_b_1)
    kv_dim = (wqkv_1.shape[1] - d) // 2
    colgain1 = jnp.concatenate([jnp.tile(qnorm_1 * q_scale, d // HEAD_DIM),
                                jnp.tile(knorm_1, kv_dim // HEAD_DIM),
                                jnp.ones((kv_dim,), F32)]).reshape(1, d + 2 * kv_dim)
    qkv_lat = _qkv_proj(x_lat, norm1_1, mod1, lat_row, wqkv_1, colgain1, cos, sin, d + kv_dim)
    qkv_ctx = _qkv_proj(x_ctx, norm1_1, mod1, ctx_row, wqkv_1, colgain1, cos_id, sin_id, d + kv_dim)
    o_lat = _gqa_attention(qkv_lat, qkv_ctx, nb, nq, nc)
    x_lat = _proj_residual(o_lat, wo_1, x_lat, mod1, lat_row, 2)
    (x_lat,) = _ec_moe_residual([(x_lat, nq, lat_b)], norm2_1, mod1,
                                router_1, we_gate_1, we_up_1, we_down_1, nb, 1024)
    return x_lat.reshape(nb, nq, d)
```

```python
import functools
import math

import jax
import jax.numpy as jnp
from jax import lax
from jax.experimental import pallas as pl
from jax.experimental.pallas import tpu as pltpu

F32 = jnp.float32
BF16 = jnp.bfloat16

D_MODEL = 4096
GRID_W = 64
HEAD_DIM = 128
N_EXPERTS = 16
EC_CAPACITY_FACTOR = 2
EXPERT_FF = D_MODEL // 4
ROPE_THETA = 10000.0
NORM_EPS = 1e-6
N_MOD = 6
MOD_ROWS = 16
VMEM_LIMIT = 58 * 1024 * 1024


def _cparams(sem):
    return pltpu.CompilerParams(dimension_semantics=sem, vmem_limit_bytes=VMEM_LIMIT)


def _bdot(a, b):
    return jnp.dot(a, b, preferred_element_type=F32)


def _dot_nt(a, b):
    return lax.dot_general(a, b, (((1,), (1,)), ((), ())), preferred_element_type=F32)


def _silu(x):
    return x * (1.0 / (1.0 + jnp.exp(-x)))


def _split_bf16(x):
    hi = x.astype(BF16)
    lo = (x - hi.astype(F32)).astype(BF16)
    return hi, lo


def _adaln_kernel(c_ref, w_ref, b_ref, o_ref):
    s = _silu(c_ref[...]).astype(BF16)
    o_ref[...] = _bdot(s, w_ref[...].astype(BF16)) + b_ref[...]


def _adaln(cond, w, b):
    n = w.shape[1]
    tn = 512
    out = pl.pallas_call(
        _adaln_kernel,
        grid=(n // tn,),
        in_specs=[
            pl.BlockSpec((MOD_ROWS, D_MODEL), lambda j: (0, 0)),
            pl.BlockSpec((D_MODEL, tn), lambda j: (0, j)),
            pl.BlockSpec((1, tn), lambda j: (0, j)),
        ],
        out_specs=pl.BlockSpec((MOD_ROWS, tn), lambda j: (0, j)),
        out_shape=jax.ShapeDtypeStruct((MOD_ROWS, n), F32),
        compiler_params=_cparams(("arbitrary",)),
        name="adaln",
    )(cond, w, b.reshape(1, n))
    return out.reshape(MOD_ROWS, N_MOD, 1, D_MODEL)


def _norm_mod(xv, g, shift, scale):
    ms = jnp.mean(xv * xv, axis=-1, keepdims=True)
    y = xv * lax.rsqrt(ms + NORM_EPS) * g
    return y * (1.0 + scale) + shift


def _qkv_kernel(x_ref, g_ref, sh_ref, sc_ref, w_ref, cg_ref, cos_ref, sin_ref, o_ref, h_scr,
                *, tm, tn, n_qk_tiles, chunk, row_group):
    j = pl.program_id(1)

    @pl.when(j == 0)
    def _():
        def body(r, carry):
            rows = pl.ds(pl.multiple_of(r * chunk, chunk), chunk)
            h = _norm_mod(x_ref[rows, :], g_ref[...], sh_ref[...], sc_ref[...])
            h_scr[rows, :] = h.astype(BF16)
            return carry
        lax.fori_loop(0, tm // chunk, body, 0)

    @pl.when(j < n_qk_tiles)
    def _():
        w = w_ref[...].astype(BF16)
        for r in range(tm // row_group):
            rows = slice(r * row_group, (r + 1) * row_group)
            acc = _bdot(h_scr[rows, :], w)
            cos = cos_ref[rows, :]
            sin = sin_ref[rows, :]
            for gi in range(tn // HEAD_DIM):
                cols = slice(gi * HEAD_DIM, (gi + 1) * HEAD_DIM)
                a = acc[:, cols]
                ms = jnp.mean(a * a, axis=-1, keepdims=True)
                a = a * lax.rsqrt(ms + NORM_EPS) * cg_ref[:, cols]
                a = a * cos + pltpu.roll(a, HEAD_DIM // 2, axis=1) * sin
                o_ref[rows, cols] = a.astype(BF16)

    @pl.when(j >= n_qk_tiles)
    def _():
        o_ref[...] = _bdot(h_scr[...], w_ref[...].astype(BF16)).astype(BF16)


def _qkv_proj(x, gain, mod, mod_row_fn, w, colgain, cos, sin, n_qk_cols):
    r, k = x.shape
    n = w.shape[1]
    tm, tn = 1024, 512
    npos = cos.shape[0]
    kern = functools.partial(_qkv_kernel, tm=tm, tn=tn, n_qk_tiles=n_qk_cols // tn, chunk=32,
                             row_group=256)
    return pl.pallas_call(
        kern,
        grid=(r // tm, n // tn),
        in_specs=[
            pl.BlockSpec((tm, k), lambda i, j: (i, 0), pipeline_mode=pl.Buffered(1)),
            pl.BlockSpec((1, k), lambda i, j: (0, 0)),
            pl.BlockSpec((None, None, 1, k), lambda i, j: (mod_row_fn(i), 0, 0, 0)),
            pl.BlockSpec((None, None, 1, k), lambda i, j: (mod_row_fn(i), 1, 0, 0)),
            pl.BlockSpec((k, tn), lambda i, j: (0, j)),
            pl.BlockSpec((1, tn), lambda i, j: (0, j)),
            pl.BlockSpec((None, tm, HEAD_DIM), lambda i, j: (i % npos, 0, 0)),
            pl.BlockSpec((None, tm, HEAD_DIM), lambda i, j: (i % npos, 0, 0)),
        ],
        out_specs=pl.BlockSpec((tm, tn), lambda i, j: (i, j)),
        out_shape=jax.ShapeDtypeStruct((r, n), BF16),
        scratch_shapes=[pltpu.VMEM((tm, k), BF16)],
        compiler_params=_cparams(("arbitrary", "arbitrary")),
        name="qkv_proj",
    )(x, gain.reshape(1, k), mod, mod, w, colgain, cos, sin)


def _attend(q, kvs, row_block=16):
    tq = q.shape[0]
    ss = [_dot_nt(q, k) for k, _ in kvs]
    p_blocks = [[] for _ in kvs]
    l_blocks = []
    for r in range(tq // row_block):
        rows = slice(r * row_block, (r + 1) * row_block)
        sb = [s[rows, :] for s in ss]
        m = functools.reduce(jnp.maximum, [jnp.max(x, axis=-1, keepdims=True) for x in sb])
        pb = [jnp.exp2(x - m) for x in sb]
        l_blocks.append(functools.reduce(lambda a, b: a + b,
                                         [jnp.sum(p, axis=-1, keepdims=True) for p in pb]))
        for blocks, p in zip(p_blocks, pb):
            blocks.append(p.astype(BF16))
    l = jnp.concatenate(l_blocks, axis=0)
    o = functools.reduce(lambda a, b: a + b,
                         [_bdot(jnp.concatenate(blocks, axis=0), v) for blocks, (_, v) in zip(p_blocks, kvs)])
    return o / l


def _diff_lambda(lam_ref, lam_init):
    lp = lam_ref[...]
    s1 = jnp.sum(lp[0:1, :] * lp[1:2, :], axis=-1, keepdims=True)
    s2 = jnp.sum(lp[2:3, :] * lp[3:4, :], axis=-1, keepdims=True)
    return jnp.exp(s1) - jnp.exp(s2) + lam_init


def _diff_attn_kernel(*refs, lam_init, n_kv, sub_rows=256):
    lam_ref, subln_ref, q_ref = refs[:3]
    k_refs = refs[3:3 + n_kv]
    v_refs = refs[3 + n_kv:3 + 2 * n_kv]
    o_ref = refs[3 + 2 * n_kv]
    lam = _diff_lambda(lam_ref, lam_init)
    tq = q_ref.shape[0]
    sub = min(tq, sub_rows)
    for qs in range(tq // sub):
        rows = slice(qs * sub, (qs + 1) * sub)
        outs = []
        for t in range(2):
            cols = slice(t * HEAD_DIM, (t + 1) * HEAD_DIM)
            kvs = [(kr[:, cols], vr[...]) for kr, vr in zip(k_refs, v_refs)]
            outs.append(_attend(q_ref[rows, cols], kvs))
        o = outs[0] - lam * outs[1]
        ms = jnp.mean(o * o, axis=-1, keepdims=True)
        o = o * lax.rsqrt(ms + NORM_EPS) * subln_ref[...] * (1.0 - lam_init)
        o_ref[rows, :] = o.astype(BF16)


def _diff_attention(lam_params, subln, qkv_q, kv_sources, nb, nq, lam_init):
    heads = D_MODEL // (2 * HEAD_DIM)
    hw = 2 * HEAD_DIM
    tq = min(nq, 1024)
    nqb = nq // tq
    n_kv = len(kv_sources)
    in_specs = [
        pl.BlockSpec((4, HEAD_DIM), lambda b, h, i: (0, 0)),
        pl.BlockSpec((1, hw), lambda b, h, i: (0, 0)),
        pl.BlockSpec((tq, hw), lambda b, h, i: (b * nqb + i, h)),
    ]
    args = [lam_params, subln.reshape(1, hw), qkv_q]
    for arr, nk in kv_sources:
        in_specs.append(pl.BlockSpec((nk, hw), lambda b, h, i: (b, heads + h)))
        args.append(arr)
    for arr, nk in kv_sources:
        in_specs.append(pl.BlockSpec((nk, hw), lambda b, h, i: (b, 2 * heads + h)))
        args.append(arr)
    kern = functools.partial(_diff_attn_kernel, lam_init=lam_init, n_kv=n_kv)
    return pl.pallas_call(
        kern,
        grid=(nb, heads, nqb),
        in_specs=in_specs,
        out_specs=pl.BlockSpec((tq, hw), lambda b, h, i: (b * nqb + i, h)),
        out_shape=jax.ShapeDtypeStruct((nb * nq, D_MODEL), BF16),
        compiler_params=_cparams(("arbitrary", "arbitrary", "arbitrary")),
        name="diff_attn",
    )(*args)


def _gqa_attn_kernel(q_ref, kl_ref, kc_ref, vl_ref, vc_ref, o_ref, *, group, sub_rows=256):
    kvs = [(kl_ref[...], vl_ref[...]), (kc_ref[...], vc_ref[...])]
    tq = q_ref.shape[0]
    sub = min(tq, sub_rows)
    for qs in range(tq // sub):
        rows = slice(qs * sub, (qs + 1) * sub)
        for g in range(group):
            cols = slice(g * HEAD_DIM, (g + 1) * HEAD_DIM)
            o_ref[rows, cols] = _attend(q_ref[rows, cols], kvs).astype(BF16)


def _gqa_attention(qkv_lat, qkv_ctx, nb, nq, nc):
    q_heads = D_MODEL // HEAD_DIM
    kv_heads = q_heads // 4
    group = q_heads // kv_heads
    gw = group * HEAD_DIM
    tq = 512
    nqb = nq // tq
    k_off = D_MODEL // HEAD_DIM
    v_off = k_off + kv_heads
    kern = functools.partial(_gqa_attn_kernel, group=group)
    return pl.pallas_call(
        kern,
        grid=(nb, kv_heads, nqb),
        in_specs=[
            pl.BlockSpec((tq, gw), lambda b, h, i: (b * nqb + i, h)),
            pl.BlockSpec((nq, HEAD_DIM), lambda b, h, i: (b, k_off + h)),
            pl.BlockSpec((nc, HEAD_DIM), lambda b, h, i: (b, k_off + h)),
            pl.BlockSpec((nq, HEAD_DIM), lambda b, h, i: (b, v_off + h)),
            pl.BlockSpec((nc, HEAD_DIM), lambda b, h, i: (b, v_off + h)),
        ],
        out_specs=pl.BlockSpec((tq, gw), lambda b, h, i: (b * nqb + i, h)),
        out_shape=jax.ShapeDtypeStruct((nb * nq, D_MODEL), BF16),
        compiler_params=_cparams(("arbitrary", "arbitrary", "arbitrary")),
        name="gqa_attn",
    )(qkv_lat, qkv_lat, qkv_ctx, qkv_lat, qkv_ctx)


def _proj_res_kernel(o_ref, w_ref, res_ref, gate_ref, out_ref):
    acc = _bdot(o_ref[...], w_ref[...].astype(BF16))
    out_ref[...] = res_ref[...] + gate_ref[...] * acc


def _proj_residual(o, w, res, mod, mod_row_fn, gate_idx):
    r, k = o.shape
    n = w.shape[1]
    tm, tn = 1024, 512
    return pl.pallas_call(
        _proj_res_kernel,
        grid=(r // tm, n // tn),
        in_specs=[
            pl.BlockSpec((tm, k), lambda i, j: (i, 0)),
            pl.BlockSpec((k, tn), lambda i, j: (0, j)),
            pl.BlockSpec((tm, tn), lambda i, j: (i, j)),
            pl.BlockSpec((None, None, 1, tn), lambda i, j: (mod_row_fn(i), gate_idx, 0, j)),
        ],
        out_specs=pl.BlockSpec((tm, tn), lambda i, j: (i, j)),
        out_shape=jax.ShapeDtypeStruct((r, n), F32),
        compiler_params=_cparams(("arbitrary", "arbitrary")),
        name="out_proj",
    )(o, w, res, mod)


def _route_kernel(x_ref, g_ref, sh_ref, sc_ref, rt_ref, tri_ref, h_ref, gate_ref, slot_ref, lg_scr,
                  *, tc, n, cap):
    c = pl.program_id(1)
    h = _norm_mod(x_ref[...], g_ref[...], sh_ref[...], sc_ref[...])
    h_ref[...] = h.astype(BF16)
    h_hi, h_lo = _split_bf16(h)
    r_hi, r_lo = _split_bf16(rt_ref[...])
    lg_scr[c] = _dot_nt(r_hi, h_hi) + _dot_nt(r_hi, h_lo) + _dot_nt(r_lo, h_hi)

    @pl.when(c == n // tc - 1)
    def _():
        lg = jnp.concatenate([lg_scr[i] for i in range(n // tc)], axis=1)
        mx = jnp.max(lg, axis=0, keepdims=True)
        ex = jnp.exp(lg - mx)
        aff = ex / jnp.sum(ex, axis=0, keepdims=True)
        bits = pltpu.bitcast(aff, jnp.int32)
        thr = jnp.zeros((N_EXPERTS, 1), jnp.int32)
        for bit in range(30, -1, -1):
            cand = thr | jnp.int32(1 << bit)
            cnt = jnp.sum(jnp.where(bits >= cand, 1.0, 0.0), axis=1, keepdims=True)
            thr = jnp.where(cnt >= cap, cand, thr)
        gt = bits > thr
        eq = bits == thr
        need = cap - jnp.sum(jnp.where(gt, 1.0, 0.0), axis=1, keepdims=True)
        tri = tri_ref[...]
        eq_rank = _bdot(jnp.where(eq, 1.0, 0.0).astype(BF16), tri)
        sel = gt | (eq & (eq_rank < need))
        slot = _bdot(jnp.where(sel, 1.0, 0.0).astype(BF16), tri)
        gate_ref[...] = jnp.where(sel, aff, 0.0)
        slot_ref[...] = jnp.where(sel, slot.astype(jnp.int32), -1)


def _route(x, gain, mod, mod_row_fn, router_t, tri, nb, n, cap):
    k = x.shape[1]
    tc = 256
    nch = n // tc
    kern = functools.partial(_route_kernel, tc=tc, n=n, cap=cap)
    return pl.pallas_call(
        kern,
        grid=(nb, nch),
        in_specs=[
            pl.BlockSpec((tc, k), lambda b, c: (b * nch + c, 0)),
            pl.BlockSpec((1, k), lambda b, c: (0, 0)),
            pl.BlockSpec((None, None, 1, k), lambda b, c: (mod_row_fn(b), 3, 0, 0)),
            pl.BlockSpec((None, None, 1, k), lambda b, c: (mod_row_fn(b), 4, 0, 0)),
            pl.BlockSpec((N_EXPERTS, k), lambda b, c: (0, 0)),
            pl.BlockSpec((n, n), lambda b, c: (0, 0)),
        ],
        out_specs=[
            pl.BlockSpec((tc, k), lambda b, c: (b * nch + c, 0)),
            pl.BlockSpec((None, N_EXPERTS, n), lambda b, c: (b, 0, 0)),
            pl.BlockSpec((None, N_EXPERTS, n), lambda b, c: (b, 0, 0)),
        ],
        out_shape=[
            jax.ShapeDtypeStruct((nb * n, k), BF16),
            jax.ShapeDtypeStruct((nb, N_EXPERTS, n), F32),
            jax.ShapeDtypeStruct((nb, N_EXPERTS, n), jnp.int32),
        ],
        scratch_shapes=[pltpu.VMEM((nch, N_EXPERTS, tc), F32)],
        compiler_params=_cparams(("arbitrary", "arbitrary")),
        name="route",
    )(x, gain.reshape(1, k), mod, mod, router_t, tri)


def _gather_kernel(slot_ref, h_ref, xs_ref, *, cap, n):
    onehot = lax.broadcasted_iota(jnp.int32, (cap, n), 0) == slot_ref[...]
    xs_ref[...] = _bdot(jnp.where(onehot, 1.0, 0.0).astype(BF16), h_ref[...]).astype(BF16)


def _gather(slot_t, h, nb, n, cap):
    k = h.shape[1]
    kern = functools.partial(_gather_kernel, cap=cap, n=n)
    return pl.pallas_call(
        kern,
        grid=(nb, N_EXPERTS),
        in_specs=[
            pl.BlockSpec((None, None, 1, n), lambda b, e: (b, e, 0, 0)),
            pl.BlockSpec((n, k), lambda b, e: (b, 0)),
        ],
        out_specs=pl.BlockSpec((None, cap, k), lambda b, e: (e, b, 0)),
        out_shape=jax.ShapeDtypeStruct((N_EXPERTS, nb * cap, k), BF16),
        compiler_params=_cparams(("arbitrary", "arbitrary")),
        name="gather",
    )(slot_t.reshape(nb, N_EXPERTS, 1, n), h)


def _ffn_kernel(xs_ref, wg_ref, wu_ref, wd_ref, y_ref, hid_scr, *, nf, tf):
    s = pl.program_id(2)

    @pl.when(s < nf)
    def _():
        xs = xs_ref[...]
        a = _bdot(xs, wg_ref[...].astype(BF16))
        u = _bdot(xs, wu_ref[...].astype(BF16))
        hid_scr[s] = (_silu(a) * u).astype(BF16)

    @pl.when(s >= nf)
    def _():
        acc = _bdot(hid_scr[0], wd_ref[0:tf, :].astype(BF16))
        for f in range(1, nf):
            acc = acc + _bdot(hid_scr[f], wd_ref[f * tf:(f + 1) * tf, :].astype(BF16))
        y_ref[...] = acc.astype(BF16)


def _expert_ffn(xs, wg, wu, wd):
    e, m, k = xs.shape
    ff = wg.shape[2]
    tm = min(m, 1024)
    tf, td = 256, 1024
    nf, nd = ff // tf, k // td
    kern = functools.partial(_ffn_kernel, nf=nf, tf=tf)
    return pl.pallas_call(
        kern,
        grid=(e, m // tm, nf + nd),
        in_specs=[
            pl.BlockSpec((None, tm, k), lambda ei, mi, s: (ei, mi, 0)),
            pl.BlockSpec((None, k, tf), lambda ei, mi, s: (ei, 0, jnp.minimum(s, nf - 1))),
            pl.BlockSpec((None, k, tf), lambda ei, mi, s: (ei, 0, jnp.minimum(s, nf - 1))),
            pl.BlockSpec((None, ff, td), lambda ei, mi, s: (ei, 0, jnp.maximum(s - nf, 0))),
        ],
        out_specs=pl.BlockSpec((None, tm, td), lambda ei, mi, s: (ei, mi, jnp.maximum(s - nf, 0))),
        out_shape=jax.ShapeDtypeStruct((e, m, k), BF16),
        scratch_shapes=[pltpu.VMEM((nf, tm, tf), BF16)],
        compiler_params=_cparams(("arbitrary", "arbitrary", "arbitrary")),
        name="expert_ffn",
    )(xs, wg, wu, wd)


def _combine_kernel(slot_ref, gate_ref, y_ref, x_ref, g2_ref, o_ref, acc_scr, *, tt, cap):
    e = pl.program_id(2)

    @pl.when(e == 0)
    def _():
        acc_scr[...] = jnp.zeros_like(acc_scr)

    onehot = lax.broadcasted_iota(jnp.int32, (tt, cap), 1) == slot_ref[...]
    contrib = _bdot(jnp.where(onehot, 1.0, 0.0).astype(BF16), y_ref[...])
    acc_scr[...] += gate_ref[...] * contrib

    @pl.when(e == N_EXPERTS - 1)
    def _():
        o_ref[...] = x_ref[...] + g2_ref[...] * acc_scr[...]


def _combine(slot_c, gate_c, y, x, mod, mod_row_fn, nb, n, cap):
    k = x.shape[1]
    tt = min(n, 512)
    ntb = n // tt
    kern = functools.partial(_combine_kernel, tt=tt, cap=cap)
    return pl.pallas_call(
        kern,
        grid=(nb, ntb, N_EXPERTS),
        in_specs=[
            pl.BlockSpec((None, None, tt, 1), lambda b, t, e: (b, e, t, 0)),
            pl.BlockSpec((None, None, tt, 1), lambda b, t, e: (b, e, t, 0)),
            pl.BlockSpec((None, cap, k), lambda b, t, e: (e, b, 0)),
            pl.BlockSpec((tt, k), lambda b, t, e: (b * ntb + t, 0)),
            pl.BlockSpec((None, None, 1, k), lambda b, t, e: (mod_row_fn(b), 5, 0, 0)),
        ],
        out_specs=pl.BlockSpec((tt, k), lambda b, t, e: (b * ntb + t, 0)),
        out_shape=jax.ShapeDtypeStruct((nb * n, k), F32),
        scratch_shapes=[pltpu.VMEM((tt, k), F32)],
        compiler_params=_cparams(("arbitrary", "arbitrary", "arbitrary")),
        name="combine",
    )(slot_c, gate_c, y, x, mod)


def _ec_moe_residual(x, gain, mod, mod_row_fn, router, wg, wu, wd, nb, n):
    cap = EC_CAPACITY_FACTOR * n // N_EXPERTS
    idx = jnp.arange(n, dtype=jnp.int32)
    tri = (idx[:, None] < idx[None, :]).astype(BF16)
    h, gate_t, slot_t = _route(x, gain, mod, mod_row_fn, router.T, tri, nb, n, cap)
    xs = _gather(slot_t, h, nb, n, cap)
    y = _expert_ffn(xs, wg, wu, wd)
    return _combine(slot_t[..., None], gate_t[..., None], y, x, mod, mod_row_fn, nb, n, cap)


def _rope_tables(n, tm):
    rows = n // GRID_W
    row = jnp.repeat(jnp.arange(rows, dtype=F32), GRID_W)
    col = jnp.tile(jnp.arange(GRID_W, dtype=F32), rows)
    axis_dim = HEAD_DIM // 2
    inv_freq = ROPE_THETA ** (-jnp.arange(0, axis_dim, 2, dtype=F32) / axis_dim)
    ang = jnp.concatenate([row[:, None] * inv_freq, col[:, None] * inv_freq], axis=-1)
    cos, sin = jnp.cos(ang), jnp.sin(ang)
    cos_full = jnp.concatenate([cos, cos], axis=-1).reshape(n // tm, tm, HEAD_DIM)
    sin_signed = jnp.concatenate([-sin, sin], axis=-1).reshape(n // tm, tm, HEAD_DIM)
    return cos_full, sin_signed


def kernel(x, c, ctx, c_ctx,
           mod_w_0, mod_b_0, norm1_0, norm2_0, wqkv_0, wo_0, qnorm_0, knorm_0,
           lam_q1_0, lam_k1_0, lam_q2_0, lam_k2_0, subln_0, router_0, we_gate_0, we_up_0, we_down_0,
           mod_w_1, mod_b_1, norm1_1, norm2_1, wqkv_1, wo_1, qnorm_1, knorm_1,
           router_1, we_gate_1, we_up_1, we_down_1):
    nb, nq, d = x.shape
    nc = ctx.shape[1]
    tm = 1024
    x_lat = x.reshape(nb * nq, d)
    x_ctx = ctx.reshape(nb * nc, d)
    cond = jnp.zeros((MOD_ROWS, d), F32).at[:nb].set(c).at[nb].set(c_ctx)
    lat_row = lambda i: i // (nq // tm)
    ctx_row = lambda i: nb
    lat_b = lambda b: b
    cos, sin = _rope_tables(nq, tm)
    cos_id = jnp.ones((1, tm, HEAD_DIM), F32)
    sin_id = jnp.zeros((1, tm, HEAD_DIM), F32)
    q_scale = math.log2(math.e) / math.sqrt(HEAD_DIM)

    mod0 = _adaln(cond, mod_w_0, mod_b_0)
    n_heads2 = d // HEAD_DIM
    colgain0 = jnp.concatenate([jnp.tile(qnorm_0 * q_scale, n_heads2), jnp.tile(knorm_0, n_heads2),
                                jnp.ones((d,), F32)]).reshape(1, 3 * d)
    qkv_lat = _qkv_proj(x_lat, norm1_0, mod0, lat_row, wqkv_0, colgain0, cos, sin, 2 * d)
    qkv_ctx = _qkv_proj(x_ctx, norm1_0, mod0, ctx_row, wqkv_0, colgain0, cos_id, sin_id, 2 * d)
    lam_init = 0.8 - 0.6 * math.exp(-0.3 * 0)
    lam_params = jnp.stack([lam_q1_0, lam_k1_0, lam_q2_0, lam_k2_0])
    o_lat = _diff_attention(lam_params, subln_0, qkv_lat, [(qkv_lat, nq), (qkv_ctx, nc)], nb, nq, lam_init)
    o_ctx = _diff_attention(lam_params, subln_0, qkv_ctx, [(qkv_ctx, nc)], nb, nc, lam_init)
    x_lat = _proj_residual(o_lat, wo_0, x_lat, mod0, lat_row, 2)
    x_ctx = _proj_residual(o_ctx, wo_0, x_ctx, mod0, ctx_row, 2)
    x_lat = _ec_moe_residual(x_lat, norm2_0, mod0, lat_b, router_0, we_gate_0, we_up_0, we_down_0, nb, nq)
    x_ctx = _ec_moe_residual(x_ctx, norm2_0, mod0, ctx_row, router_0, we_gate_0, we_up_0, we_down_0, nb, nc)

    mod1 = _adaln(cond, mod_w_1, mod_b_1)
    kv_dim = (wqkv_1.shape[1] - d) // 2
    colgain1 = jnp.concatenate([jnp.tile(qnorm_1 * q_scale, d // HEAD_DIM),
                                jnp.tile(knorm_1, kv_dim // HEAD_DIM),
                                jnp.ones((kv_dim,), F32)]).reshape(1, d + 2 * kv_dim)
    qkv_lat = _qkv_proj(x_lat, norm1_1, mod1, lat_row, wqkv_1, colgain1, cos, sin, d + kv_dim)
    qkv_ctx = _qkv_proj(x_ctx, norm1_1, mod1, ctx_row, wqkv_1, colgain1, cos_id, sin_id, d + kv_dim)
    o_lat = _gqa_attention(qkv_lat, qkv_ctx, nb, nq, nc)
    x_lat = _proj_residual(o_lat, wo_1, x_lat, mod1, lat_row, 2)
    x_lat = _ec_moe_residual(x_lat, norm2_1, mod1, lat_b, router_1, we_gate_1, we_up_1, we_down_1, nb, nq)
    return x_lat.reshape(nb, nq, d)
```

```python
import functools
import math

import jax
import jax.numpy as jnp
from jax import lax
from jax.experimental import pallas as pl
from jax.experimental.pallas import tpu as pltpu

F32 = jnp.float32
BF16 = jnp.bfloat16

D_MODEL = 4096
GRID_W = 64
HEAD_DIM = 128
N_EXPERTS = 16
EC_CAPACITY_FACTOR = 2
EXPERT_FF = D_MODEL // 4
ROPE_THETA = 10000.0
NORM_EPS = 1e-6
N_MOD = 6
MOD_ROWS = 16
VMEM_LIMIT = 58 * 1024 * 1024


def _cparams(sem):
    return pltpu.CompilerParams(dimension_semantics=sem, vmem_limit_bytes=VMEM_LIMIT)


def _bdot(a, b):
    return jnp.dot(a, b, preferred_element_type=F32)


def _dot_nt(a, b):
    return lax.dot_general(a, b, (((1,), (1,)), ((), ())), preferred_element_type=F32)


def _silu(x):
    return x * (1.0 / (1.0 + jnp.exp(-x)))


def _split_bf16(x):
    hi = x.astype(BF16)
    lo = (x - hi.astype(F32)).astype(BF16)
    return hi, lo


def _adaln_kernel(c_ref, w_ref, b_ref, o_ref):
    s = _silu(c_ref[...]).astype(BF16)
    o_ref[...] = _bdot(s, w_ref[...].astype(BF16)) + b_ref[...]


def _adaln(cond, w, b):
    n = w.shape[1]
    tn = 512
    out = pl.pallas_call(
        _adaln_kernel,
        grid=(n // tn,),
        in_specs=[
            pl.BlockSpec((MOD_ROWS, D_MODEL), lambda j: (0, 0)),
            pl.BlockSpec((D_MODEL, tn), lambda j: (0, j)),
            pl.BlockSpec((1, tn), lambda j: (0, j)),
        ],
        out_specs=pl.BlockSpec((MOD_ROWS, tn), lambda j: (0, j)),
        out_shape=jax.ShapeDtypeStruct((MOD_ROWS, n), F32),
        compiler_params=_cparams(("arbitrary",)),
        name="adaln",
    )(cond, w, b.reshape(1, n))
    return out.reshape(MOD_ROWS, N_MOD, 1, D_MODEL)


def _norm_mod(xv, g, shift, scale):
    ms = jnp.mean(xv * xv, axis=-1, keepdims=True)
    y = xv * lax.rsqrt(ms + NORM_EPS) * g
    return y * (1.0 + scale) + shift


def _qkv_kernel(x_ref, g_ref, sh_ref, sc_ref, w_ref, cg_ref, cos_ref, sin_ref, o_ref, h_scr,
                *, tm, tn, n_qk_tiles, chunk, row_group):
    j = pl.program_id(1)

    def qk_tile(make_h):
        w = w_ref[...].astype(BF16)
        for r in range(tm // row_group):
            rows = slice(r * row_group, (r + 1) * row_group)
            if make_h:
                for c in range(row_group // chunk):
                    rr = slice(r * row_group + c * chunk, r * row_group + (c + 1) * chunk)
                    h = _norm_mod(x_ref[rr, :], g_ref[...], sh_ref[...], sc_ref[...])
                    h_scr[rr, :] = h.astype(BF16)
            acc = _bdot(h_scr[rows, :], w)
            cos = cos_ref[rows, :]
            sin = sin_ref[rows, :]
            for gi in range(tn // HEAD_DIM):
                cols = slice(gi * HEAD_DIM, (gi + 1) * HEAD_DIM)
                a = acc[:, cols]
                ms = jnp.mean(a * a, axis=-1, keepdims=True)
                a = a * lax.rsqrt(ms + NORM_EPS) * cg_ref[:, cols]
                a = a * cos + pltpu.roll(a, HEAD_DIM // 2, axis=1) * sin
                o_ref[rows, cols] = a.astype(BF16)

    @pl.when(j == 0)
    def _():
        qk_tile(True)

    @pl.when((j > 0) & (j < n_qk_tiles))
    def _():
        qk_tile(False)

    @pl.when(j >= n_qk_tiles)
    def _():
        o_ref[...] = _bdot(h_scr[...], w_ref[...].astype(BF16)).astype(BF16)


def _qkv_proj(x, gain, mod, mod_row_fn, w, colgain, cos, sin, n_qk_cols):
    r, k = x.shape
    n = w.shape[1]
    tm, tn = 1024, 512
    npos = cos.shape[0]
    kern = functools.partial(_qkv_kernel, tm=tm, tn=tn, n_qk_tiles=n_qk_cols // tn, chunk=32,
                             row_group=256)
    return pl.pallas_call(
        kern,
        grid=(r // tm, n // tn),
        in_specs=[
            pl.BlockSpec((tm, k), lambda i, j: (i, 0), pipeline_mode=pl.Buffered(1)),
            pl.BlockSpec((1, k), lambda i, j: (0, 0)),
            pl.BlockSpec((None, None, 1, k), lambda i, j: (mod_row_fn(i), 0, 0, 0)),
            pl.BlockSpec((None, None, 1, k), lambda i, j: (mod_row_fn(i), 1, 0, 0)),
            pl.BlockSpec((k, tn), lambda i, j: (0, j)),
            pl.BlockSpec((1, tn), lambda i, j: (0, j)),
            pl.BlockSpec((None, tm, HEAD_DIM), lambda i, j: (i % npos, 0, 0)),
            pl.BlockSpec((None, tm, HEAD_DIM), lambda i, j: (i % npos, 0, 0)),
        ],
        out_specs=pl.BlockSpec((tm, tn), lambda i, j: (i, j)),
        out_shape=jax.ShapeDtypeStruct((r, n), BF16),
        scratch_shapes=[pltpu.VMEM((tm, k), BF16)],
        compiler_params=_cparams(("arbitrary", "arbitrary")),
        name="qkv_proj",
    )(x, gain.reshape(1, k), mod, mod, w, colgain, cos, sin)


def _attend(q, kvs, row_block=16):
    tq = q.shape[0]
    ss = [_dot_nt(q, k) for k, _ in kvs]
    p_blocks = [[] for _ in kvs]
    l_blocks = []
    for r in range(tq // row_block):
        rows = slice(r * row_block, (r + 1) * row_block)
        sb = [s[rows, :] for s in ss]
        m = functools.reduce(jnp.maximum, [jnp.max(x, axis=-1, keepdims=True) for x in sb])
        pb = [jnp.exp2(x - m) for x in sb]
        l_blocks.append(functools.reduce(lambda a, b: a + b,
                                         [jnp.sum(p, axis=-1, keepdims=True) for p in pb]))
        for blocks, p in zip(p_blocks, pb):
            blocks.append(p.astype(BF16))
    l = jnp.concatenate(l_blocks, axis=0)
    o = functools.reduce(lambda a, b: a + b,
                         [_bdot(jnp.concatenate(blocks, axis=0), v) for blocks, (_, v) in zip(p_blocks, kvs)])
    return o / l


def _diff_lambda(lam_ref, lam_init):
    lp = lam_ref[...]
    s1 = jnp.sum(lp[0:1, :] * lp[1:2, :], axis=-1, keepdims=True)
    s2 = jnp.sum(lp[2:3, :] * lp[3:4, :], axis=-1, keepdims=True)
    return jnp.exp(s1) - jnp.exp(s2) + lam_init


def _diff_attn_kernel(*refs, lam_init, n_kv, sub_rows=256):
    lam_ref, subln_ref, q_ref = refs[:3]
    k_refs = refs[3:3 + n_kv]
    v_refs = refs[3 + n_kv:3 + 2 * n_kv]
    o_ref = refs[3 + 2 * n_kv]
    lam = _diff_lambda(lam_ref, lam_init)
    tq = q_ref.shape[0]
    sub = min(tq, sub_rows)
    for qs in range(tq // sub):
        rows = slice(qs * sub, (qs + 1) * sub)
        outs = []
        for t in range(2):
            cols = slice(t * HEAD_DIM, (t + 1) * HEAD_DIM)
            kvs = [(kr[:, cols], vr[...]) for kr, vr in zip(k_refs, v_refs)]
            outs.append(_attend(q_ref[rows, cols], kvs))
        o = outs[0] - lam * outs[1]
        ms = jnp.mean(o * o, axis=-1, keepdims=True)
        o = o * lax.rsqrt(ms + NORM_EPS) * subln_ref[...] * (1.0 - lam_init)
        o_ref[rows, :] = o.astype(BF16)


def _diff_attention(lam_params, subln, qkv_q, kv_sources, nb, nq, lam_init):
    heads = D_MODEL // (2 * HEAD_DIM)
    hw = 2 * HEAD_DIM
    tq = min(nq, 1024)
    nqb = nq // tq
    n_kv = len(kv_sources)
    in_specs = [
        pl.BlockSpec((4, HEAD_DIM), lambda b, h, i: (0, 0)),
        pl.BlockSpec((1, hw), lambda b, h, i: (0, 0)),
        pl.BlockSpec((tq, hw), lambda b, h, i: (b * nqb + i, h)),
    ]
    args = [lam_params, subln.reshape(1, hw), qkv_q]
    for arr, nk in kv_sources:
        in_specs.append(pl.BlockSpec((nk, hw), lambda b, h, i: (b, heads + h)))
        args.append(arr)
    for arr, nk in kv_sources:
        in_specs.append(pl.BlockSpec((nk, hw), lambda b, h, i: (b, 2 * heads + h)))
        args.append(arr)
    kern = functools.partial(_diff_attn_kernel, lam_init=lam_init, n_kv=n_kv)
    return pl.pallas_call(
        kern,
        grid=(nb, heads, nqb),
        in_specs=in_specs,
        out_specs=pl.BlockSpec((tq, hw), lambda b, h, i: (b * nqb + i, h)),
        out_shape=jax.ShapeDtypeStruct((nb * nq, D_MODEL), BF16),
        compiler_params=_cparams(("arbitrary", "arbitrary", "arbitrary")),
        name="diff_attn",
    )(*args)


def _gqa_attn_kernel(q_ref, kl_ref, kc_ref, vl_ref, vc_ref, o_ref, *, group, sub_rows=256):
    kvs = [(kl_ref[...], vl_ref[...]), (kc_ref[...], vc_ref[...])]
    tq = q_ref.shape[0]
    sub = min(tq, sub_rows)
    for qs in range(tq // sub):
        rows = slice(qs * sub, (qs + 1) * sub)
        for g in range(group):
            cols = slice(g * HEAD_DIM, (g + 1) * HEAD_DIM)
            o_ref[rows, cols] = _attend(q_ref[rows, cols], kvs).astype(BF16)


def _gqa_attention(qkv_lat, qkv_ctx, nb, nq, nc):
    q_heads = D_MODEL // HEAD_DIM
    kv_heads = q_heads // 4
    group = q_heads // kv_heads
    gw = group * HEAD_DIM
    tq = 512
    nqb = nq // tq
    k_off = D_MODEL // HEAD_DIM
    v_off = k_off + kv_heads
    kern = functools.partial(_gqa_attn_kernel, group=group)
    return pl.pallas_call(
        kern,
        grid=(nb, kv_heads, nqb),
        in_specs=[
            pl.BlockSpec((tq, gw), lambda b, h, i: (b * nqb + i, h)),
            pl.BlockSpec((nq, HEAD_DIM), lambda b, h, i: (b, k_off + h)),
            pl.BlockSpec((nc, HEAD_DIM), lambda b, h, i: (b, k_off + h)),
            pl.BlockSpec((nq, HEAD_DIM), lambda b, h, i: (b, v_off + h)),
            pl.BlockSpec((nc, HEAD_DIM), lambda b, h, i: (b, v_off + h)),
        ],
        out_specs=pl.BlockSpec((tq, gw), lambda b, h, i: (b * nqb + i, h)),
        out_shape=jax.ShapeDtypeStruct((nb * nq, D_MODEL), BF16),
        compiler_params=_cparams(("arbitrary", "arbitrary", "arbitrary")),
        name="gqa_attn",
    )(qkv_lat, qkv_lat, qkv_ctx, qkv_lat, qkv_ctx)


def _proj_res_kernel(o_ref, w_ref, res_ref, gate_ref, out_ref):
    acc = _bdot(o_ref[...], w_ref[...].astype(BF16))
    out_ref[...] = res_ref[...] + gate_ref[...] * acc


def _proj_residual(o, w, res, mod, mod_row_fn, gate_idx):
    r, k = o.shape
    n = w.shape[1]
    tm, tn = 1024, 512
    return pl.pallas_call(
        _proj_res_kernel,
        grid=(r // tm, n // tn),
        in_specs=[
            pl.BlockSpec((tm, k), lambda i, j: (i, 0)),
            pl.BlockSpec((k, tn), lambda i, j: (0, j)),
            pl.BlockSpec((tm, tn), lambda i, j: (i, j)),
            pl.BlockSpec((None, None, 1, tn), lambda i, j: (mod_row_fn(i), gate_idx, 0, j)),
        ],
        out_specs=pl.BlockSpec((tm, tn), lambda i, j: (i, j)),
        out_shape=jax.ShapeDtypeStruct((r, n), F32),
        compiler_params=_cparams(("arbitrary", "arbitrary")),
        name="out_proj",
    )(o, w, res, mod)


def _route_kernel(x_ref, g_ref, sh_ref, sc_ref, rt_ref, tri_ref, h_ref, gate_ref, slot_ref, lg_scr,
                  *, tc, n, cap):
    c = pl.program_id(1)
    h = _norm_mod(x_ref[...], g_ref[...], sh_ref[...], sc_ref[...])
    h_ref[...] = h.astype(BF16)
    h_hi, h_lo = _split_bf16(h)
    r_hi, r_lo = _split_bf16(rt_ref[...])
    lg_scr[c] = _dot_nt(r_hi, h_hi) + _dot_nt(r_hi, h_lo) + _dot_nt(r_lo, h_hi)

    @pl.when(c == n // tc - 1)
    def _():
        lg = jnp.concatenate([lg_scr[i] for i in range(n // tc)], axis=1)
        mx = jnp.max(lg, axis=0, keepdims=True)
        ex = jnp.exp(lg - mx)
        aff = ex / jnp.sum(ex, axis=0, keepdims=True)
        bits = pltpu.bitcast(aff, jnp.int32)
        thr = jnp.zeros((N_EXPERTS, 1), jnp.int32)
        for bit in range(30, -1, -1):
            cand = thr | jnp.int32(1 << bit)
            cnt = jnp.sum(jnp.where(bits >= cand, 1.0, 0.0), axis=1, keepdims=True)
            thr = jnp.where(cnt >= cap, cand, thr)
        gt = bits > thr
        eq = bits == thr
        need = cap - jnp.sum(jnp.where(gt, 1.0, 0.0), axis=1, keepdims=True)
        tri = tri_ref[...]
        eq_rank = _bdot(jnp.where(eq, 1.0, 0.0).astype(BF16), tri)
        sel = gt | (eq & (eq_rank < need))
        slot = _bdot(jnp.where(sel, 1.0, 0.0).astype(BF16), tri)
        gate_ref[...] = jnp.where(sel, aff, 0.0)
        slot_ref[...] = jnp.where(sel, slot.astype(jnp.int32), -1)


def _route(x, gain, mod, mod_row_fn, router_t, tri, nb, n, cap):
    k = x.shape[1]
    tc = 256
    nch = n // tc
    kern = functools.partial(_route_kernel, tc=tc, n=n, cap=cap)
    return pl.pallas_call(
        kern,
        grid=(nb, nch),
        in_specs=[
            pl.BlockSpec((tc, k), lambda b, c: (b * nch + c, 0)),
            pl.BlockSpec((1, k), lambda b, c: (0, 0)),
            pl.BlockSpec((None, None, 1, k), lambda b, c: (mod_row_fn(b), 3, 0, 0)),
            pl.BlockSpec((None, None, 1, k), lambda b, c: (mod_row_fn(b), 4, 0, 0)),
            pl.BlockSpec((N_EXPERTS, k), lambda b, c: (0, 0)),
            pl.BlockSpec((n, n), lambda b, c: (0, 0)),
        ],
        out_specs=[
            pl.BlockSpec((tc, k), lambda b, c: (b * nch + c, 0)),
            pl.BlockSpec((None, N_EXPERTS, n), lambda b, c: (b, 0, 0)),
            pl.BlockSpec((None, N_EXPERTS, n), lambda b, c: (b, 0, 0)),
        ],
        out_shape=[
            jax.ShapeDtypeStruct((nb * n, k), BF16),
            jax.ShapeDtypeStruct((nb, N_EXPERTS, n), F32),
            jax.ShapeDtypeStruct((nb, N_EXPERTS, n), jnp.int32),
        ],
        scratch_shapes=[pltpu.VMEM((nch, N_EXPERTS, tc), F32)],
        compiler_params=_cparams(("arbitrary", "arbitrary")),
        name="route",
    )(x, gain.reshape(1, k), mod, mod, router_t, tri)


def _gather_kernel(slot_ref, gate_ref, h_ref, xs_ref, gs_ref, *, cap, n):
    onehot = lax.broadcasted_iota(jnp.int32, (cap, n), 0) == slot_ref[...]
    xs_ref[...] = _bdot(jnp.where(onehot, 1.0, 0.0).astype(BF16), h_ref[...]).astype(BF16)
    gs_ref[...] = jnp.sum(jnp.where(onehot, gate_ref[...], 0.0), axis=1, keepdims=True)


def _gather(slot_t, gate_t, h, nb, n, cap):
    k = h.shape[1]
    kern = functools.partial(_gather_kernel, cap=cap, n=n)
    row_spec = pl.BlockSpec((None, None, 1, n), lambda b, e: (b, e, 0, 0))
    return pl.pallas_call(
        kern,
        grid=(nb, N_EXPERTS),
        in_specs=[row_spec, row_spec, pl.BlockSpec((n, k), lambda b, e: (b, 0))],
        out_specs=[pl.BlockSpec((None, cap, k), lambda b, e: (e, b, 0)),
                   pl.BlockSpec((None, cap, 1), lambda b, e: (e, b, 0))],
        out_shape=[jax.ShapeDtypeStruct((N_EXPERTS, nb * cap, k), BF16),
                   jax.ShapeDtypeStruct((N_EXPERTS, nb * cap, 1), F32)],
        compiler_params=_cparams(("arbitrary", "arbitrary")),
        name="gather",
    )(slot_t.reshape(nb, N_EXPERTS, 1, n), gate_t.reshape(nb, N_EXPERTS, 1, n), h)


def _ffn_kernel(xs_ref, gs_ref, wg_ref, wu_ref, wd_ref, y_ref, hid_scr, *, nf, tf):
    s = pl.program_id(2)

    @pl.when(s < nf)
    def _():
        xs = xs_ref[...]
        a = _bdot(xs, wg_ref[...].astype(BF16))
        u = _bdot(xs, wu_ref[...].astype(BF16))
        hid_scr[s] = (_silu(a) * u * gs_ref[...]).astype(BF16)

    @pl.when(s >= nf)
    def _():
        acc = _bdot(hid_scr[0], wd_ref[0:tf, :].astype(BF16))
        for f in range(1, nf):
            acc = acc + _bdot(hid_scr[f], wd_ref[f * tf:(f + 1) * tf, :].astype(BF16))
        y_ref[...] = acc.astype(BF16)


def _expert_ffn(xs, gs, wg, wu, wd):
    e, m, k = xs.shape
    ff = wg.shape[2]
    tm = min(m, 1024)
    tf, td = 256, 1024
    nf, nd = ff // tf, k // td
    kern = functools.partial(_ffn_kernel, nf=nf, tf=tf)
    return pl.pallas_call(
        kern,
        grid=(e, m // tm, nf + nd),
        in_specs=[
            pl.BlockSpec((None, tm, k), lambda ei, mi, s: (ei, mi, 0)),
            pl.BlockSpec((None, tm, 1), lambda ei, mi, s: (ei, mi, 0)),
            pl.BlockSpec((None, k, tf), lambda ei, mi, s: (ei, 0, jnp.minimum(s, nf - 1))),
            pl.BlockSpec((None, k, tf), lambda ei, mi, s: (ei, 0, jnp.minimum(s, nf - 1))),
            pl.BlockSpec((None, ff, td), lambda ei, mi, s: (ei, 0, jnp.maximum(s - nf, 0))),
        ],
        out_specs=pl.BlockSpec((None, tm, td), lambda ei, mi, s: (ei, mi, jnp.maximum(s - nf, 0))),
        out_shape=jax.ShapeDtypeStruct((e, m, k), BF16),
        scratch_shapes=[pltpu.VMEM((nf, tm, tf), BF16)],
        compiler_params=_cparams(("arbitrary", "arbitrary", "arbitrary")),
        name="expert_ffn",
    )(xs, gs, wg, wu, wd)


def _combine_kernel(slot_ref, y_ref, x_ref, g2_ref, o_ref, acc_scr, *, tt, cap, cap_pad, eg):
    e = pl.program_id(2)

    @pl.when(e == 0)
    def _():
        acc_scr[...] = jnp.zeros_like(acc_scr)

    slot_iota = lax.broadcasted_iota(jnp.int32, (cap_pad, tt), 0)
    onehot_t = jnp.concatenate([jnp.where(slot_iota == slot_ref[i], 1.0, 0.0) for i in range(eg)], axis=0)
    onehot = onehot_t.T[:, 0:eg * cap].astype(BF16)
    acc_scr[...] += _bdot(onehot, y_ref[...].reshape(eg * cap, y_ref.shape[2]))

    @pl.when(e == N_EXPERTS // eg - 1)
    def _():
        o_ref[...] = x_ref[...] + g2_ref[...] * acc_scr[...]


def _combine(slot_t, y, x, mod, mod_row_fn, nb, n, cap):
    k = x.shape[1]
    tt = 256
    ntb = n // tt
    cap_pad = max(cap, 128)
    eg = 4 if cap_pad == cap else 1
    kern = functools.partial(_combine_kernel, tt=tt, cap=cap, cap_pad=cap_pad, eg=eg)
    return pl.pallas_call(
        kern,
        grid=(nb, ntb, N_EXPERTS // eg),
        in_specs=[
            pl.BlockSpec((None, eg, 1, tt), lambda b, t, e: (b, e, 0, t)),
            pl.BlockSpec((eg, cap, k), lambda b, t, e: (e, b, 0)),
            pl.BlockSpec((tt, k), lambda b, t, e: (b * ntb + t, 0)),
            pl.BlockSpec((None, None, 1, k), lambda b, t, e: (mod_row_fn(b), 5, 0, 0)),
        ],
        out_specs=pl.BlockSpec((tt, k), lambda b, t, e: (b * ntb + t, 0)),
        out_shape=jax.ShapeDtypeStruct((nb * n, k), F32),
        scratch_shapes=[pltpu.VMEM((tt, k), F32)],
        compiler_params=_cparams(("arbitrary", "arbitrary", "arbitrary")),
        name="combine",
    )(slot_t.reshape(nb, N_EXPERTS, 1, n), y, x, mod)


def _ec_moe_residual(x, gain, mod, mod_row_fn, router, wg, wu, wd, nb, n):
    cap = EC_CAPACITY_FACTOR * n // N_EXPERTS
    idx = jnp.arange(n, dtype=jnp.int32)
    tri = (idx[:, None] < idx[None, :]).astype(BF16)
    h, gate_t, slot_t = _route(x, gain, mod, mod_row_fn, router.T, tri, nb, n, cap)
    xs, gs = _gather(slot_t, gate_t, h, nb, n, cap)
    y = _expert_ffn(xs, gs, wg, wu, wd)
    return _combine(slot_t, y, x, mod, mod_row_fn, nb, n, cap)


def _rope_tables(n, tm):
    rows = n // GRID_W
    row = jnp.repeat(jnp.arange(rows, dtype=F32), GRID_W)
    col = jnp.tile(jnp.arange(GRID_W, dtype=F32), rows)
    axis_dim = HEAD_DIM // 2
    inv_freq = ROPE_THETA ** (-jnp.arange(0, axis_dim, 2, dtype=F32) / axis_dim)
    ang = jnp.concatenate([row[:, None] * inv_freq, col[:, None] * inv_freq], axis=-1)
    cos, sin = jnp.cos(ang), jnp.sin(ang)
    cos_full = jnp.concatenate([cos, cos], axis=-1).reshape(n // tm, tm, HEAD_DIM)
    sin_signed = jnp.concatenate([-sin, sin], axis=-1).reshape(n // tm, tm, HEAD_DIM)
    return cos_full, sin_signed


def kernel(x, c, ctx, c_ctx,
           mod_w_0, mod_b_0, norm1_0, norm2_0, wqkv_0, wo_0, qnorm_0, knorm_0,
           lam_q1_0, lam_k1_0, lam_q2_0, lam_k2_0, subln_0, router_0, we_gate_0, we_up_0, we_down_0,
           mod_w_1, mod_b_1, norm1_1, norm2_1, wqkv_1, wo_1, qnorm_1, knorm_1,
           router_1, we_gate_1, we_up_1, we_down_1):
    nb, nq, d = x.shape
    nc = ctx.shape[1]
    tm = 1024
    x_lat = x.reshape(nb * nq, d)
    x_ctx = ctx.reshape(nb * nc, d)
    cond = jnp.zeros((MOD_ROWS, d), F32).at[:nb].set(c).at[nb].set(c_ctx)
    lat_row = lambda i: i // (nq // tm)
    ctx_row = lambda i: nb
    lat_b = lambda b: b
    cos, sin = _rope_tables(nq, tm)
    cos_id = jnp.ones((1, tm, HEAD_DIM), F32)
    sin_id = jnp.zeros((1, tm, HEAD_DIM), F32)
    q_scale = math.log2(math.e) / math.sqrt(HEAD_DIM)

    mod0 = _adaln(cond, mod_w_0, mod_b_0)
    n_heads2 = d // HEAD_DIM
    colgain0 = jnp.concatenate([jnp.tile(qnorm_0 * q_scale, n_heads2), jnp.tile(knorm_0, n_heads2),
                                jnp.ones((d,), F32)]).reshape(1, 3 * d)
    qkv_lat = _qkv_proj(x_lat, norm1_0, mod0, lat_row, wqkv_0, colgain0, cos, sin, 2 * d)
    qkv_ctx = _qkv_proj(x_ctx, norm1_0, mod0, ctx_row, wqkv_0, colgain0, cos_id, sin_id, 2 * d)
    lam_init = 0.8 - 0.6 * math.exp(-0.3 * 0)
    lam_params = jnp.stack([lam_q1_0, lam_k1_0, lam_q2_0, lam_k2_0])
    o_lat = _diff_attention(lam_params, subln_0, qkv_lat, [(qkv_lat, nq), (qkv_ctx, nc)], nb, nq, lam_init)
    o_ctx = _diff_attention(lam_params, subln_0, qkv_ctx, [(qkv_ctx, nc)], nb, nc, lam_init)
    x_lat = _proj_residual(o_lat, wo_0, x_lat, mod0, lat_row, 2)
    x_ctx = _proj_residual(o_ctx, wo_0, x_ctx, mod0, ctx_row, 2)
    x_lat = _ec_moe_residual(x_lat, norm2_0, mod0, lat_b, router_0, we_gate_0, we_up_0, we_down_0, nb, nq)
    x_ctx = _ec_moe_residual(x_ctx, norm2_0, mod0, ctx_row, router_0, we_gate_0, we_up_0, we_down_0, nb, nc)

    mod1 = _adaln(cond, mod_w_1, mod_b_1)
    kv_dim = (wqkv_1.shape[1] - d) // 2
    colgain1 = jnp.concatenate([jnp.tile(qnorm_1 * q_scale, d // HEAD_DIM),
                                jnp.tile(knorm_1, kv_dim // HEAD_DIM),
                                jnp.ones((kv_dim,), F32)]).reshape(1, d + 2 * kv_dim)
    qkv_lat = _qkv_proj(x_lat, norm1_1, mod1, lat_row, wqkv_1, colgain1, cos, sin, d + kv_dim)
    qkv_ctx = _qkv_proj(x_ctx, norm1_1, mod1, ctx_row, wqkv_1, colgain1, cos_id, sin_id, d + kv_dim)
    o_lat = _gqa_attention(qkv_lat, qkv_ctx, nb, nq, nc)
    x_lat = _proj_residual(o_lat, wo_1, x_lat, mod1, lat_row, 2)
    x_lat = _ec_moe_residual(x_lat, norm2_1, mod1, lat_b, router_1, we_gate_1, we_up_1, we_down_1, nb, nq)
    return x_lat.reshape(nb, nq, d)
```

```python
import functools
import math

import jax
import jax.numpy as jnp
from jax import lax
from jax.experimental import pallas as pl
from jax.experimental.pallas import tpu as pltpu

F32 = jnp.float32
BF16 = jnp.bfloat16

D_MODEL = 4096
GRID_W = 64
HEAD_DIM = 128
N_EXPERTS = 16
EC_CAPACITY_FACTOR = 2
EXPERT_FF = D_MODEL // 4
ROPE_THETA = 10000.0
NORM_EPS = 1e-6
N_MOD = 6
MOD_ROWS = 16
VMEM_LIMIT = 58 * 1024 * 1024


def _cparams(sem):
    return pltpu.CompilerParams(dimension_semantics=sem, vmem_limit_bytes=VMEM_LIMIT)


def _bdot(a, b):
    return jnp.dot(a, b, preferred_element_type=F32)


def _dot_nt(a, b):
    return lax.dot_general(a, b, (((1,), (1,)), ((), ())), preferred_element_type=F32)


def _silu(x):
    return x * (1.0 / (1.0 + jnp.exp(-x)))


def _split_bf16(x):
    hi = x.astype(BF16)
    lo = (x - hi.astype(F32)).astype(BF16)
    return hi, lo


def _adaln_kernel(c_ref, w_ref, b_ref, o_ref):
    s = _silu(c_ref[...]).astype(BF16)
    o_ref[...] = _bdot(s, w_ref[...].astype(BF16)) + b_ref[...]


def _adaln(cond, w, b):
    n = w.shape[1]
    tn = 512
    out = pl.pallas_call(
        _adaln_kernel,
        grid=(n // tn,),
        in_specs=[
            pl.BlockSpec((MOD_ROWS, D_MODEL), lambda j: (0, 0)),
            pl.BlockSpec((D_MODEL, tn), lambda j: (0, j)),
            pl.BlockSpec((1, tn), lambda j: (0, j)),
        ],
        out_specs=pl.BlockSpec((MOD_ROWS, tn), lambda j: (0, j)),
        out_shape=jax.ShapeDtypeStruct((MOD_ROWS, n), F32),
        compiler_params=_cparams(("arbitrary",)),
        name="adaln",
    )(cond, w, b.reshape(1, n))
    return out.reshape(MOD_ROWS, N_MOD, 1, D_MODEL)


def _norm_mod(xv, g, shift, scale):
    ms = jnp.mean(xv * xv, axis=-1, keepdims=True)
    y = xv * lax.rsqrt(ms + NORM_EPS) * g
    return y * (1.0 + scale) + shift


def _qkv_kernel(x_ref, g_ref, sh_ref, sc_ref, w_ref, cg_ref, cos_ref, sin_ref, o_ref, h_scr,
                *, tm, tn, n_qk_tiles, chunk, row_group):
    j = pl.program_id(1)

    def qk_tile(make_h):
        w = w_ref[...].astype(BF16)
        for r in range(tm // row_group):
            rows = slice(r * row_group, (r + 1) * row_group)
            if make_h:
                for c in range(row_group // chunk):
                    rr = slice(r * row_group + c * chunk, r * row_group + (c + 1) * chunk)
                    h = _norm_mod(x_ref[rr, :], g_ref[...], sh_ref[...], sc_ref[...])
                    h_scr[rr, :] = h.astype(BF16)
            acc = _bdot(h_scr[rows, :], w)
            cos = cos_ref[rows, :]
            sin = sin_ref[rows, :]
            for gi in range(tn // HEAD_DIM):
                cols = slice(gi * HEAD_DIM, (gi + 1) * HEAD_DIM)
                a = acc[:, cols]
                ms = jnp.mean(a * a, axis=-1, keepdims=True)
                a = a * lax.rsqrt(ms + NORM_EPS) * cg_ref[:, cols]
                a = a * cos + pltpu.roll(a, HEAD_DIM // 2, axis=1) * sin
                o_ref[rows, cols] = a.astype(BF16)

    @pl.when(j == 0)
    def _():
        qk_tile(True)

    @pl.when((j > 0) & (j < n_qk_tiles))
    def _():
        qk_tile(False)

    @pl.when(j >= n_qk_tiles)
    def _():
        o_ref[...] = _bdot(h_scr[...], w_ref[...].astype(BF16)).astype(BF16)


def _qkv_proj(x, gain, mod, mod_row_fn, w, colgain, cos, sin, n_qk_cols, col_start=0):
    r, k = x.shape
    tm, tn = 1024, 512
    n = w.shape[1] - col_start
    j0 = col_start // tn
    npos = cos.shape[0]
    kern = functools.partial(_qkv_kernel, tm=tm, tn=tn, n_qk_tiles=(n_qk_cols - col_start) // tn, chunk=32,
                             row_group=256)
    return pl.pallas_call(
        kern,
        grid=(r // tm, n // tn),
        in_specs=[
            pl.BlockSpec((tm, k), lambda i, j: (i, 0), pipeline_mode=pl.Buffered(1)),
            pl.BlockSpec((1, k), lambda i, j: (0, 0)),
            pl.BlockSpec((None, None, 1, k), lambda i, j: (mod_row_fn(i), 0, 0, 0)),
            pl.BlockSpec((None, None, 1, k), lambda i, j: (mod_row_fn(i), 1, 0, 0)),
            pl.BlockSpec((k, tn), lambda i, j: (0, j0 + j)),
            pl.BlockSpec((1, tn), lambda i, j: (0, j0 + j)),
            pl.BlockSpec((None, tm, HEAD_DIM), lambda i, j: (i % npos, 0, 0)),
            pl.BlockSpec((None, tm, HEAD_DIM), lambda i, j: (i % npos, 0, 0)),
        ],
        out_specs=pl.BlockSpec((tm, tn), lambda i, j: (i, j)),
        out_shape=jax.ShapeDtypeStruct((r, n), BF16),
        scratch_shapes=[pltpu.VMEM((tm, k), BF16)],
        compiler_params=_cparams(("arbitrary", "arbitrary")),
        name="qkv_proj",
    )(x, gain.reshape(1, k), mod, mod, w, colgain, cos, sin)


def _attend(q, kvs, row_block=16):
    tq = q.shape[0]
    ss = [_dot_nt(q, k) for k, _ in kvs]
    p_blocks = [[] for _ in kvs]
    l_blocks = []
    for r in range(tq // row_block):
        rows = slice(r * row_block, (r + 1) * row_block)
        sb = [s[rows, :] for s in ss]
        m = functools.reduce(jnp.maximum, [jnp.max(x, axis=-1, keepdims=True) for x in sb])
        pb = [jnp.exp2(x - m) for x in sb]
        l_blocks.append(functools.reduce(lambda a, b: a + b,
                                         [jnp.sum(p, axis=-1, keepdims=True) for p in pb]))
        for blocks, p in zip(p_blocks, pb):
            blocks.append(p.astype(BF16))
    l = jnp.concatenate(l_blocks, axis=0)
    o = functools.reduce(lambda a, b: a + b,
                         [_bdot(jnp.concatenate(blocks, axis=0), v) for blocks, (_, v) in zip(p_blocks, kvs)])
    return o / l


def _diff_lambda(lam_ref, lam_init):
    lp = lam_ref[...]
    s1 = jnp.sum(lp[0:1, :] * lp[1:2, :], axis=-1, keepdims=True)
    s2 = jnp.sum(lp[2:3, :] * lp[3:4, :], axis=-1, keepdims=True)
    return jnp.exp(s1) - jnp.exp(s2) + lam_init


def _diff_attn_kernel(*refs, lam_init, n_kv, sub_rows=256):
    lam_ref, subln_ref, q_ref = refs[:3]
    k_refs = refs[3:3 + n_kv]
    v_refs = refs[3 + n_kv:3 + 2 * n_kv]
    o_ref = refs[3 + 2 * n_kv]
    lam = _diff_lambda(lam_ref, lam_init)
    tq = q_ref.shape[0]
    sub = min(tq, sub_rows)
    for qs in range(tq // sub):
        rows = slice(qs * sub, (qs + 1) * sub)
        outs = []
        for t in range(2):
            cols = slice(t * HEAD_DIM, (t + 1) * HEAD_DIM)
            kvs = [(kr[:, cols], vr[...]) for kr, vr in zip(k_refs, v_refs)]
            outs.append(_attend(q_ref[rows, cols], kvs))
        o = outs[0] - lam * outs[1]
        ms = jnp.mean(o * o, axis=-1, keepdims=True)
        o = o * lax.rsqrt(ms + NORM_EPS) * subln_ref[...] * (1.0 - lam_init)
        o_ref[rows, :] = o.astype(BF16)


def _diff_attention(lam_params, subln, qkv_q, kv_sources, nb, nq, lam_init):
    heads = D_MODEL // (2 * HEAD_DIM)
    hw = 2 * HEAD_DIM
    tq = min(nq, 1024)
    nqb = nq // tq
    n_kv = len(kv_sources)
    in_specs = [
        pl.BlockSpec((4, HEAD_DIM), lambda b, h, i: (0, 0)),
        pl.BlockSpec((1, hw), lambda b, h, i: (0, 0)),
        pl.BlockSpec((tq, hw), lambda b, h, i: (b * nqb + i, h)),
    ]
    args = [lam_params, subln.reshape(1, hw), qkv_q]
    for arr, nk in kv_sources:
        in_specs.append(pl.BlockSpec((nk, hw), lambda b, h, i: (b, heads + h)))
        args.append(arr)
    for arr, nk in kv_sources:
        in_specs.append(pl.BlockSpec((nk, hw), lambda b, h, i: (b, 2 * heads + h)))
        args.append(arr)
    kern = functools.partial(_diff_attn_kernel, lam_init=lam_init, n_kv=n_kv)
    return pl.pallas_call(
        kern,
        grid=(nb, heads, nqb),
        in_specs=in_specs,
        out_specs=pl.BlockSpec((tq, hw), lambda b, h, i: (b * nqb + i, h)),
        out_shape=jax.ShapeDtypeStruct((nb * nq, D_MODEL), BF16),
        compiler_params=_cparams(("arbitrary", "arbitrary", "arbitrary")),
        name="diff_attn",
    )(*args)


def _gqa_attn_kernel(q_ref, kl_ref, kc_ref, vl_ref, vc_ref, o_ref, *, group, sub_rows=256):
    kvs = [(kl_ref[...], vl_ref[...]), (kc_ref[...], vc_ref[...])]
    tq = q_ref.shape[0]
    sub = min(tq, sub_rows)
    for qs in range(tq // sub):
        rows = slice(qs * sub, (qs + 1) * sub)
        for g in range(group):
            cols = slice(g * HEAD_DIM, (g + 1) * HEAD_DIM)
            o_ref[rows, cols] = _attend(q_ref[rows, cols], kvs).astype(BF16)


def _gqa_attention(qkv_lat, kv_ctx, nb, nq, nc):
    q_heads = D_MODEL // HEAD_DIM
    kv_heads = q_heads // 4
    group = q_heads // kv_heads
    gw = group * HEAD_DIM
    tq = 512
    nqb = nq // tq
    k_off = D_MODEL // HEAD_DIM
    v_off = k_off + kv_heads
    kern = functools.partial(_gqa_attn_kernel, group=group)
    return pl.pallas_call(
        kern,
        grid=(nb, kv_heads, nqb),
        in_specs=[
            pl.BlockSpec((tq, gw), lambda b, h, i: (b * nqb + i, h)),
            pl.BlockSpec((nq, HEAD_DIM), lambda b, h, i: (b, k_off + h)),
            pl.BlockSpec((nc, HEAD_DIM), lambda b, h, i: (b, h)),
            pl.BlockSpec((nq, HEAD_DIM), lambda b, h, i: (b, v_off + h)),
            pl.BlockSpec((nc, HEAD_DIM), lambda b, h, i: (b, kv_heads + h)),
        ],
        out_specs=pl.BlockSpec((tq, gw), lambda b, h, i: (b * nqb + i, h)),
        out_shape=jax.ShapeDtypeStruct((nb * nq, D_MODEL), BF16),
        compiler_params=_cparams(("arbitrary", "arbitrary", "arbitrary")),
        name="gqa_attn",
    )(qkv_lat, qkv_lat, kv_ctx, qkv_lat, kv_ctx)


def _proj_res_kernel(o_ref, w_ref, res_ref, gate_ref, out_ref):
    acc = _bdot(o_ref[...], w_ref[...].astype(BF16))
    out_ref[...] = res_ref[...] + gate_ref[...] * acc


def _proj_residual(o, w, res, mod, mod_row_fn, gate_idx):
    r, k = o.shape
    n = w.shape[1]
    tm, tn = 1024, 512
    return pl.pallas_call(
        _proj_res_kernel,
        grid=(r // tm, n // tn),
        in_specs=[
            pl.BlockSpec((tm, k), lambda i, j: (i, 0)),
            pl.BlockSpec((k, tn), lambda i, j: (0, j)),
            pl.BlockSpec((tm, tn), lambda i, j: (i, j)),
            pl.BlockSpec((None, None, 1, tn), lambda i, j: (mod_row_fn(i), gate_idx, 0, j)),
        ],
        out_specs=pl.BlockSpec((tm, tn), lambda i, j: (i, j)),
        out_shape=jax.ShapeDtypeStruct((r, n), F32),
        compiler_params=_cparams(("arbitrary", "arbitrary")),
        name="out_proj",
    )(o, w, res, mod)


def _route_kernel(x_ref, g_ref, sh_ref, sc_ref, rt_ref, tri_ref, h_ref, gate_ref, slot_ref, lg_scr,
                  *, tc, n, cap):
    c = pl.program_id(1)
    h = _norm_mod(x_ref[...], g_ref[...], sh_ref[...], sc_ref[...])
    h_ref[...] = h.astype(BF16)
    h_hi, h_lo = _split_bf16(h)
    r_hi, r_lo = _split_bf16(rt_ref[...])
    lg_scr[c] = _dot_nt(r_hi, h_hi) + _dot_nt(r_hi, h_lo) + _dot_nt(r_lo, h_hi)

    @pl.when(c == n // tc - 1)
    def _():
        lg = jnp.concatenate([lg_scr[i] for i in range(n // tc)], axis=1)
        mx = jnp.max(lg, axis=0, keepdims=True)
        ex = jnp.exp(lg - mx)
        aff = ex / jnp.sum(ex, axis=0, keepdims=True)
        bits = pltpu.bitcast(aff, jnp.int32)
        thr = jnp.zeros((N_EXPERTS, 1), jnp.int32)
        for bit in range(30, -1, -1):
            cand = thr | jnp.int32(1 << bit)
            cnt = jnp.sum(jnp.where(bits >= cand, 1.0, 0.0), axis=1, keepdims=True)
            thr = jnp.where(cnt >= cap, cand, thr)
        gt = bits > thr
        eq = bits == thr
        need = cap - jnp.sum(jnp.where(gt, 1.0, 0.0), axis=1, keepdims=True)
        tri = tri_ref[...]
        eq_rank = _bdot(jnp.where(eq, 1.0, 0.0).astype(BF16), tri)
        sel = gt | (eq & (eq_rank < need))
        slot = _bdot(jnp.where(sel, 1.0, 0.0).astype(BF16), tri)
        gate_ref[...] = jnp.where(sel, aff, 0.0)
        slot_ref[...] = jnp.where(sel, slot.astype(jnp.int32), -1)


def _route(x, gain, mod, mod_row_fn, router_t, tri, nb, n, cap):
    k = x.shape[1]
    tc = 256
    nch = n // tc
    kern = functools.partial(_route_kernel, tc=tc, n=n, cap=cap)
    return pl.pallas_call(
        kern,
        grid=(nb, nch),
        in_specs=[
            pl.BlockSpec((tc, k), lambda b, c: (b * nch + c, 0)),
            pl.BlockSpec((1, k), lambda b, c: (0, 0)),
            pl.BlockSpec((None, None, 1, k), lambda b, c: (mod_row_fn(b), 3, 0, 0)),
            pl.BlockSpec((None, None, 1, k), lambda b, c: (mod_row_fn(b), 4, 0, 0)),
            pl.BlockSpec((N_EXPERTS, k), lambda b, c: (0, 0)),
            pl.BlockSpec((n, n), lambda b, c: (0, 0)),
        ],
        out_specs=[
            pl.BlockSpec((tc, k), lambda b, c: (b * nch + c, 0)),
            pl.BlockSpec((None, N_EXPERTS, n), lambda b, c: (b, 0, 0)),
            pl.BlockSpec((None, N_EXPERTS, n), lambda b, c: (b, 0, 0)),
        ],
        out_shape=[
            jax.ShapeDtypeStruct((nb * n, k), BF16),
            jax.ShapeDtypeStruct((nb, N_EXPERTS, n), F32),
            jax.ShapeDtypeStruct((nb, N_EXPERTS, n), jnp.int32),
        ],
        scratch_shapes=[pltpu.VMEM((nch, N_EXPERTS, tc), F32)],
        compiler_params=_cparams(("arbitrary", "arbitrary")),
        name="route",
    )(x, gain.reshape(1, k), mod, mod, router_t, tri)


def _gather_kernel(slot_ref, gate_ref, h_ref, xs_ref, gs_ref, *, cap, n):
    onehot = lax.broadcasted_iota(jnp.int32, (cap, n), 0) == slot_ref[...]
    xs_ref[...] = _bdot(jnp.where(onehot, 1.0, 0.0).astype(BF16), h_ref[...]).astype(BF16)
    gs_ref[...] = jnp.sum(jnp.where(onehot, gate_ref[...], 0.0), axis=1, keepdims=True)


def _gather(slot_t, gate_t, h, nb, n, cap):
    k = h.shape[1]
    kern = functools.partial(_gather_kernel, cap=cap, n=n)
    row_spec = pl.BlockSpec((None, None, 1, n), lambda b, e: (b, e, 0, 0))
    return pl.pallas_call(
        kern,
        grid=(nb, N_EXPERTS),
        in_specs=[row_spec, row_spec, pl.BlockSpec((n, k), lambda b, e: (b, 0))],
        out_specs=[pl.BlockSpec((None, cap, k), lambda b, e: (e, b, 0)),
                   pl.BlockSpec((None, cap, 1), lambda b, e: (e, b, 0))],
        out_shape=[jax.ShapeDtypeStruct((N_EXPERTS, nb * cap, k), BF16),
                   jax.ShapeDtypeStruct((N_EXPERTS, nb * cap, 1), F32)],
        compiler_params=_cparams(("arbitrary", "arbitrary")),
        name="gather",
    )(slot_t.reshape(nb, N_EXPERTS, 1, n), gate_t.reshape(nb, N_EXPERTS, 1, n), h)


def _ffn_kernel(xs_ref, gs_ref, wg_ref, wu_ref, wd_ref, y_ref, hid_scr, *, nf, tf):
    s = pl.program_id(2)

    @pl.when(s < nf)
    def _():
        xs = xs_ref[...]
        a = _bdot(xs, wg_ref[...].astype(BF16))
        u = _bdot(xs, wu_ref[...].astype(BF16))
        hid_scr[s] = (_silu(a) * u * gs_ref[...]).astype(BF16)

    @pl.when(s >= nf)
    def _():
        acc = _bdot(hid_scr[0], wd_ref[0:tf, :].astype(BF16))
        for f in range(1, nf):
            acc = acc + _bdot(hid_scr[f], wd_ref[f * tf:(f + 1) * tf, :].astype(BF16))
        y_ref[...] = acc.astype(BF16)


def _expert_ffn(xs, gs, wg, wu, wd):
    e, m, k = xs.shape
    ff = wg.shape[2]
    tm = min(m, 1024)
    tf, td = 256, 1024
    nf, nd = ff // tf, k // td
    kern = functools.partial(_ffn_kernel, nf=nf, tf=tf)
    return pl.pallas_call(
        kern,
        grid=(e, m // tm, nf + nd),
        in_specs=[
            pl.BlockSpec((None, tm, k), lambda ei, mi, s: (ei, mi, 0)),
            pl.BlockSpec((None, tm, 1), lambda ei, mi, s: (ei, mi, 0)),
            pl.BlockSpec((None, k, tf), lambda ei, mi, s: (ei, 0, jnp.minimum(s, nf - 1))),
            pl.BlockSpec((None, k, tf), lambda ei, mi, s: (ei, 0, jnp.minimum(s, nf - 1))),
            pl.BlockSpec((None, ff, td), lambda ei, mi, s: (ei, 0, jnp.maximum(s - nf, 0))),
        ],
        out_specs=pl.BlockSpec((None, tm, td), lambda ei, mi, s: (ei, mi, jnp.maximum(s - nf, 0))),
        out_shape=jax.ShapeDtypeStruct((e, m, k), BF16),
        scratch_shapes=[pltpu.VMEM((nf, tm, tf), BF16)],
        compiler_params=_cparams(("arbitrary", "arbitrary", "arbitrary")),
        name="expert_ffn",
    )(xs, gs, wg, wu, wd)


def _combine_kernel(slot_ref, y_ref, x_ref, g2_ref, o_ref, acc_scr, *, tt, cap, cap_pad, eg):
    e = pl.program_id(2)

    @pl.when(e == 0)
    def _():
        acc_scr[...] = jnp.zeros_like(acc_scr)

    slot_iota = lax.broadcasted_iota(jnp.int32, (cap_pad, tt), 0)
    onehot_t = jnp.concatenate([jnp.where(slot_iota == slot_ref[i], 1.0, 0.0) for i in range(eg)], axis=0)
    onehot = onehot_t.T[:, 0:eg * cap].astype(BF16)
    acc_scr[...] += _bdot(onehot, y_ref[...].reshape(eg * cap, y_ref.shape[2]))

    @pl.when(e == N_EXPERTS // eg - 1)
    def _():
        o_ref[...] = x_ref[...] + g2_ref[...] * acc_scr[...]


def _combine(slot_t, y, x, mod, mod_row_fn, nb, n, cap):
    k = x.shape[1]
    tt = min(n, 512)
    ntb = n // tt
    cap_pad = max(cap, 128)
    eg = 2 if cap_pad == cap else 1
    kern = functools.partial(_combine_kernel, tt=tt, cap=cap, cap_pad=cap_pad, eg=eg)
    return pl.pallas_call(
        kern,
        grid=(nb, ntb, N_EXPERTS // eg),
        in_specs=[
            pl.BlockSpec((None, eg, 1, tt), lambda b, t, e: (b, e, 0, t)),
            pl.BlockSpec((eg, cap, k), lambda b, t, e: (e, b, 0)),
            pl.BlockSpec((tt, k), lambda b, t, e: (b * ntb + t, 0)),
            pl.BlockSpec((None, None, 1, k), lambda b, t, e: (mod_row_fn(b), 5, 0, 0)),
        ],
        out_specs=pl.BlockSpec((tt, k), lambda b, t, e: (b * ntb + t, 0)),
        out_shape=jax.ShapeDtypeStruct((nb * n, k), F32),
        scratch_shapes=[pltpu.VMEM((tt, k), F32)],
        compiler_params=_cparams(("arbitrary", "arbitrary", "arbitrary")),
        name="combine",
    )(slot_t.reshape(nb, N_EXPERTS, 1, n), y, x, mod)


def _ec_moe_residual(x, gain, mod, mod_row_fn, router, wg, wu, wd, nb, n):
    cap = EC_CAPACITY_FACTOR * n // N_EXPERTS
    idx = jnp.arange(n, dtype=jnp.int32)
    tri = (idx[:, None] < idx[None, :]).astype(BF16)
    h, gate_t, slot_t = _route(x, gain, mod, mod_row_fn, router.T, tri, nb, n, cap)
    xs, gs = _gather(slot_t, gate_t, h, nb, n, cap)
    y = _expert_ffn(xs, gs, wg, wu, wd)
    return _combine(slot_t, y, x, mod, mod_row_fn, nb, n, cap)


def _rope_tables(n, tm):
    rows = n // GRID_W
    row = jnp.repeat(jnp.arange(rows, dtype=F32), GRID_W)
    col = jnp.tile(jnp.arange(GRID_W, dtype=F32), rows)
    axis_dim = HEAD_DIM // 2
    inv_freq = ROPE_THETA ** (-jnp.arange(0, axis_dim, 2, dtype=F32) / axis_dim)
    ang = jnp.concatenate([row[:, None] * inv_freq, col[:, None] * inv_freq], axis=-1)
    cos, sin = jnp.cos(ang), jnp.sin(ang)
    cos_full = jnp.concatenate([cos, cos], axis=-1).reshape(n // tm, tm, HEAD_DIM)
    sin_signed = jnp.concatenate([-sin, sin], axis=-1).reshape(n // tm, tm, HEAD_DIM)
    return cos_full, sin_signed


def kernel(x, c, ctx, c_ctx,
           mod_w_0, mod_b_0, norm1_0, norm2_0, wqkv_0, wo_0, qnorm_0, knorm_0,
           lam_q1_0, lam_k1_0, lam_q2_0, lam_k2_0, subln_0, router_0, we_gate_0, we_up_0, we_down_0,
           mod_w_1, mod_b_1, norm1_1, norm2_1, wqkv_1, wo_1, qnorm_1, knorm_1,
           router_1, we_gate_1, we_up_1, we_down_1):
    nb, nq, d = x.shape
    nc = ctx.shape[1]
    tm = 1024
    x_lat = x.reshape(nb * nq, d)
    x_ctx = ctx.reshape(nb * nc, d)
    cond = jnp.zeros((MOD_ROWS, d), F32).at[:nb].set(c).at[nb].set(c_ctx)
    lat_row = lambda i: i // (nq // tm)
    ctx_row = lambda i: nb
    lat_b = lambda b: b
    cos, sin = _rope_tables(nq, tm)
    cos_id = jnp.ones((1, tm, HEAD_DIM), F32)
    sin_id = jnp.zeros((1, tm, HEAD_DIM), F32)
    q_scale = math.log2(math.e) / math.sqrt(HEAD_DIM)

    mod0 = _adaln(cond, mod_w_0, mod_b_0)
    n_heads2 = d // HEAD_DIM
    colgain0 = jnp.concatenate([jnp.tile(qnorm_0 * q_scale, n_heads2), jnp.tile(knorm_0, n_heads2),
                                jnp.ones((d,), F32)]).reshape(1, 3 * d)
    qkv_lat = _qkv_proj(x_lat, norm1_0, mod0, lat_row, wqkv_0, colgain0, cos, sin, 2 * d)
    qkv_ctx = _qkv_proj(x_ctx, norm1_0, mod0, ctx_row, wqkv_0, colgain0, cos_id, sin_id, 2 * d)
    lam_init = 0.8 - 0.6 * math.exp(-0.3 * 0)
    lam_params = jnp.stack([lam_q1_0, lam_k1_0, lam_q2_0, lam_k2_0])
    o_lat = _diff_attention(lam_params, subln_0, qkv_lat, [(qkv_lat, nq), (qkv_ctx, nc)], nb, nq, lam_init)
    o_ctx = _diff_attention(lam_params, subln_0, qkv_ctx, [(qkv_ctx, nc)], nb, nc, lam_init)
    x_lat = _proj_residual(o_lat, wo_0, x_lat, mod0, lat_row, 2)
    x_ctx = _proj_residual(o_ctx, wo_0, x_ctx, mod0, ctx_row, 2)
    x_lat = _ec_moe_residual(x_lat, norm2_0, mod0, lat_b, router_0, we_gate_0, we_up_0, we_down_0, nb, nq)
    x_ctx = _ec_moe_residual(x_ctx, norm2_0, mod0, ctx_row, router_0, we_gate_0, we_up_0, we_down_0, nb, nc)

    mod1 = _adaln(cond, mod_w_1, mod_b_1)
    kv_dim = (wqkv_1.shape[1] - d) // 2
    colgain1 = jnp.concatenate([jnp.tile(qnorm_1 * q_scale, d // HEAD_DIM),
                                jnp.tile(knorm_1, kv_dim // HEAD_DIM),
                                jnp.ones((kv_dim,), F32)]).reshape(1, d + 2 * kv_dim)
    qkv_lat = _qkv_proj(x_lat, norm1_1, mod1, lat_row, wqkv_1, colgain1, cos, sin, d + kv_dim)
    kv_ctx = _qkv_proj(x_ctx, norm1_1, mod1, ctx_row, wqkv_1, colgain1, cos_id, sin_id, d + kv_dim, col_start=d)
    o_lat = _gqa_attention(qkv_lat, kv_ctx, nb, nq, nc)
    x_lat = _proj_residual(o_lat, wo_1, x_lat, mod1, lat_row, 2)
    x_lat = _ec_moe_residual(x_lat, norm2_1, mod1, lat_b, router_1, we_gate_1, we_up_1, we_down_1, nb, nq)
    return x_lat.reshape(nb, nq, d)
```

```python
import functools
import math

import jax
import jax.numpy as jnp
from jax import lax
from jax.experimental import pallas as pl
from jax.experimental.pallas import tpu as pltpu

F32 = jnp.float32
BF16 = jnp.bfloat16

D_MODEL = 4096
GRID_W = 64
HEAD_DIM = 128
N_EXPERTS = 16
EC_CAPACITY_FACTOR = 2
EXPERT_FF = D_MODEL // 4
ROPE_THETA = 10000.0
NORM_EPS = 1e-6
N_MOD = 6
MOD_ROWS = 16
VMEM_LIMIT = 58 * 1024 * 1024


def _cparams(sem):
    return pltpu.CompilerParams(dimension_semantics=sem, vmem_limit_bytes=VMEM_LIMIT)


def _bdot(a, b):
    return jnp.dot(a, b, preferred_element_type=F32)


def _dot_nt(a, b):
    return lax.dot_general(a, b, (((1,), (1,)), ((), ())), preferred_element_type=F32)


def _silu(x):
    return x * (1.0 / (1.0 + jnp.exp(-x)))


def _split_bf16(x):
    hi = x.astype(BF16)
    lo = (x - hi.astype(F32)).astype(BF16)
    return hi, lo


def _adaln_kernel(c_ref, w_ref, b_ref, o_ref):
    s = _silu(c_ref[...]).astype(BF16)
    o_ref[...] = _bdot(s, w_ref[...].astype(BF16)) + b_ref[...]


def _adaln(cond, w, b):
    n = w.shape[1]
    tn = 512
    out = pl.pallas_call(
        _adaln_kernel,
        grid=(n // tn,),
        in_specs=[
            pl.BlockSpec((MOD_ROWS, D_MODEL), lambda j: (0, 0)),
            pl.BlockSpec((D_MODEL, tn), lambda j: (0, j)),
            pl.BlockSpec((1, tn), lambda j: (0, j)),
        ],
        out_specs=pl.BlockSpec((MOD_ROWS, tn), lambda j: (0, j)),
        out_shape=jax.ShapeDtypeStruct((MOD_ROWS, n), F32),
        compiler_params=_cparams(("arbitrary",)),
        name="adaln",
    )(cond, w, b.reshape(1, n))
    return out.reshape(MOD_ROWS, N_MOD, 1, D_MODEL)


def _norm_mod(xv, g, shift, scale):
    ms = jnp.mean(xv * xv, axis=-1, keepdims=True)
    y = xv * lax.rsqrt(ms + NORM_EPS) * g
    return y * (1.0 + scale) + shift


def _qkv_kernel(x_ref, g_ref, sh_ref, sc_ref, w_ref, cg_ref, cos_ref, sin_ref, o_ref, h_scr,
                *, tm, tn, n_qk_tiles, chunk, row_group):
    j = pl.program_id(1)

    def qk_tile(make_h):
        w = w_ref[...].astype(BF16)
        for r in range(tm // row_group):
            rows = slice(r * row_group, (r + 1) * row_group)
            if make_h:
                for c in range(row_group // chunk):
                    rr = slice(r * row_group + c * chunk, r * row_group + (c + 1) * chunk)
                    h = _norm_mod(x_ref[rr, :], g_ref[...], sh_ref[...], sc_ref[...])
                    h_scr[rr, :] = h.astype(BF16)
            acc = _bdot(h_scr[rows, :], w)
            cos = cos_ref[rows, :]
            sin = sin_ref[rows, :]
            for gi in range(tn // HEAD_DIM):
                cols = slice(gi * HEAD_DIM, (gi + 1) * HEAD_DIM)
                a = acc[:, cols]
                ms = jnp.mean(a * a, axis=-1, keepdims=True)
                a = a * lax.rsqrt(ms + NORM_EPS) * cg_ref[:, cols]
                a = a * cos + pltpu.roll(a, HEAD_DIM // 2, axis=1) * sin
                o_ref[rows, cols] = a.astype(BF16)

    @pl.when(j == 0)
    def _():
        qk_tile(True)

    @pl.when((j > 0) & (j < n_qk_tiles))
    def _():
        qk_tile(False)

    @pl.when(j >= n_qk_tiles)
    def _():
        o_ref[...] = _bdot(h_scr[...], w_ref[...].astype(BF16)).astype(BF16)


def _qkv_proj(x, gain, mod, mod_row_fn, w, colgain, cos, sin, n_qk_cols, col_start=0):
    r, k = x.shape
    tm, tn = 1024, 1024
    n = w.shape[1] - col_start
    j0 = col_start // tn
    npos = cos.shape[0]
    kern = functools.partial(_qkv_kernel, tm=tm, tn=tn, n_qk_tiles=(n_qk_cols - col_start) // tn, chunk=32,
                             row_group=256)
    return pl.pallas_call(
        kern,
        grid=(r // tm, n // tn),
        in_specs=[
            pl.BlockSpec((tm, k), lambda i, j: (i, 0), pipeline_mode=pl.Buffered(1)),
            pl.BlockSpec((1, k), lambda i, j: (0, 0)),
            pl.BlockSpec((None, None, 1, k), lambda i, j: (mod_row_fn(i), 0, 0, 0)),
            pl.BlockSpec((None, None, 1, k), lambda i, j: (mod_row_fn(i), 1, 0, 0)),
            pl.BlockSpec((k, tn), lambda i, j: (0, j0 + j)),
            pl.BlockSpec((1, tn), lambda i, j: (0, j0 + j)),
            pl.BlockSpec((None, tm, HEAD_DIM), lambda i, j: (i % npos, 0, 0)),
            pl.BlockSpec((None, tm, HEAD_DIM), lambda i, j: (i % npos, 0, 0)),
        ],
        out_specs=pl.BlockSpec((tm, tn), lambda i, j: (i, j)),
        out_shape=jax.ShapeDtypeStruct((r, n), BF16),
        scratch_shapes=[pltpu.VMEM((tm, k), BF16)],
        compiler_params=_cparams(("arbitrary", "arbitrary")),
        name="qkv_proj",
    )(x, gain.reshape(1, k), mod, mod, w, colgain, cos, sin)


def _attend(q, kvs, row_block=16):
    tq = q.shape[0]
    ss = [_dot_nt(q, k) for k, _ in kvs]
    p_blocks = [[] for _ in kvs]
    l_blocks = []
    for r in range(tq // row_block):
        rows = slice(r * row_block, (r + 1) * row_block)
        sb = [s[rows, :] for s in ss]
        m = functools.reduce(jnp.maximum, [jnp.max(x, axis=-1, keepdims=True) for x in sb])
        pb = [jnp.exp2(x - m) for x in sb]
        l_blocks.append(functools.reduce(lambda a, b: a + b,
                                         [jnp.sum(p, axis=-1, keepdims=True) for p in pb]))
        for blocks, p in zip(p_blocks, pb):
            blocks.append(p.astype(BF16))
    l = jnp.concatenate(l_blocks, axis=0)
    o = functools.reduce(lambda a, b: a + b,
                         [_bdot(jnp.concatenate(blocks, axis=0), v) for blocks, (_, v) in zip(p_blocks, kvs)])
    return o / l


def _diff_lambda(lam_ref, lam_init):
    lp = lam_ref[...]
    s1 = jnp.sum(lp[0:1, :] * lp[1:2, :], axis=-1, keepdims=True)
    s2 = jnp.sum(lp[2:3, :] * lp[3:4, :], axis=-1, keepdims=True)
    return jnp.exp(s1) - jnp.exp(s2) + lam_init


def _diff_attn_kernel(*refs, lam_init, n_kv, sub_rows=256):
    lam_ref, subln_ref, q_ref = refs[:3]
    k_refs = refs[3:3 + n_kv]
    v_refs = refs[3 + n_kv:3 + 2 * n_kv]
    o_ref = refs[3 + 2 * n_kv]
    lam = _diff_lambda(lam_ref, lam_init)
    tq = q_ref.shape[0]
    sub = min(tq, sub_rows)
    for qs in range(tq // sub):
        rows = slice(qs * sub, (qs + 1) * sub)
        outs = []
        for t in range(2):
            cols = slice(t * HEAD_DIM, (t + 1) * HEAD_DIM)
            kvs = [(kr[:, cols], vr[...]) for kr, vr in zip(k_refs, v_refs)]
            outs.append(_attend(q_ref[rows, cols], kvs))
        o = outs[0] - lam * outs[1]
        ms = jnp.mean(o * o, axis=-1, keepdims=True)
        o = o * lax.rsqrt(ms + NORM_EPS) * subln_ref[...] * (1.0 - lam_init)
        o_ref[rows, :] = o.astype(BF16)


def _diff_attention(lam_params, subln, qkv_q, kv_sources, nb, nq, lam_init):
    heads = D_MODEL // (2 * HEAD_DIM)
    hw = 2 * HEAD_DIM
    tq = min(nq, 1024)
    nqb = nq // tq
    n_kv = len(kv_sources)
    in_specs = [
        pl.BlockSpec((4, HEAD_DIM), lambda b, h, i: (0, 0)),
        pl.BlockSpec((1, hw), lambda b, h, i: (0, 0)),
        pl.BlockSpec((tq, hw), lambda b, h, i: (b * nqb + i, h)),
    ]
    args = [lam_params, subln.reshape(1, hw), qkv_q]
    for arr, nk in kv_sources:
        in_specs.append(pl.BlockSpec((nk, hw), lambda b, h, i: (b, heads + h)))
        args.append(arr)
    for arr, nk in kv_sources:
        in_specs.append(pl.BlockSpec((nk, hw), lambda b, h, i: (b, 2 * heads + h)))
        args.append(arr)
    kern = functools.partial(_diff_attn_kernel, lam_init=lam_init, n_kv=n_kv)
    return pl.pallas_call(
        kern,
        grid=(nb, heads, nqb),
        in_specs=in_specs,
        out_specs=pl.BlockSpec((tq, hw), lambda b, h, i: (b * nqb + i, h)),
        out_shape=jax.ShapeDtypeStruct((nb * nq, D_MODEL), BF16),
        compiler_params=_cparams(("arbitrary", "arbitrary", "arbitrary")),
        name="diff_attn",
    )(*args)


def _gqa_attn_kernel(q_ref, kl_ref, kc_ref, vl_ref, vc_ref, o_ref, *, group, sub_rows=256):
    kvs = [(kl_ref[...], vl_ref[...]), (kc_ref[...], vc_ref[...])]
    tq = q_ref.shape[0]
    sub = min(tq, sub_rows)
    for qs in range(tq // sub):
        rows = slice(qs * sub, (qs + 1) * sub)
        for g in range(group):
            cols = slice(g * HEAD_DIM, (g + 1) * HEAD_DIM)
            o_ref[rows, cols] = _attend(q_ref[rows, cols], kvs).astype(BF16)


def _gqa_attention(qkv_lat, kv_ctx, nb, nq, nc):
    q_heads = D_MODEL // HEAD_DIM
    kv_heads = q_heads // 4
    group = q_heads // kv_heads
    gw = group * HEAD_DIM
    tq = 512
    nqb = nq // tq
    k_off = D_MODEL // HEAD_DIM
    v_off = k_off + kv_heads
    kern = functools.partial(_gqa_attn_kernel, group=group)
    return pl.pallas_call(
        kern,
        grid=(nb, kv_heads, nqb),
        in_specs=[
            pl.BlockSpec((tq, gw), lambda b, h, i: (b * nqb + i, h)),
            pl.BlockSpec((nq, HEAD_DIM), lambda b, h, i: (b, k_off + h)),
            pl.BlockSpec((nc, HEAD_DIM), lambda b, h, i: (b, h)),
            pl.BlockSpec((nq, HEAD_DIM), lambda b, h, i: (b, v_off + h)),
            pl.BlockSpec((nc, HEAD_DIM), lambda b, h, i: (b, kv_heads + h)),
        ],
        out_specs=pl.BlockSpec((tq, gw), lambda b, h, i: (b * nqb + i, h)),
        out_shape=jax.ShapeDtypeStruct((nb * nq, D_MODEL), BF16),
        compiler_params=_cparams(("arbitrary", "arbitrary", "arbitrary")),
        name="gqa_attn",
    )(qkv_lat, qkv_lat, kv_ctx, qkv_lat, kv_ctx)


def _proj_res_kernel(o_ref, w_ref, res_ref, gate_ref, out_ref):
    acc = _bdot(o_ref[...], w_ref[...].astype(BF16))
    out_ref[...] = res_ref[...] + gate_ref[...] * acc


def _proj_residual(o, w, res, mod, mod_row_fn, gate_idx):
    r, k = o.shape
    n = w.shape[1]
    tm, tn = 2048, 256
    return pl.pallas_call(
        _proj_res_kernel,
        grid=(r // tm, n // tn),
        in_specs=[
            pl.BlockSpec((tm, k), lambda i, j: (i, 0)),
            pl.BlockSpec((k, tn), lambda i, j: (0, j)),
            pl.BlockSpec((tm, tn), lambda i, j: (i, j)),
            pl.BlockSpec((None, None, 1, tn), lambda i, j: (mod_row_fn(i), gate_idx, 0, j)),
        ],
        out_specs=pl.BlockSpec((tm, tn), lambda i, j: (i, j)),
        out_shape=jax.ShapeDtypeStruct((r, n), F32),
        compiler_params=_cparams(("arbitrary", "arbitrary")),
        name="out_proj",
    )(o, w, res, mod)


def _route_kernel(x_ref, g_ref, sh_ref, sc_ref, rt_ref, tri_ref, h_ref, gate_ref, slot_ref, lg_scr,
                  *, tc, n, cap):
    c = pl.program_id(1)
    h = _norm_mod(x_ref[...], g_ref[...], sh_ref[...], sc_ref[...])
    h_ref[...] = h.astype(BF16)
    h_hi, h_lo = _split_bf16(h)
    r_hi, r_lo = _split_bf16(rt_ref[...])
    lg_scr[c] = _dot_nt(r_hi, h_hi) + _dot_nt(r_hi, h_lo) + _dot_nt(r_lo, h_hi)

    @pl.when(c == n // tc - 1)
    def _():
        lg = jnp.concatenate([lg_scr[i] for i in range(n // tc)], axis=1)
        mx = jnp.max(lg, axis=0, keepdims=True)
        ex = jnp.exp(lg - mx)
        aff = ex / jnp.sum(ex, axis=0, keepdims=True)
        bits = pltpu.bitcast(aff, jnp.int32)
        thr = jnp.zeros((N_EXPERTS, 1), jnp.int32)
        for bit in range(30, -1, -1):
            cand = thr | jnp.int32(1 << bit)
            cnt = jnp.sum(jnp.where(bits >= cand, 1.0, 0.0), axis=1, keepdims=True)
            thr = jnp.where(cnt >= cap, cand, thr)
        gt = bits > thr
        eq = bits == thr
        need = cap - jnp.sum(jnp.where(gt, 1.0, 0.0), axis=1, keepdims=True)
        tri = tri_ref[...]
        eq_rank = _bdot(jnp.where(eq, 1.0, 0.0).astype(BF16), tri)
        sel = gt | (eq & (eq_rank < need))
        slot = _bdot(jnp.where(sel, 1.0, 0.0).astype(BF16), tri)
        gate_ref[...] = jnp.where(sel, aff, 0.0)
        slot_ref[...] = jnp.where(sel, slot.astype(jnp.int32), -1)


def _route(x, gain, mod, mod_row_fn, router_t, tri, nb, n, cap):
    k = x.shape[1]
    tc = 256
    nch = n // tc
    kern = functools.partial(_route_kernel, tc=tc, n=n, cap=cap)
    return pl.pallas_call(
        kern,
        grid=(nb, nch),
        in_specs=[
            pl.BlockSpec((tc, k), lambda b, c: (b * nch + c, 0)),
            pl.BlockSpec((1, k), lambda b, c: (0, 0)),
            pl.BlockSpec((None, None, 1, k), lambda b, c: (mod_row_fn(b), 3, 0, 0)),
            pl.BlockSpec((None, None, 1, k), lambda b, c: (mod_row_fn(b), 4, 0, 0)),
            pl.BlockSpec((N_EXPERTS, k), lambda b, c: (0, 0)),
            pl.BlockSpec((n, n), lambda b, c: (0, 0)),
        ],
        out_specs=[
            pl.BlockSpec((tc, k), lambda b, c: (b * nch + c, 0)),
            pl.BlockSpec((None, N_EXPERTS, n), lambda b, c: (b, 0, 0)),
            pl.BlockSpec((None, N_EXPERTS, n), lambda b, c: (b, 0, 0)),
        ],
        out_shape=[
            jax.ShapeDtypeStruct((nb * n, k), BF16),
            jax.ShapeDtypeStruct((nb, N_EXPERTS, n), F32),
            jax.ShapeDtypeStruct((nb, N_EXPERTS, n), jnp.int32),
        ],
        scratch_shapes=[pltpu.VMEM((nch, N_EXPERTS, tc), F32)],
        compiler_params=_cparams(("arbitrary", "arbitrary")),
        name="route",
    )(x, gain.reshape(1, k), mod, mod, router_t, tri)


def _gather_kernel(slot_ref, gate_ref, h_ref, xs_ref, gs_ref, *, cap, n):
    onehot = lax.broadcasted_iota(jnp.int32, (cap, n), 0) == slot_ref[...]
    xs_ref[...] = _bdot(jnp.where(onehot, 1.0, 0.0).astype(BF16), h_ref[...]).astype(BF16)
    gs_ref[...] = jnp.sum(jnp.where(onehot, gate_ref[...], 0.0), axis=1, keepdims=True)


def _gather(slot_t, gate_t, h, nb, n, cap):
    k = h.shape[1]
    kern = functools.partial(_gather_kernel, cap=cap, n=n)
    row_spec = pl.BlockSpec((None, None, 1, n), lambda b, e: (b, e, 0, 0))
    return pl.pallas_call(
        kern,
        grid=(nb, N_EXPERTS),
        in_specs=[row_spec, row_spec, pl.BlockSpec((n, k), lambda b, e: (b, 0))],
        out_specs=[pl.BlockSpec((None, cap, k), lambda b, e: (e, b, 0)),
                   pl.BlockSpec((None, cap, 1), lambda b, e: (e, b, 0))],
        out_shape=[jax.ShapeDtypeStruct((N_EXPERTS, nb * cap, k), BF16),
                   jax.ShapeDtypeStruct((N_EXPERTS, nb * cap, 1), F32)],
        compiler_params=_cparams(("arbitrary", "arbitrary")),
        name="gather",
    )(slot_t.reshape(nb, N_EXPERTS, 1, n), gate_t.reshape(nb, N_EXPERTS, 1, n), h)


def _ffn_kernel(xs_ref, gs_ref, wg_ref, wu_ref, wd_ref, y_ref, hid_scr, *, nf, tf, row_group):
    s = pl.program_id(2)
    tm = xs_ref.shape[0]

    @pl.when(s < nf)
    def _():
        wg = wg_ref[...].astype(BF16)
        wu = wu_ref[...].astype(BF16)
        for r in range(tm // row_group):
            rows = slice(r * row_group, (r + 1) * row_group)
            xs = xs_ref[rows, :]
            a = _bdot(xs, wg)
            u = _bdot(xs, wu)
            hid_scr[s, rows, :] = (_silu(a) * u * gs_ref[rows, :]).astype(BF16)

    @pl.when(s >= nf)
    def _():
        wds = [wd_ref[f * tf:(f + 1) * tf, :].astype(BF16) for f in range(nf)]
        for r in range(tm // row_group):
            rows = slice(r * row_group, (r + 1) * row_group)
            acc = _bdot(hid_scr[0, rows, :], wds[0])
            for f in range(1, nf):
                acc = acc + _bdot(hid_scr[f, rows, :], wds[f])
            y_ref[rows, :] = acc.astype(BF16)


def _expert_ffn(xs, gs, wg, wu, wd):
    e, m, k = xs.shape
    ff = wg.shape[2]
    tm = m
    tf, td = 256, 512
    nf, nd = ff // tf, k // td
    kern = functools.partial(_ffn_kernel, nf=nf, tf=tf, row_group=min(m, 1024))
    return pl.pallas_call(
        kern,
        grid=(e, m // tm, nf + nd),
        in_specs=[
            pl.BlockSpec((None, tm, k), lambda ei, mi, s: (ei, mi, 0), pipeline_mode=pl.Buffered(1)),
            pl.BlockSpec((None, tm, 1), lambda ei, mi, s: (ei, mi, 0)),
            pl.BlockSpec((None, k, tf), lambda ei, mi, s: (ei, 0, jnp.minimum(s, nf - 1))),
            pl.BlockSpec((None, k, tf), lambda ei, mi, s: (ei, 0, jnp.minimum(s, nf - 1))),
            pl.BlockSpec((None, ff, td), lambda ei, mi, s: (ei, 0, jnp.maximum(s - nf, 0))),
        ],
        out_specs=pl.BlockSpec((None, tm, td), lambda ei, mi, s: (ei, mi, jnp.maximum(s - nf, 0))),
        out_shape=jax.ShapeDtypeStruct((e, m, k), BF16),
        scratch_shapes=[pltpu.VMEM((nf, tm, tf), BF16)],
        compiler_params=_cparams(("arbitrary", "arbitrary", "arbitrary")),
        name="expert_ffn",
    )(xs, gs, wg, wu, wd)


def _combine_kernel(slot_ref, y_ref, x_ref, g2_ref, o_ref, acc_scr, *, tt, cap, cap_pad, eg):
    e = pl.program_id(2)

    @pl.when(e == 0)
    def _():
        acc_scr[...] = jnp.zeros_like(acc_scr)

    slot_iota = lax.broadcasted_iota(jnp.int32, (cap_pad, tt), 0)
    onehot_t = jnp.concatenate([jnp.where(slot_iota == slot_ref[i], 1.0, 0.0) for i in range(eg)], axis=0)
    onehot = onehot_t.T[:, 0:eg * cap].astype(BF16)
    acc_scr[...] += _bdot(onehot, y_ref[...].reshape(eg * cap, y_ref.shape[2]))

    @pl.when(e == N_EXPERTS // eg - 1)
    def _():
        o_ref[...] = x_ref[...] + g2_ref[...] * acc_scr[...]


def _combine(slot_t, y, x, mod, mod_row_fn, nb, n, cap):
    k = x.shape[1]
    tt = min(n, 512)
    ntb = n // tt
    cap_pad = max(cap, 128)
    eg = 2 if cap_pad == cap else 1
    kern = functools.partial(_combine_kernel, tt=tt, cap=cap, cap_pad=cap_pad, eg=eg)
    return pl.pallas_call(
        kern,
        grid=(nb, ntb, N_EXPERTS // eg),
        in_specs=[
            pl.BlockSpec((None, eg, 1, tt), lambda b, t, e: (b, e, 0, t)),
            pl.BlockSpec((eg, cap, k), lambda b, t, e: (e, b, 0)),
            pl.BlockSpec((tt, k), lambda b, t, e: (b * ntb + t, 0)),
            pl.BlockSpec((None, None, 1, k), lambda b, t, e: (mod_row_fn(b), 5, 0, 0)),
        ],
        out_specs=pl.BlockSpec((tt, k), lambda b, t, e: (b * ntb + t, 0)),
        out_shape=jax.ShapeDtypeStruct((nb * n, k), F32),
        scratch_shapes=[pltpu.VMEM((tt, k), F32)],
        compiler_params=_cparams(("arbitrary", "arbitrary", "arbitrary")),
        name="combine",
    )(slot_t.reshape(nb, N_EXPERTS, 1, n), y, x, mod)


def _ec_moe_residual(x, gain, mod, mod_row_fn, router, wg, wu, wd, nb, n):
    cap = EC_CAPACITY_FACTOR * n // N_EXPERTS
    idx = jnp.arange(n, dtype=jnp.int32)
    tri = (idx[:, None] < idx[None, :]).astype(BF16)
    h, gate_t, slot_t = _route(x, gain, mod, mod_row_fn, router.T, tri, nb, n, cap)
    xs, gs = _gather(slot_t, gate_t, h, nb, n, cap)
    y = _expert_ffn(xs, gs, wg, wu, wd)
    return _combine(slot_t, y, x, mod, mod_row_fn, nb, n, cap)


def _rope_tables(n, tm):
    rows = n // GRID_W
    row = jnp.repeat(jnp.arange(rows, dtype=F32), GRID_W)
    col = jnp.tile(jnp.arange(GRID_W, dtype=F32), rows)
    axis_dim = HEAD_DIM // 2
    inv_freq = ROPE_THETA ** (-jnp.arange(0, axis_dim, 2, dtype=F32) / axis_dim)
    ang = jnp.concatenate([row[:, None] * inv_freq, col[:, None] * inv_freq], axis=-1)
    cos, sin = jnp.cos(ang), jnp.sin(ang)
    cos_full = jnp.concatenate([cos, cos], axis=-1).reshape(n // tm, tm, HEAD_DIM)
    sin_signed = jnp.concatenate([-sin, sin], axis=-1).reshape(n // tm, tm, HEAD_DIM)
    return cos_full, sin_signed


def kernel(x, c, ctx, c_ctx,
           mod_w_0, mod_b_0, norm1_0, norm2_0, wqkv_0, wo_0, qnorm_0, knorm_0,
           lam_q1_0, lam_k1_0, lam_q2_0, lam_k2_0, subln_0, router_0, we_gate_0, we_up_0, we_down_0,
           mod_w_1, mod_b_1, norm1_1, norm2_1, wqkv_1, wo_1, qnorm_1, knorm_1,
           router_1, we_gate_1, we_up_1, we_down_1):
    nb, nq, d = x.shape
    nc = ctx.shape[1]
    tm = 1024
    x_lat = x.reshape(nb * nq, d)
    x_ctx = ctx.reshape(nb * nc, d)
    cond = jnp.zeros((MOD_ROWS, d), F32).at[:nb].set(c).at[nb].set(c_ctx)
    lat_row = lambda i: i // (nq // tm)
    ctx_row = lambda i: nb
    lat_b = lambda b: b
    cos, sin = _rope_tables(nq, tm)
    cos_id = jnp.ones((1, tm, HEAD_DIM), F32)
    sin_id = jnp.zeros((1, tm, HEAD_DIM), F32)
    q_scale = math.log2(math.e) / math.sqrt(HEAD_DIM)

    mod0 = _adaln(cond, mod_w_0, mod_b_0)
    n_heads2 = d // HEAD_DIM
    colgain0 = jnp.concatenate([jnp.tile(qnorm_0 * q_scale, n_heads2), jnp.tile(knorm_0, n_heads2),
                                jnp.ones((d,), F32)]).reshape(1, 3 * d)
    wqkv_0b = wqkv_0.astype(BF16)
    qkv_lat = _qkv_proj(x_lat, norm1_0, mod0, lat_row, wqkv_0b, colgain0, cos, sin, 2 * d)
    qkv_ctx = _qkv_proj(x_ctx, norm1_0, mod0, ctx_row, wqkv_0b, colgain0, cos_id, sin_id, 2 * d)
    lam_init = 0.8 - 0.6 * math.exp(-0.3 * 0)
    lam_params = jnp.stack([lam_q1_0, lam_k1_0, lam_q2_0, lam_k2_0])
    o_lat = _diff_attention(lam_params, subln_0, qkv_lat, [(qkv_lat, nq), (qkv_ctx, nc)], nb, nq, lam_init)
    o_ctx = _diff_attention(lam_params, subln_0, qkv_ctx, [(qkv_ctx, nc)], nb, nc, lam_init)
    wo_0b = wo_0.astype(BF16)
    x_lat = _proj_residual(o_lat, wo_0b, x_lat, mod0, lat_b, 2)
    x_ctx = _proj_residual(o_ctx, wo_0b, x_ctx, mod0, ctx_row, 2)
    x_lat = _ec_moe_residual(x_lat, norm2_0, mod0, lat_b, router_0, we_gate_0, we_up_0, we_down_0, nb, nq)
    x_ctx = _ec_moe_residual(x_ctx, norm2_0, mod0, ctx_row, router_0, we_gate_0, we_up_0, we_down_0, nb, nc)

    mod1 = _adaln(cond, mod_w_1, mod_b_1)
    kv_dim = (wqkv_1.shape[1] - d) // 2
    colgain1 = jnp.concatenate([jnp.tile(qnorm_1 * q_scale, d // HEAD_DIM),
                                jnp.tile(knorm_1, kv_dim // HEAD_DIM),
                                jnp.ones((kv_dim,), F32)]).reshape(1, d + 2 * kv_dim)
    wqkv_1b = wqkv_1.astype(BF16)
    qkv_lat = _qkv_proj(x_lat, norm1_1, mod1, lat_row, wqkv_1b, colgain1, cos, sin, d + kv_dim)
    kv_ctx = _qkv_proj(x_ctx, norm1_1, mod1, ctx_row, wqkv_1b, colgain1, cos_id, sin_id, d + kv_dim, col_start=d)
    o_lat = _gqa_attention(qkv_lat, kv_ctx, nb, nq, nc)
    x_lat = _proj_residual(o_lat, wo_1.astype(BF16), x_lat, mod1, lat_b, 2)
    x_lat = _ec_moe_residual(x_lat, norm2_1, mod1, lat_b, router_1, we_gate_1, we_up_1, we_down_1, nb, nq)
    return x_lat.reshape(nb, nq, d)
```

```python
import functools
import math

import jax
import jax.numpy as jnp
from jax import lax
from jax.experimental import pallas as pl
from jax.experimental.pallas import tpu as pltpu

F32 = jnp.float32
BF16 = jnp.bfloat16

D_MODEL = 4096
GRID_W = 64
HEAD_DIM = 128
N_EXPERTS = 16
EC_CAPACITY_FACTOR = 2
EXPERT_FF = D_MODEL // 4
ROPE_THETA = 10000.0
NORM_EPS = 1e-6
N_MOD = 6
MOD_ROWS = 16
VMEM_LIMIT = 58 * 1024 * 1024


def _cparams(sem):
    return pltpu.CompilerParams(dimension_semantics=sem, vmem_limit_bytes=VMEM_LIMIT)


def _bdot(a, b):
    return jnp.dot(a, b, preferred_element_type=F32)


def _dot_nt(a, b):
    return lax.dot_general(a, b, (((1,), (1,)), ((), ())), preferred_element_type=F32)


def _silu(x):
    return x * (1.0 / (1.0 + jnp.exp(-x)))


def _split_bf16(x):
    hi = x.astype(BF16)
    lo = (x - hi.astype(F32)).astype(BF16)
    return hi, lo


def _adaln_kernel(c_ref, w_ref, b_ref, o_ref):
    s = _silu(c_ref[...]).astype(BF16)
    o_ref[...] = _bdot(s, w_ref[...].astype(BF16)) + b_ref[...]


def _adaln(cond, w, b):
    n = w.shape[1]
    tn = 512
    out = pl.pallas_call(
        _adaln_kernel,
        grid=(n // tn,),
        in_specs=[
            pl.BlockSpec((MOD_ROWS, D_MODEL), lambda j: (0, 0)),
            pl.BlockSpec((D_MODEL, tn), lambda j: (0, j)),
            pl.BlockSpec((1, tn), lambda j: (0, j)),
        ],
        out_specs=pl.BlockSpec((MOD_ROWS, tn), lambda j: (0, j)),
        out_shape=jax.ShapeDtypeStruct((MOD_ROWS, n), F32),
        compiler_params=_cparams(("arbitrary",)),
        name="adaln",
    )(cond, w, b.reshape(1, n))
    return out.reshape(MOD_ROWS, N_MOD, 1, D_MODEL)


def _norm_mod(xv, g, shift, scale):
    ms = jnp.mean(xv * xv, axis=-1, keepdims=True)
    y = xv * lax.rsqrt(ms + NORM_EPS) * g
    return y * (1.0 + scale) + shift


def _qkv_kernel(x_ref, g_ref, sh_ref, sc_ref, w_ref, cg_ref, cos_ref, sin_ref, o_ref, h_scr,
                *, tm, tn, n_qk_tiles, chunk, row_group):
    j = pl.program_id(1)

    def qk_tile(make_h):
        w = w_ref[...].astype(BF16)
        for r in range(tm // row_group):
            rows = slice(r * row_group, (r + 1) * row_group)
            if make_h:
                for c in range(row_group // chunk):
                    rr = slice(r * row_group + c * chunk, r * row_group + (c + 1) * chunk)
                    h = _norm_mod(x_ref[rr, :], g_ref[...], sh_ref[...], sc_ref[...])
                    h_scr[rr, :] = h.astype(BF16)
            acc = _bdot(h_scr[rows, :], w)
            cos = cos_ref[rows, :]
            sin = sin_ref[rows, :]
            for gi in range(tn // HEAD_DIM):
                cols = slice(gi * HEAD_DIM, (gi + 1) * HEAD_DIM)
                a = acc[:, cols]
                ms = jnp.mean(a * a, axis=-1, keepdims=True)
                a = a * lax.rsqrt(ms + NORM_EPS) * cg_ref[:, cols]
                a = a * cos + pltpu.roll(a, HEAD_DIM // 2, axis=1) * sin
                o_ref[rows, cols] = a.astype(BF16)

    @pl.when(j == 0)
    def _():
        qk_tile(True)

    @pl.when((j > 0) & (j < n_qk_tiles))
    def _():
        qk_tile(False)

    @pl.when(j >= n_qk_tiles)
    def _():
        o_ref[...] = _bdot(h_scr[...], w_ref[...].astype(BF16)).astype(BF16)


def _qkv_proj(x, gain, mod, mod_row_fn, w, colgain, cos, sin, n_qk_cols, col_start=0):
    r, k = x.shape
    tm, tn = 1024, 512
    n = w.shape[1] - col_start
    j0 = col_start // tn
    npos = cos.shape[0]
    kern = functools.partial(_qkv_kernel, tm=tm, tn=tn, n_qk_tiles=(n_qk_cols - col_start) // tn, chunk=32,
                             row_group=256)
    return pl.pallas_call(
        kern,
        grid=(r // tm, n // tn),
        in_specs=[
            pl.BlockSpec((tm, k), lambda i, j: (i, 0), pipeline_mode=pl.Buffered(1)),
            pl.BlockSpec((1, k), lambda i, j: (0, 0)),
            pl.BlockSpec((None, None, 1, k), lambda i, j: (mod_row_fn(i), 0, 0, 0)),
            pl.BlockSpec((None, None, 1, k), lambda i, j: (mod_row_fn(i), 1, 0, 0)),
            pl.BlockSpec((k, tn), lambda i, j: (0, j0 + j)),
            pl.BlockSpec((1, tn), lambda i, j: (0, j0 + j)),
            pl.BlockSpec((None, tm, HEAD_DIM), lambda i, j: (i % npos, 0, 0)),
            pl.BlockSpec((None, tm, HEAD_DIM), lambda i, j: (i % npos, 0, 0)),
        ],
        out_specs=pl.BlockSpec((tm, tn), lambda i, j: (i, j)),
        out_shape=jax.ShapeDtypeStruct((r, n), BF16),
        scratch_shapes=[pltpu.VMEM((tm, k), BF16)],
        compiler_params=_cparams(("arbitrary", "arbitrary")),
        name="qkv_proj",
    )(x, gain.reshape(1, k), mod, mod, w, colgain, cos, sin)


def _attend(q, kvs, row_block=16, ones_cols=0):
    tq = q.shape[0]
    ss = [_dot_nt(q, k) for k, _ in kvs]
    p_blocks = [[] for _ in kvs]
    l_blocks = []
    for r in range(tq // row_block):
        rows = slice(r * row_block, (r + 1) * row_block)
        sb = [s[rows, :] for s in ss]
        m = functools.reduce(jnp.maximum, [jnp.max(x, axis=-1, keepdims=True) for x in sb])
        pb = [jnp.exp2(x - m) for x in sb]
        if not ones_cols:
            l_blocks.append(functools.reduce(lambda a, b: a + b,
                                             [jnp.sum(p, axis=-1, keepdims=True) for p in pb]))
        for blocks, p in zip(p_blocks, pb):
            blocks.append(p.astype(BF16))
    o = functools.reduce(lambda a, b: a + b,
                         [_bdot(jnp.concatenate(blocks, axis=0), v) for blocks, (_, v) in zip(p_blocks, kvs)])
    if ones_cols:
        dv = o.shape[1] - ones_cols
        return o[:, 0:dv] / o[:, dv:dv + 1]
    return o / jnp.concatenate(l_blocks, axis=0)


def _diff_lambda(lam_ref, lam_init):
    lp = lam_ref[...]
    s1 = jnp.sum(lp[0:1, :] * lp[1:2, :], axis=-1, keepdims=True)
    s2 = jnp.sum(lp[2:3, :] * lp[3:4, :], axis=-1, keepdims=True)
    return jnp.exp(s1) - jnp.exp(s2) + lam_init


def _diff_attn_kernel(*refs, lam_init, n_kv, sub_rows=256):
    lam_ref, subln_ref, q_ref = refs[:3]
    k_refs = refs[3:3 + n_kv]
    v_refs = refs[3 + n_kv:3 + 2 * n_kv]
    o_ref = refs[3 + 2 * n_kv]
    lam = _diff_lambda(lam_ref, lam_init)
    tq = q_ref.shape[0]
    sub = min(tq, sub_rows)
    for qs in range(tq // sub):
        rows = slice(qs * sub, (qs + 1) * sub)
        outs = []
        for t in range(2):
            cols = slice(t * HEAD_DIM, (t + 1) * HEAD_DIM)
            kvs = [(kr[:, cols], vr[...]) for kr, vr in zip(k_refs, v_refs)]
            outs.append(_attend(q_ref[rows, cols], kvs))
        o = outs[0] - lam * outs[1]
        ms = jnp.mean(o * o, axis=-1, keepdims=True)
        o = o * lax.rsqrt(ms + NORM_EPS) * subln_ref[...] * (1.0 - lam_init)
        o_ref[rows, :] = o.astype(BF16)


def _diff_attention(lam_params, subln, qkv_q, kv_sources, nb, nq, lam_init):
    heads = D_MODEL // (2 * HEAD_DIM)
    hw = 2 * HEAD_DIM
    tq = min(nq, 1024)
    nqb = nq // tq
    n_kv = len(kv_sources)
    in_specs = [
        pl.BlockSpec((4, HEAD_DIM), lambda b, h, i: (0, 0)),
        pl.BlockSpec((1, hw), lambda b, h, i: (0, 0)),
        pl.BlockSpec((tq, hw), lambda b, h, i: (b * nqb + i, h)),
    ]
    args = [lam_params, subln.reshape(1, hw), qkv_q]
    for arr, nk in kv_sources:
        in_specs.append(pl.BlockSpec((nk, hw), lambda b, h, i: (b, heads + h)))
        args.append(arr)
    for arr, nk in kv_sources:
        in_specs.append(pl.BlockSpec((nk, hw), lambda b, h, i: (b, 2 * heads + h)))
        args.append(arr)
    kern = functools.partial(_diff_attn_kernel, lam_init=lam_init, n_kv=n_kv)
    return pl.pallas_call(
        kern,
        grid=(nb, heads, nqb),
        in_specs=in_specs,
        out_specs=pl.BlockSpec((tq, hw), lambda b, h, i: (b * nqb + i, h)),
        out_shape=jax.ShapeDtypeStruct((nb * nq, D_MODEL), BF16),
        compiler_params=_cparams(("arbitrary", "arbitrary", "arbitrary")),
        name="diff_attn",
    )(*args)


def _gqa_attn_kernel(q_ref, kl_ref, kc_ref, vl_ref, vc_ref, o_ref, *, group, sub_rows=256):
    def with_ones(v):
        return jnp.concatenate([v, jnp.ones(v.shape, v.dtype)], axis=1)
    kvs = [(kl_ref[...], with_ones(vl_ref[...])), (kc_ref[...], with_ones(vc_ref[...]))]
    tq = q_ref.shape[0]
    sub = min(tq, sub_rows)
    for qs in range(tq // sub):
        rows = slice(qs * sub, (qs + 1) * sub)
        for g in range(group):
            cols = slice(g * HEAD_DIM, (g + 1) * HEAD_DIM)
            o_ref[rows, cols] = _attend(q_ref[rows, cols], kvs, ones_cols=HEAD_DIM).astype(BF16)


def _gqa_attention(qkv_lat, kv_ctx, nb, nq, nc):
    q_heads = D_MODEL // HEAD_DIM
    kv_heads = q_heads // 4
    group = q_heads // kv_heads
    gw = group * HEAD_DIM
    tq = 512
    nqb = nq // tq
    k_off = D_MODEL // HEAD_DIM
    v_off = k_off + kv_heads
    kern = functools.partial(_gqa_attn_kernel, group=group)
    return pl.pallas_call(
        kern,
        grid=(nb, kv_heads, nqb),
        in_specs=[
            pl.BlockSpec((tq, gw), lambda b, h, i: (b * nqb + i, h)),
            pl.BlockSpec((nq, HEAD_DIM), lambda b, h, i: (b, k_off + h)),
            pl.BlockSpec((nc, HEAD_DIM), lambda b, h, i: (b, h)),
            pl.BlockSpec((nq, HEAD_DIM), lambda b, h, i: (b, v_off + h)),
            pl.BlockSpec((nc, HEAD_DIM), lambda b, h, i: (b, kv_heads + h)),
        ],
        out_specs=pl.BlockSpec((tq, gw), lambda b, h, i: (b * nqb + i, h)),
        out_shape=jax.ShapeDtypeStruct((nb * nq, D_MODEL), BF16),
        compiler_params=_cparams(("arbitrary", "arbitrary", "arbitrary")),
        name="gqa_attn",
    )(qkv_lat, qkv_lat, kv_ctx, qkv_lat, kv_ctx)


def _proj_res_kernel(o_ref, w_ref, res_ref, gate_ref, out_ref):
    acc = _bdot(o_ref[...], w_ref[...].astype(BF16))
    out_ref[...] = res_ref[...] + gate_ref[...] * acc


def _proj_residual(o, w, res, mod, mod_row_fn, gate_idx):
    r, k = o.shape
    n = w.shape[1]
    tm, tn = 1024, 512
    return pl.pallas_call(
        _proj_res_kernel,
        grid=(r // tm, n // tn),
        in_specs=[
            pl.BlockSpec((tm, k), lambda i, j: (i, 0)),
            pl.BlockSpec((k, tn), lambda i, j: (0, j)),
            pl.BlockSpec((tm, tn), lambda i, j: (i, j)),
            pl.BlockSpec((None, None, 1, tn), lambda i, j: (mod_row_fn(i), gate_idx, 0, j)),
        ],
        out_specs=pl.BlockSpec((tm, tn), lambda i, j: (i, j)),
        out_shape=jax.ShapeDtypeStruct((r, n), F32),
        compiler_params=_cparams(("arbitrary", "arbitrary")),
        name="out_proj",
    )(o, w, res, mod)


def _route_kernel(x_ref, g_ref, sh_ref, sc_ref, rt_ref, tri_ref, h_ref, gate_ref, slot_ref, lg_scr,
                  *, tc, n, cap):
    c = pl.program_id(1)
    h = _norm_mod(x_ref[...], g_ref[...], sh_ref[...], sc_ref[...])
    h_ref[...] = h.astype(BF16)
    h_hi, h_lo = _split_bf16(h)
    r_hi, r_lo = _split_bf16(rt_ref[...])
    lg_scr[c] = _dot_nt(r_hi, h_hi) + _dot_nt(r_hi, h_lo) + _dot_nt(r_lo, h_hi)

    @pl.when(c == n // tc - 1)
    def _():
        lg = jnp.concatenate([lg_scr[i] for i in range(n // tc)], axis=1)
        mx = jnp.max(lg, axis=0, keepdims=True)
        ex = jnp.exp(lg - mx)
        aff = ex / jnp.sum(ex, axis=0, keepdims=True)
        bits = pltpu.bitcast(aff, jnp.int32)
        thr = jnp.zeros((N_EXPERTS, 1), jnp.int32)
        for bit in range(30, -1, -1):
            cand = thr | jnp.int32(1 << bit)
            cnt = jnp.sum(jnp.where(bits >= cand, 1.0, 0.0), axis=1, keepdims=True)
            thr = jnp.where(cnt >= cap, cand, thr)
        gt = bits > thr
        eq = bits == thr
        need = cap - jnp.sum(jnp.where(gt, 1.0, 0.0), axis=1, keepdims=True)
        tri = tri_ref[...]
        eq_rank = _bdot(jnp.where(eq, 1.0, 0.0).astype(BF16), tri)
        sel = gt | (eq & (eq_rank < need))
        slot = _bdot(jnp.where(sel, 1.0, 0.0).astype(BF16), tri)
        gate_ref[...] = jnp.where(sel, aff, 0.0)
        slot_ref[...] = jnp.where(sel, slot.astype(jnp.int32), -1)


def _route(x, gain, mod, mod_row_fn, router_t, tri, nb, n, cap):
    k = x.shape[1]
    tc = 256
    nch = n // tc
    kern = functools.partial(_route_kernel, tc=tc, n=n, cap=cap)
    return pl.pallas_call(
        kern,
        grid=(nb, nch),
        in_specs=[
            pl.BlockSpec((tc, k), lambda b, c: (b * nch + c, 0)),
            pl.BlockSpec((1, k), lambda b, c: (0, 0)),
            pl.BlockSpec((None, None, 1, k), lambda b, c: (mod_row_fn(b), 3, 0, 0)),
            pl.BlockSpec((None, None, 1, k), lambda b, c: (mod_row_fn(b), 4, 0, 0)),
            pl.BlockSpec((N_EXPERTS, k), lambda b, c: (0, 0)),
            pl.BlockSpec((n, n), lambda b, c: (0, 0)),
        ],
        out_specs=[
            pl.BlockSpec((tc, k), lambda b, c: (b * nch + c, 0)),
            pl.BlockSpec((None, N_EXPERTS, n), lambda b, c: (b, 0, 0)),
            pl.BlockSpec((None, N_EXPERTS, n), lambda b, c: (b, 0, 0)),
        ],
        out_shape=[
            jax.ShapeDtypeStruct((nb * n, k), BF16),
            jax.ShapeDtypeStruct((nb, N_EXPERTS, n), F32),
            jax.ShapeDtypeStruct((nb, N_EXPERTS, n), jnp.int32),
        ],
        scratch_shapes=[pltpu.VMEM((nch, N_EXPERTS, tc), F32)],
        compiler_params=_cparams(("arbitrary", "arbitrary")),
        name="route",
    )(x, gain.reshape(1, k), mod, mod, router_t, tri)


def _gather_kernel(slot_ref, gate_ref, h_ref, xs_ref, gs_ref, *, cap, n):
    onehot = lax.broadcasted_iota(jnp.int32, (cap, n), 0) == slot_ref[...]
    xs_ref[...] = _bdot(jnp.where(onehot, 1.0, 0.0).astype(BF16), h_ref[...]).astype(BF16)
    gs_ref[...] = jnp.sum(jnp.where(onehot, gate_ref[...], 0.0), axis=1, keepdims=True)


def _gather(slot_t, gate_t, h, nb, n, cap):
    k = h.shape[1]
    kern = functools.partial(_gather_kernel, cap=cap, n=n)
    row_spec = pl.BlockSpec((None, None, 1, n), lambda b, e: (b, e, 0, 0))
    return pl.pallas_call(
        kern,
        grid=(nb, N_EXPERTS),
        in_specs=[row_spec, row_spec, pl.BlockSpec((n, k), lambda b, e: (b, 0))],
        out_specs=[pl.BlockSpec((None, cap, k), lambda b, e: (e, b, 0)),
                   pl.BlockSpec((None, cap, 1), lambda b, e: (e, b, 0))],
        out_shape=[jax.ShapeDtypeStruct((N_EXPERTS, nb * cap, k), BF16),
                   jax.ShapeDtypeStruct((N_EXPERTS, nb * cap, 1), F32)],
        compiler_params=_cparams(("arbitrary", "arbitrary")),
        name="gather",
    )(slot_t.reshape(nb, N_EXPERTS, 1, n), gate_t.reshape(nb, N_EXPERTS, 1, n), h)


def _ffn_kernel(*refs, nf, tf, groups, n_parts):
    xs_refs = refs[:n_parts]
    gs_refs = refs[n_parts:2 * n_parts]
    wg_ref, wu_ref, wd_ref = refs[2 * n_parts:2 * n_parts + 3]
    y_refs = refs[2 * n_parts + 3:3 * n_parts + 3]
    hid_scr = refs[3 * n_parts + 3]
    s = pl.program_id(1)

    @pl.when(s < nf)
    def _():
        wg = wg_ref[...].astype(BF16)
        wu = wu_ref[...].astype(BF16)
        for part, r0, nr, h0 in groups:
            xs = xs_refs[part][r0:r0 + nr, :]
            a = _bdot(xs, wg)
            u = _bdot(xs, wu)
            hid_scr[s, h0:h0 + nr, :] = (_silu(a) * u * gs_refs[part][r0:r0 + nr, :]).astype(BF16)

    @pl.when(s >= nf)
    def _():
        wds = [wd_ref[f * tf:(f + 1) * tf, :].astype(BF16) for f in range(nf)]
        for part, r0, nr, h0 in groups:
            acc = _bdot(hid_scr[0, h0:h0 + nr, :], wds[0])
            for f in range(1, nf):
                acc = acc + _bdot(hid_scr[f, h0:h0 + nr, :], wds[f])
            y_refs[part][r0:r0 + nr, :] = acc.astype(BF16)


def _expert_ffn(parts, wg, wu, wd):
    e, _, k = parts[0][0].shape
    ff = wg.shape[2]
    tf, td = 256, 512
    nf, nd = ff // tf, k // td
    groups, h0 = [], 0
    for pi, (xs, _) in enumerate(parts):
        m = xs.shape[1]
        rg = min(m, 1024)
        for r0 in range(0, m, rg):
            groups.append((pi, r0, rg, h0))
            h0 += rg
    kern = functools.partial(_ffn_kernel, nf=nf, tf=tf, groups=tuple(groups), n_parts=len(parts))
    row_spec = lambda m, w, **kw: pl.BlockSpec((None, m, w), lambda ei, s: (ei, 0, 0), **kw)
    up_idx = lambda ei, s: (ei, 0, jnp.minimum(s, nf - 1))
    down_idx = lambda ei, s: (ei, 0, jnp.maximum(s - nf, 0))
    return pl.pallas_call(
        kern,
        grid=(e, nf + nd),
        in_specs=([row_spec(xs.shape[1], k, pipeline_mode=pl.Buffered(1)) for xs, _ in parts]
                  + [row_spec(xs.shape[1], 1) for xs, _ in parts]
                  + [pl.BlockSpec((None, k, tf), up_idx), pl.BlockSpec((None, k, tf), up_idx),
                     pl.BlockSpec((None, ff, td), down_idx)]),
        out_specs=[pl.BlockSpec((None, xs.shape[1], td), down_idx) for xs, _ in parts],
        out_shape=[jax.ShapeDtypeStruct(xs.shape, BF16) for xs, _ in parts],
        scratch_shapes=[pltpu.VMEM((nf, h0, tf), BF16)],
        compiler_params=_cparams(("arbitrary", "arbitrary")),
        name="expert_ffn",
    )(*[xs for xs, _ in parts], *[gs for _, gs in parts], wg, wu, wd)


def _combine_kernel(slot_ref, y_ref, x_ref, g2_ref, o_ref, acc_scr, *, tt, cap, cap_pad, eg):
    e = pl.program_id(2)

    @pl.when(e == 0)
    def _():
        acc_scr[...] = jnp.zeros_like(acc_scr)

    slot_iota = lax.broadcasted_iota(jnp.int32, (cap_pad, tt), 0)
    onehot_t = jnp.concatenate([jnp.where(slot_iota == slot_ref[i], 1.0, 0.0) for i in range(eg)], axis=0)
    onehot = onehot_t.T[:, 0:eg * cap].astype(BF16)
    acc_scr[...] += _bdot(onehot, y_ref[...].reshape(eg * cap, y_ref.shape[2]))

    @pl.when(e == N_EXPERTS // eg - 1)
    def _():
        o_ref[...] = x_ref[...] + g2_ref[...] * acc_scr[...]


def _combine(slot_t, y, x, mod, mod_row_fn, nb, n, cap):
    k = x.shape[1]
    tt = min(n, 512)
    ntb = n // tt
    cap_pad = max(cap, 128)
    eg = 2 if cap_pad == cap else 1
    kern = functools.partial(_combine_kernel, tt=tt, cap=cap, cap_pad=cap_pad, eg=eg)
    return pl.pallas_call(
        kern,
        grid=(nb, ntb, N_EXPERTS // eg),
        in_specs=[
            pl.BlockSpec((None, eg, 1, tt), lambda b, t, e: (b, e, 0, t)),
            pl.BlockSpec((eg, cap, k), lambda b, t, e: (e, b, 0)),
            pl.BlockSpec((tt, k), lambda b, t, e: (b * ntb + t, 0)),
            pl.BlockSpec((None, None, 1, k), lambda b, t, e: (mod_row_fn(b), 5, 0, 0)),
        ],
        out_specs=pl.BlockSpec((tt, k), lambda b, t, e: (b * ntb + t, 0)),
        out_shape=jax.ShapeDtypeStruct((nb * n, k), F32),
        scratch_shapes=[pltpu.VMEM((tt, k), F32)],
        compiler_params=_cparams(("arbitrary", "arbitrary", "arbitrary")),
        name="combine",
    )(slot_t.reshape(nb, N_EXPERTS, 1, n), y, x, mod)


def _ec_moe_residual(streams, gain, mod, router, wg, wu, wd, nb):
    routed = []
    for x, n, mod_row_fn in streams:
        cap = EC_CAPACITY_FACTOR * n // N_EXPERTS
        idx = jnp.arange(n, dtype=jnp.int32)
        tri = (idx[:, None] < idx[None, :]).astype(BF16)
        h, gate_t, slot_t = _route(x, gain, mod, mod_row_fn, router.T, tri, nb, n, cap)
        routed.append((slot_t, cap) + _gather(slot_t, gate_t, h, nb, n, cap))
    ys = _expert_ffn([(xs, gs) for _, _, xs, gs in routed], wg, wu, wd)
    return [_combine(slot_t, y, x, mod, mod_row_fn, nb, n, cap)
            for (x, n, mod_row_fn), (slot_t, cap, _, _), y in zip(streams, routed, ys)]


def _rope_tables(n, tm):
    rows = n // GRID_W
    row = jnp.repeat(jnp.arange(rows, dtype=F32), GRID_W)
    col = jnp.tile(jnp.arange(GRID_W, dtype=F32), rows)
    axis_dim = HEAD_DIM // 2
    inv_freq = ROPE_THETA ** (-jnp.arange(0, axis_dim, 2, dtype=F32) / axis_dim)
    ang = jnp.concatenate([row[:, None] * inv_freq, col[:, None] * inv_freq], axis=-1)
    cos, sin = jnp.cos(ang), jnp.sin(ang)
    cos_full = jnp.concatenate([cos, cos], axis=-1).reshape(n // tm, tm, HEAD_DIM)
    sin_signed = jnp.concatenate([-sin, sin], axis=-1).reshape(n // tm, tm, HEAD_DIM)
    return cos_full, sin_signed


def kernel(x, c, ctx, c_ctx,
           mod_w_0, mod_b_0, norm1_0, norm2_0, wqkv_0, wo_0, qnorm_0, knorm_0,
           lam_q1_0, lam_k1_0, lam_q2_0, lam_k2_0, subln_0, router_0, we_gate_0, we_up_0, we_down_0,
           mod_w_1, mod_b_1, norm1_1, norm2_1, wqkv_1, wo_1, qnorm_1, knorm_1,
           router_1, we_gate_1, we_up_1, we_down_1):
    nb, nq, d = x.shape
    nc = ctx.shape[1]
    tm = 1024
    x_lat = x.reshape(nb * nq, d)
    x_ctx = ctx.reshape(nb * nc, d)
    cond = jnp.zeros((MOD_ROWS, d), F32).at[:nb].set(c).at[nb].set(c_ctx)
    lat_row = lambda i: i // (nq // tm)
    ctx_row = lambda i: nb
    lat_b = lambda b: b
    cos, sin = _rope_tables(nq, tm)
    cos_id = jnp.ones((1, tm, HEAD_DIM), F32)
    sin_id = jnp.zeros((1, tm, HEAD_DIM), F32)
    q_scale = math.log2(math.e) / math.sqrt(HEAD_DIM)

    mod0 = _adaln(cond, mod_w_0, mod_b_0)
    n_heads2 = d // HEAD_DIM
    colgain0 = jnp.concatenate([jnp.tile(qnorm_0 * q_scale, n_heads2), jnp.tile(knorm_0, n_heads2),
                                jnp.ones((d,), F32)]).reshape(1, 3 * d)
    qkv_lat = _qkv_proj(x_lat, norm1_0, mod0, lat_row, wqkv_0, colgain0, cos, sin, 2 * d)
    qkv_ctx = _qkv_proj(x_ctx, norm1_0, mod0, ctx_row, wqkv_0, colgain0, cos_id, sin_id, 2 * d)
    lam_init = 0.8 - 0.6 * math.exp(-0.3 * 0)
    lam_params = jnp.stack([lam_q1_0, lam_k1_0, lam_q2_0, lam_k2_0])
    o_lat = _diff_attention(lam_params, subln_0, qkv_lat, [(qkv_lat, nq), (qkv_ctx, nc)], nb, nq, lam_init)
    o_ctx = _diff_attention(lam_params, subln_0, qkv_ctx, [(qkv_ctx, nc)], nb, nc, lam_init)
    x_lat = _proj_residual(o_lat, wo_0, x_lat, mod0, lat_row, 2)
    x_ctx = _proj_residual(o_ctx, wo_0, x_ctx, mod0, ctx_row, 2)
    x_lat, x_ctx = _ec_moe_residual([(x_lat, nq, lat_b), (x_ctx, nc, ctx_row)], norm2_0, mod0,
                                    router_0, we_gate_0, we_up_0, we_down_0, nb)

    mod1 = _adaln(cond, mod_w_1, mod_b_1)
    kv_dim = (wqkv_1.shape[1] - d) // 2
    colgain1 = jnp.concatenate([jnp.tile(qnorm_1 * q_scale, d // HEAD_DIM),
                                jnp.tile(knorm_1, kv_dim // HEAD_DIM),
                                jnp.ones((kv_dim,), F32)]).reshape(1, d + 2 * kv_dim)
    qkv_lat = _qkv_proj(x_lat, norm1_1, mod1, lat_row, wqkv_1, colgain1, cos, sin, d + kv_dim)
    kv_ctx = _qkv_proj(x_ctx, norm1_1, mod1, ctx_row, wqkv_1, colgain1, cos_id, sin_id, d + kv_dim, col_start=d)
    o_lat = _gqa_attention(qkv_lat, kv_ctx, nb, nq, nc)
    x_lat = _proj_residual(o_lat, wo_1, x_lat, mod1, lat_row, 2)
    (x_lat,) = _ec_moe_residual([(x_lat, nq, lat_b)], norm2_1, mod1,
                                router_1, we_gate_1, we_up_1, we_down_1, nb)
    return x_lat.reshape(nb, nq, d)
```

```python
import functools
import math

import jax
import jax.numpy as jnp
from jax import lax
from jax.experimental import pallas as pl
from jax.experimental.pallas import tpu as pltpu

F32 = jnp.float32
BF16 = jnp.bfloat16

D_MODEL = 4096
GRID_W = 64
HEAD_DIM = 128
N_EXPERTS = 16
EC_CAPACITY_FACTOR = 2
EXPERT_FF = D_MODEL // 4
ROPE_THETA = 10000.0
NORM_EPS = 1e-6
N_MOD = 6
MOD_ROWS = 16
VMEM_LIMIT = 58 * 1024 * 1024


def _cparams(sem):
    return pltpu.CompilerParams(dimension_semantics=sem, vmem_limit_bytes=VMEM_LIMIT)


def _bdot(a, b):
    return jnp.dot(a, b, preferred_element_type=F32)


def _dot_nt(a, b):
    return lax.dot_general(a, b, (((1,), (1,)), ((), ())), preferred_element_type=F32)


def _silu(x):
    return x * (1.0 / (1.0 + jnp.exp(-x)))


def _split_bf16(x):
    hi = x.astype(BF16)
    lo = (x - hi.astype(F32)).astype(BF16)
    return hi, lo


def _adaln_kernel(c_ref, w_ref, b_ref, o_ref):
    s = _silu(c_ref[...]).astype(BF16)
    o_ref[...] = _bdot(s, w_ref[...].astype(BF16)) + b_ref[...]


def _adaln(cond, w, b):
    n = w.shape[1]
    tn = 512
    out = pl.pallas_call(
        _adaln_kernel,
        grid=(n // tn,),
        in_specs=[
            pl.BlockSpec((MOD_ROWS, D_MODEL), lambda j: (0, 0)),
            pl.BlockSpec((D_MODEL, tn), lambda j: (0, j)),
            pl.BlockSpec((1, tn), lambda j: (0, j)),
        ],
        out_specs=pl.BlockSpec((MOD_ROWS, tn), lambda j: (0, j)),
        out_shape=jax.ShapeDtypeStruct((MOD_ROWS, n), F32),
        compiler_params=_cparams(("arbitrary",)),
        name="adaln",
    )(cond, w, b.reshape(1, n))
    return out.reshape(MOD_ROWS, N_MOD, 1, D_MODEL)


def _norm_mod(xv, g, shift, scale):
    ms = jnp.mean(xv * xv, axis=-1, keepdims=True)
    y = xv * lax.rsqrt(ms + NORM_EPS) * g
    return y * (1.0 + scale) + shift


def _qkv_kernel(x_hbm, g_ref, sh_ref, sc_ref, w_ref, cg_ref, cos_ref, sin_ref, o_ref, h_scr, x_ref, x_sem,
                *, tm, tn, n_qk_tiles, chunk, row_group, n_row_blocks):
    i = pl.program_id(0)
    j = pl.program_id(1)

    def x_copy(blk):
        return pltpu.make_async_copy(x_hbm.at[pl.ds(pl.multiple_of(blk * tm, tm), tm)], x_ref, x_sem)

    @pl.when((i == 0) & (j == 0))
    def _():
        x_copy(0).start()

    @pl.when(j == 0)
    def _():
        x_copy(i).wait()

    @pl.when((j == 1) & (i + 1 < n_row_blocks))
    def _():
        x_copy(i + 1).start()

    def qk_tile(make_h):
        w = w_ref[...].astype(BF16)
        for r in range(tm // row_group):
            rows = slice(r * row_group, (r + 1) * row_group)
            if make_h:
                for c in range(row_group // chunk):
                    rr = slice(r * row_group + c * chunk, r * row_group + (c + 1) * chunk)
                    h = _norm_mod(x_ref[rr, :], g_ref[...], sh_ref[...], sc_ref[...])
                    h_scr[rr, :] = h.astype(BF16)
            acc = _bdot(h_scr[rows, :], w)
            cos = cos_ref[rows, :]
            sin = sin_ref[rows, :]
            for gi in range(tn // HEAD_DIM):
                cols = slice(gi * HEAD_DIM, (gi + 1) * HEAD_DIM)
                a = acc[:, cols]
                ms = jnp.mean(a * a, axis=-1, keepdims=True)
                a = a * lax.rsqrt(ms + NORM_EPS) * cg_ref[:, cols]
                a = a * cos + pltpu.roll(a, HEAD_DIM // 2, axis=1) * sin
                o_ref[rows, cols] = a.astype(BF16)

    @pl.when(j == 0)
    def _():
        qk_tile(True)

    @pl.when((j > 0) & (j < n_qk_tiles))
    def _():
        qk_tile(False)

    @pl.when(j >= n_qk_tiles)
    def _():
        o_ref[...] = _bdot(h_scr[...], w_ref[...].astype(BF16)).astype(BF16)


def _qkv_proj(x, gain, mod, mod_row_fn, w, colgain, cos, sin, n_qk_cols, col_start=0):
    r, k = x.shape
    tm, tn = 1024, 512
    n = w.shape[1] - col_start
    j0 = col_start // tn
    npos = cos.shape[0]
    assert n // tn >= 2
    kern = functools.partial(_qkv_kernel, tm=tm, tn=tn, n_qk_tiles=(n_qk_cols - col_start) // tn, chunk=32,
                             row_group=256, n_row_blocks=r // tm)
    return pl.pallas_call(
        kern,
        grid=(r // tm, n // tn),
        in_specs=[
            pl.BlockSpec(memory_space=pl.ANY),
            pl.BlockSpec((1, k), lambda i, j: (0, 0)),
            pl.BlockSpec((None, None, 1, k), lambda i, j: (mod_row_fn(i), 0, 0, 0)),
            pl.BlockSpec((None, None, 1, k), lambda i, j: (mod_row_fn(i), 1, 0, 0)),
            pl.BlockSpec((k, tn), lambda i, j: (0, j0 + j)),
            pl.BlockSpec((1, tn), lambda i, j: (0, j0 + j)),
            pl.BlockSpec((None, tm, HEAD_DIM), lambda i, j: (i % npos, 0, 0)),
            pl.BlockSpec((None, tm, HEAD_DIM), lambda i, j: (i % npos, 0, 0)),
        ],
        out_specs=pl.BlockSpec((tm, tn), lambda i, j: (i, j)),
        out_shape=jax.ShapeDtypeStruct((r, n), BF16),
        scratch_shapes=[pltpu.VMEM((tm, k), BF16), pltpu.VMEM((tm, k), F32), pltpu.SemaphoreType.DMA(())],
        compiler_params=_cparams(("arbitrary", "arbitrary")),
        name="qkv_proj",
    )(x, gain.reshape(1, k), mod, mod, w, colgain, cos, sin)


def _attend(q, kvs, row_block=16, ones_cols=0):
    tq = q.shape[0]
    ss = [_dot_nt(q, k) for k, _ in kvs]
    p_blocks = [[] for _ in kvs]
    l_blocks = []
    for r in range(tq // row_block):
        rows = slice(r * row_block, (r + 1) * row_block)
        sb = [s[rows, :] for s in ss]
        m = functools.reduce(jnp.maximum, [jnp.max(x, axis=-1, keepdims=True) for x in sb])
        pb = [jnp.exp2(x - m) for x in sb]
        if not ones_cols:
            l_blocks.append(functools.reduce(lambda a, b: a + b,
                                             [jnp.sum(p, axis=-1, keepdims=True) for p in pb]))
        for blocks, p in zip(p_blocks, pb):
            blocks.append(p.astype(BF16))
    o = functools.reduce(lambda a, b: a + b,
                         [_bdot(jnp.concatenate(blocks, axis=0), v) for blocks, (_, v) in zip(p_blocks, kvs)])
    if ones_cols:
        dv = o.shape[1] - ones_cols
        return o[:, 0:dv] / o[:, dv:dv + 1]
    return o / jnp.concatenate(l_blocks, axis=0)


def _diff_lambda(lam_ref, lam_init):
    lp = lam_ref[...]
    s1 = jnp.sum(lp[0:1, :] * lp[1:2, :], axis=-1, keepdims=True)
    s2 = jnp.sum(lp[2:3, :] * lp[3:4, :], axis=-1, keepdims=True)
    return jnp.exp(s1) - jnp.exp(s2) + lam_init


def _diff_attn_kernel(*refs, lam_init, n_kv, sub_rows=256):
    lam_ref, subln_ref, q_ref = refs[:3]
    k_refs = refs[3:3 + n_kv]
    v_refs = refs[3 + n_kv:3 + 2 * n_kv]
    o_ref = refs[3 + 2 * n_kv]
    lam = _diff_lambda(lam_ref, lam_init)
    tq = q_ref.shape[0]
    sub = min(tq, sub_rows)
    for qs in range(tq // sub):
        rows = slice(qs * sub, (qs + 1) * sub)
        outs = []
        for t in range(2):
            cols = slice(t * HEAD_DIM, (t + 1) * HEAD_DIM)
            kvs = [(kr[:, cols], vr[...]) for kr, vr in zip(k_refs, v_refs)]
            outs.append(_attend(q_ref[rows, cols], kvs))
        o = outs[0] - lam * outs[1]
        ms = jnp.mean(o * o, axis=-1, keepdims=True)
        o = o * lax.rsqrt(ms + NORM_EPS) * subln_ref[...] * (1.0 - lam_init)
        o_ref[rows, :] = o.astype(BF16)


def _diff_attention(lam_params, subln, qkv_q, kv_sources, nb, nq, lam_init):
    heads = D_MODEL // (2 * HEAD_DIM)
    hw = 2 * HEAD_DIM
    tq = min(nq, 2048)
    nqb = nq // tq
    n_kv = len(kv_sources)
    in_specs = [
        pl.BlockSpec((4, HEAD_DIM), lambda b, h, i: (0, 0)),
        pl.BlockSpec((1, hw), lambda b, h, i: (0, 0)),
        pl.BlockSpec((tq, hw), lambda b, h, i: (b * nqb + i, h)),
    ]
    args = [lam_params, subln.reshape(1, hw), qkv_q]
    for arr, nk in kv_sources:
        in_specs.append(pl.BlockSpec((nk, hw), lambda b, h, i: (b, heads + h)))
        args.append(arr)
    for arr, nk in kv_sources:
        in_specs.append(pl.BlockSpec((nk, hw), lambda b, h, i: (b, 2 * heads + h)))
        args.append(arr)
    kern = functools.partial(_diff_attn_kernel, lam_init=lam_init, n_kv=n_kv)
    return pl.pallas_call(
        kern,
        grid=(nb, heads, nqb),
        in_specs=in_specs,
        out_specs=pl.BlockSpec((tq, hw), lambda b, h, i: (b * nqb + i, h)),
        out_shape=jax.ShapeDtypeStruct((nb * nq, D_MODEL), BF16),
        compiler_params=_cparams(("arbitrary", "arbitrary", "arbitrary")),
        name="diff_attn",
    )(*args)


def _gqa_attn_kernel(q_ref, kl_ref, kc_ref, vl_ref, vc_ref, o_ref, *, group, sub_rows=256):
    def with_ones(v):
        return jnp.concatenate([v, jnp.ones(v.shape, v.dtype)], axis=1)
    kvs = [(kl_ref[...], with_ones(vl_ref[...])), (kc_ref[...], with_ones(vc_ref[...]))]
    tq = q_ref.shape[0]
    sub = min(tq, sub_rows)
    for qs in range(tq // sub):
        rows = slice(qs * sub, (qs + 1) * sub)
        for g in range(group):
            cols = slice(g * HEAD_DIM, (g + 1) * HEAD_DIM)
            o_ref[rows, cols] = _attend(q_ref[rows, cols], kvs, ones_cols=HEAD_DIM).astype(BF16)


def _gqa_attention(qkv_lat, kv_ctx, nb, nq, nc):
    q_heads = D_MODEL // HEAD_DIM
    kv_heads = q_heads // 4
    group = q_heads // kv_heads
    gw = group * HEAD_DIM
    tq = 1024
    nqb = nq // tq
    k_off = D_MODEL // HEAD_DIM
    v_off = k_off + kv_heads
    kern = functools.partial(_gqa_attn_kernel, group=group)
    return pl.pallas_call(
        kern,
        grid=(nb, kv_heads, nqb),
        in_specs=[
            pl.BlockSpec((tq, gw), lambda b, h, i: (b * nqb + i, h)),
            pl.BlockSpec((nq, HEAD_DIM), lambda b, h, i: (b, k_off + h)),
            pl.BlockSpec((nc, HEAD_DIM), lambda b, h, i: (b, h)),
            pl.BlockSpec((nq, HEAD_DIM), lambda b, h, i: (b, v_off + h)),
            pl.BlockSpec((nc, HEAD_DIM), lambda b, h, i: (b, kv_heads + h)),
        ],
        out_specs=pl.BlockSpec((tq, gw), lambda b, h, i: (b * nqb + i, h)),
        out_shape=jax.ShapeDtypeStruct((nb * nq, D_MODEL), BF16),
        compiler_params=_cparams(("arbitrary", "arbitrary", "arbitrary")),
        name="gqa_attn",
    )(qkv_lat, qkv_lat, kv_ctx, qkv_lat, kv_ctx)


def _proj_res_kernel(o_ref, w_ref, res_ref, gate_ref, out_ref):
    acc = _bdot(o_ref[...], w_ref[...].astype(BF16))
    out_ref[...] = res_ref[...] + gate_ref[...] * acc


def _proj_residual(o, w, res, mod, mod_row_fn, gate_idx):
    r, k = o.shape
    n = w.shape[1]
    tm, tn = 1024, 512
    return pl.pallas_call(
        _proj_res_kernel,
        grid=(r // tm, n // tn),
        in_specs=[
            pl.BlockSpec((tm, k), lambda i, j: (i, 0)),
            pl.BlockSpec((k, tn), lambda i, j: (0, j)),
            pl.BlockSpec((tm, tn), lambda i, j: (i, j)),
            pl.BlockSpec((None, None, 1, tn), lambda i, j: (mod_row_fn(i), gate_idx, 0, j)),
        ],
        out_specs=pl.BlockSpec((tm, tn), lambda i, j: (i, j)),
        out_shape=jax.ShapeDtypeStruct((r, n), F32),
        compiler_params=_cparams(("arbitrary", "arbitrary")),
        name="out_proj",
    )(o, w, res, mod)


def _route_kernel(x_ref, g_ref, sh_ref, sc_ref, rt_ref, tri_ref, h_ref, gate_ref, slot_ref, lg_scr,
                  *, tc, n, cap):
    c = pl.program_id(1)
    h = _norm_mod(x_ref[...], g_ref[...], sh_ref[...], sc_ref[...])
    h_ref[...] = h.astype(BF16)
    h_hi, h_lo = _split_bf16(h)
    r_hi, r_lo = _split_bf16(rt_ref[...])
    lg_scr[c] = _dot_nt(r_hi, h_hi) + _dot_nt(r_hi, h_lo) + _dot_nt(r_lo, h_hi)

    @pl.when(c == n // tc - 1)
    def _():
        lg = jnp.concatenate([lg_scr[i] for i in range(n // tc)], axis=1)
        mx = jnp.max(lg, axis=0, keepdims=True)
        ex = jnp.exp(lg - mx)
        aff = ex / jnp.sum(ex, axis=0, keepdims=True)
        bits = pltpu.bitcast(aff, jnp.int32)
        thr = jnp.zeros((N_EXPERTS, 1), jnp.int32)
        for bit in range(30, -1, -1):
            cand = thr | jnp.int32(1 << bit)
            cnt = jnp.sum(jnp.where(bits >= cand, 1.0, 0.0), axis=1, keepdims=True)
            thr = jnp.where(cnt >= cap, cand, thr)
        gt = bits > thr
        eq = bits == thr
        need = cap - jnp.sum(jnp.where(gt, 1.0, 0.0), axis=1, keepdims=True)
        tri = tri_ref[...]
        eq_rank = _bdot(jnp.where(eq, 1.0, 0.0).astype(BF16), tri)
        sel = gt | (eq & (eq_rank < need))
        slot = _bdot(jnp.where(sel, 1.0, 0.0).astype(BF16), tri)
        gate_ref[...] = jnp.where(sel, aff, 0.0)
        slot_ref[...] = jnp.where(sel, slot.astype(jnp.int32), -1)


def _route(x, gain, mod, mod_row_fn, router_t, tri, nb, n, cap):
    k = x.shape[1]
    tc = 256
    nch = n // tc
    kern = functools.partial(_route_kernel, tc=tc, n=n, cap=cap)
    return pl.pallas_call(
        kern,
        grid=(nb, nch),
        in_specs=[
            pl.BlockSpec((tc, k), lambda b, c: (b * nch + c, 0)),
            pl.BlockSpec((1, k), lambda b, c: (0, 0)),
            pl.BlockSpec((None, None, 1, k), lambda b, c: (mod_row_fn(b), 3, 0, 0)),
            pl.BlockSpec((None, None, 1, k), lambda b, c: (mod_row_fn(b), 4, 0, 0)),
            pl.BlockSpec((N_EXPERTS, k), lambda b, c: (0, 0)),
            pl.BlockSpec((n, n), lambda b, c: (0, 0)),
        ],
        out_specs=[
            pl.BlockSpec((tc, k), lambda b, c: (b * nch + c, 0)),
            pl.BlockSpec((None, N_EXPERTS, n), lambda b, c: (b, 0, 0)),
            pl.BlockSpec((None, N_EXPERTS, n), lambda b, c: (b, 0, 0)),
        ],
        out_shape=[
            jax.ShapeDtypeStruct((nb * n, k), BF16),
            jax.ShapeDtypeStruct((nb, N_EXPERTS, n), F32),
            jax.ShapeDtypeStruct((nb, N_EXPERTS, n), jnp.int32),
        ],
        scratch_shapes=[pltpu.VMEM((nch, N_EXPERTS, tc), F32)],
        compiler_params=_cparams(("arbitrary", "arbitrary")),
        name="route",
    )(x, gain.reshape(1, k), mod, mod, router_t, tri)


def _gather_kernel(slot_ref, gate_ref, h_ref, xs_ref, gs_ref, *, cap, n):
    onehot = lax.broadcasted_iota(jnp.int32, (cap, n), 0) == slot_ref[...]
    xs_ref[...] = _bdot(jnp.where(onehot, 1.0, 0.0).astype(BF16), h_ref[...]).astype(BF16)
    gs_ref[...] = jnp.sum(jnp.where(onehot, gate_ref[...], 0.0), axis=1, keepdims=True)


def _gather(slot_t, gate_t, h, nb, n, cap):
    k = h.shape[1]
    kern = functools.partial(_gather_kernel, cap=cap, n=n)
    row_spec = pl.BlockSpec((None, None, 1, n), lambda b, e: (b, e, 0, 0))
    return pl.pallas_call(
        kern,
        grid=(nb, N_EXPERTS),
        in_specs=[row_spec, row_spec, pl.BlockSpec((n, k), lambda b, e: (b, 0))],
        out_specs=[pl.BlockSpec((None, cap, k), lambda b, e: (e, b, 0)),
                   pl.BlockSpec((None, cap, 1), lambda b, e: (e, b, 0))],
        out_shape=[jax.ShapeDtypeStruct((N_EXPERTS, nb * cap, k), BF16),
                   jax.ShapeDtypeStruct((N_EXPERTS, nb * cap, 1), F32)],
        compiler_params=_cparams(("arbitrary", "arbitrary")),
        name="gather",
    )(slot_t.reshape(nb, N_EXPERTS, 1, n), gate_t.reshape(nb, N_EXPERTS, 1, n), h)


def _ffn_kernel(*refs, nf, tf, groups, n_parts, n_experts):
    xs_hbm = refs[0]
    gs_refs = refs[n_parts:2 * n_parts]
    wg_ref, wu_ref, wd_ref = refs[2 * n_parts:2 * n_parts + 3]
    y_refs = refs[2 * n_parts + 3:3 * n_parts + 3]
    hid_scr, xs_buf, xs_sem = refs[3 * n_parts + 3:3 * n_parts + 6]
    xs_refs = (xs_buf,) + tuple(refs[1:n_parts])
    e = pl.program_id(0)
    s = pl.program_id(1)

    def xs_copy(expert):
        return pltpu.make_async_copy(xs_hbm.at[expert], xs_buf, xs_sem)

    @pl.when((e == 0) & (s == 0))
    def _():
        xs_copy(0).start()

    @pl.when(s == 0)
    def _():
        xs_copy(e).wait()

    @pl.when((s == nf) & (e + 1 < n_experts))
    def _():
        xs_copy(e + 1).start()

    @pl.when(s < nf)
    def _():
        wg = wg_ref[...].astype(BF16)
        wu = wu_ref[...].astype(BF16)
        for part, r0, nr, h0 in groups:
            xs = xs_refs[part][r0:r0 + nr, :]
            a = _bdot(xs, wg)
            u = _bdot(xs, wu)
            hid_scr[s, h0:h0 + nr, :] = (_silu(a) * u * gs_refs[part][r0:r0 + nr, :]).astype(BF16)

    @pl.when(s >= nf)
    def _():
        wds = [wd_ref[f * tf:(f + 1) * tf, :].astype(BF16) for f in range(nf)]
        for part, r0, nr, h0 in groups:
            acc = _bdot(hid_scr[0, h0:h0 + nr, :], wds[0])
            for f in range(1, nf):
                acc = acc + _bdot(hid_scr[f, h0:h0 + nr, :], wds[f])
            y_refs[part][r0:r0 + nr, :] = acc.astype(BF16)


def _expert_ffn(parts, wg, wu, wd):
    e, _, k = parts[0][0].shape
    ff = wg.shape[2]
    tf, td = 256, 512
    nf, nd = ff // tf, k // td
    groups, h0 = [], 0
    for pi, (xs, _) in enumerate(parts):
        m = xs.shape[1]
        rg = min(m, 1024)
        for r0 in range(0, m, rg):
            groups.append((pi, r0, rg, h0))
            h0 += rg
    kern = functools.partial(_ffn_kernel, nf=nf, tf=tf, groups=tuple(groups), n_parts=len(parts), n_experts=e)
    row_spec = lambda m, w, **kw: pl.BlockSpec((None, m, w), lambda ei, s: (ei, 0, 0), **kw)
    up_idx = lambda ei, s: (ei, 0, jnp.minimum(s, nf - 1))
    down_idx = lambda ei, s: (ei, 0, jnp.maximum(s - nf, 0))
    return pl.pallas_call(
        kern,
        grid=(e, nf + nd),
        in_specs=([pl.BlockSpec(memory_space=pl.ANY)]
                  + [row_spec(xs.shape[1], k, pipeline_mode=pl.Buffered(1)) for xs, _ in parts[1:]]
                  + [row_spec(xs.shape[1], 1) for xs, _ in parts]
                  + [pl.BlockSpec((None, k, tf), up_idx), pl.BlockSpec((None, k, tf), up_idx),
                     pl.BlockSpec((None, ff, td), down_idx)]),
        out_specs=[pl.BlockSpec((None, xs.shape[1], td), down_idx) for xs, _ in parts],
        out_shape=[jax.ShapeDtypeStruct(xs.shape, BF16) for xs, _ in parts],
        scratch_shapes=[pltpu.VMEM((nf, h0, tf), BF16), pltpu.VMEM(parts[0][0].shape[1:], BF16),
                        pltpu.SemaphoreType.DMA(())],
        compiler_params=_cparams(("arbitrary", "arbitrary")),
        name="expert_ffn",
    )(*[xs for xs, _ in parts], *[gs for _, gs in parts], wg, wu, wd)


def _combine_kernel(slot_ref, y_ref, x_ref, g2_ref, o_ref, acc_scr, *, tt, cap, cap_pad, eg):
    e = pl.program_id(2)

    @pl.when(e == 0)
    def _():
        acc_scr[...] = jnp.zeros_like(acc_scr)

    slot_iota = lax.broadcasted_iota(jnp.int32, (cap_pad, tt), 0)
    onehot_t = jnp.concatenate([jnp.where(slot_iota == slot_ref[i], 1.0, 0.0) for i in range(eg)], axis=0)
    onehot = onehot_t.T[:, 0:eg * cap].astype(BF16)
    acc_scr[...] += _bdot(onehot, y_ref[...].reshape(eg * cap, y_ref.shape[2]))

    @pl.when(e == N_EXPERTS // eg - 1)
    def _():
        o_ref[...] = x_ref[...] + g2_ref[...] * acc_scr[...]


def _combine(slot_t, y, x, mod, mod_row_fn, nb, n, cap):
    k = x.shape[1]
    tt = min(n, 512)
    ntb = n // tt
    cap_pad = max(cap, 128)
    eg = 2 if cap_pad == cap else 1
    kern = functools.partial(_combine_kernel, tt=tt, cap=cap, cap_pad=cap_pad, eg=eg)
    return pl.pallas_call(
        kern,
        grid=(nb, ntb, N_EXPERTS // eg),
        in_specs=[
            pl.BlockSpec((None, eg, 1, tt), lambda b, t, e: (b, e, 0, t)),
            pl.BlockSpec((eg, cap, k), lambda b, t, e: (e, b, 0)),
            pl.BlockSpec((tt, k), lambda b, t, e: (b * ntb + t, 0)),
            pl.BlockSpec((None, None, 1, k), lambda b, t, e: (mod_row_fn(b), 5, 0, 0)),
        ],
        out_specs=pl.BlockSpec((tt, k), lambda b, t, e: (b * ntb + t, 0)),
        out_shape=jax.ShapeDtypeStruct((nb * n, k), F32),
        scratch_shapes=[pltpu.VMEM((tt, k), F32)],
        compiler_params=_cparams(("arbitrary", "arbitrary", "arbitrary")),
        name="combine",
    )(slot_t.reshape(nb, N_EXPERTS, 1, n), y, x, mod)


def _ec_moe_residual(streams, gain, mod, router, wg, wu, wd, nb):
    routed = []
    for x, n, mod_row_fn in streams:
        cap = EC_CAPACITY_FACTOR * n // N_EXPERTS
        idx = jnp.arange(n, dtype=jnp.int32)
        tri = (idx[:, None] < idx[None, :]).astype(BF16)
        h, gate_t, slot_t = _route(x, gain, mod, mod_row_fn, router.T, tri, nb, n, cap)
        routed.append((slot_t, cap) + _gather(slot_t, gate_t, h, nb, n, cap))
    ys = _expert_ffn([(xs, gs) for _, _, xs, gs in routed], wg, wu, wd)
    return [_combine(slot_t, y, x, mod, mod_row_fn, nb, n, cap)
            for (x, n, mod_row_fn), (slot_t, cap, _, _), y in zip(streams, routed, ys)]


def _rope_tables(n, tm):
    rows = n // GRID_W
    row = jnp.repeat(jnp.arange(rows, dtype=F32), GRID_W)
    col = jnp.tile(jnp.arange(GRID_W, dtype=F32), rows)
    axis_dim = HEAD_DIM // 2
    inv_freq = ROPE_THETA ** (-jnp.arange(0, axis_dim, 2, dtype=F32) / axis_dim)
    ang = jnp.concatenate([row[:, None] * inv_freq, col[:, None] * inv_freq], axis=-1)
    cos, sin = jnp.cos(ang), jnp.sin(ang)
    cos_full = jnp.concatenate([cos, cos], axis=-1).reshape(n // tm, tm, HEAD_DIM)
    sin_signed = jnp.concatenate([-sin, sin], axis=-1).reshape(n // tm, tm, HEAD_DIM)
    return cos_full, sin_signed


def kernel(x, c, ctx, c_ctx,
           mod_w_0, mod_b_0, norm1_0, norm2_0, wqkv_0, wo_0, qnorm_0, knorm_0,
           lam_q1_0, lam_k1_0, lam_q2_0, lam_k2_0, subln_0, router_0, we_gate_0, we_up_0, we_down_0,
           mod_w_1, mod_b_1, norm1_1, norm2_1, wqkv_1, wo_1, qnorm_1, knorm_1,
           router_1, we_gate_1, we_up_1, we_down_1):
    nb, nq, d = x.shape
    nc = ctx.shape[1]
    tm = 1024
    x_lat = x.reshape(nb * nq, d)
    x_ctx = ctx.reshape(nb * nc, d)
    cond = jnp.zeros((MOD_ROWS, d), F32).at[:nb].set(c).at[nb].set(c_ctx)
    lat_row = lambda i: i // (nq // tm)
    ctx_row = lambda i: nb
    lat_b = lambda b: b
    cos, sin = _rope_tables(nq, tm)
    cos_id = jnp.ones((1, tm, HEAD_DIM), F32)
    sin_id = jnp.zeros((1, tm, HEAD_DIM), F32)
    q_scale = math.log2(math.e) / math.sqrt(HEAD_DIM)

    mod0 = _adaln(cond, mod_w_0, mod_b_0)
    n_heads2 = d // HEAD_DIM
    colgain0 = jnp.concatenate([jnp.tile(qnorm_0 * q_scale, n_heads2), jnp.tile(knorm_0, n_heads2),
                                jnp.ones((d,), F32)]).reshape(1, 3 * d)
    qkv_lat = _qkv_proj(x_lat, norm1_0, mod0, lat_row, wqkv_0, colgain0, cos, sin, 2 * d)
    qkv_ctx = _qkv_proj(x_ctx, norm1_0, mod0, ctx_row, wqkv_0, colgain0, cos_id, sin_id, 2 * d)
    lam_init = 0.8 - 0.6 * math.exp(-0.3 * 0)
    lam_params = jnp.stack([lam_q1_0, lam_k1_0, lam_q2_0, lam_k2_0])
    o_lat = _diff_attention(lam_params, subln_0, qkv_lat, [(qkv_lat, nq), (qkv_ctx, nc)], nb, nq, lam_init)
    o_ctx = _diff_attention(lam_params, subln_0, qkv_ctx, [(qkv_ctx, nc)], nb, nc, lam_init)
    x_lat = _proj_residual(o_lat, wo_0, x_lat, mod0, lat_row, 2)
    x_ctx = _proj_residual(o_ctx, wo_0, x_ctx, mod0, ctx_row, 2)
    x_lat, x_ctx = _ec_moe_residual([(x_lat, nq, lat_b), (x_ctx, nc, ctx_row)], norm2_0, mod0,
                                    router_0, we_gate_0, we_up_0, we_down_0, nb)

    mod1 = _adaln(cond, mod_w_1, mod_b_1)
    kv_dim = (wqkv_1.shape[1] - d) // 2
    colgain1 = jnp.concatenate([jnp.tile(qnorm_1 * q_scale, d // HEAD_DIM),
                                jnp.tile(knorm_1, kv_dim // HEAD_DIM),
                                jnp.ones((kv_dim,), F32)]).reshape(1, d + 2 * kv_dim)
    qkv_lat = _qkv_proj(x_lat, norm1_1, mod1, lat_row, wqkv_1, colgain1, cos, sin, d + kv_dim)
    kv_ctx = _qkv_proj(x_ctx, norm1_1, mod1, ctx_row, wqkv_1, colgain1, cos_id, sin_id, d + kv_dim, col_start=d)
    o_lat = _gqa_attention(qkv_lat, kv_ctx, nb, nq, nc)
    x_lat = _proj_residual(o_lat, wo_1, x_lat, mod1, lat_row, 2)
    (x_lat,) = _ec_moe_residual([(x_lat, nq, lat_b)], norm2_1, mod1,
                                router_1, we_gate_1, we_up_1, we_down_1, nb)
    return x_lat.reshape(nb, nq, d)
```

```python
import functools
import math

import jax
import jax.numpy as jnp
from jax import lax
from jax.experimental import pallas as pl
from jax.experimental.pallas import tpu as pltpu

F32 = jnp.float32
BF16 = jnp.bfloat16

D_MODEL = 4096
GRID_W = 64
HEAD_DIM = 128
N_EXPERTS = 16
EC_CAPACITY_FACTOR = 2
EXPERT_FF = D_MODEL // 4
ROPE_THETA = 10000.0
NORM_EPS = 1e-6
N_MOD = 6
MOD_ROWS = 16
LANES = 128
VMEM_LIMIT = 58 * 1024 * 1024


def _cparams(sem):
    return pltpu.CompilerParams(dimension_semantics=sem, vmem_limit_bytes=VMEM_LIMIT)


def _bdot(a, b):
    return jnp.dot(a, b, preferred_element_type=F32)


def _dot_nt(a, b):
    return lax.dot_general(a, b, (((1,), (1,)), ((), ())), preferred_element_type=F32)


def _silu(x):
    return x * (1.0 / (1.0 + jnp.exp(-x)))


def _split_bf16(x):
    hi = x.astype(BF16)
    lo = (x - hi.astype(F32)).astype(BF16)
    return hi, lo


def _adaln_kernel(c_ref, w_ref, b_ref, o_ref):
    s = _silu(c_ref[...]).astype(BF16)
    o_ref[...] = _bdot(s, w_ref[...].astype(BF16)) + b_ref[...]


def _adaln(cond, w, b):
    n = w.shape[1]
    tn = 512
    out = pl.pallas_call(
        _adaln_kernel,
        grid=(n // tn,),
        in_specs=[
            pl.BlockSpec((MOD_ROWS, D_MODEL), lambda j: (0, 0)),
            pl.BlockSpec((D_MODEL, tn), lambda j: (0, j)),
            pl.BlockSpec((1, tn), lambda j: (0, j)),
        ],
        out_specs=pl.BlockSpec((MOD_ROWS, tn), lambda j: (0, j)),
        out_shape=jax.ShapeDtypeStruct((MOD_ROWS, n), F32),
        compiler_params=_cparams(("arbitrary",)),
        name="adaln",
    )(cond, w, b.reshape(1, n))
    return out.reshape(MOD_ROWS, N_MOD, 1, D_MODEL)


def _norm_mod(xv, g, shift, scale):
    ms = jnp.mean(xv * xv, axis=-1, keepdims=True)
    y = xv * lax.rsqrt(ms + NORM_EPS) * g
    return y * (1.0 + scale) + shift


def _qkv_kernel(x_hbm, g_ref, sh_ref, sc_ref, w_ref, cg_ref, cos_ref, sin_ref, o_ref, h_scr, x_ref, x_sem,
                *, tm, tn, n_qk_tiles, chunk, row_group, n_row_blocks):
    i = pl.program_id(0)
    j = pl.program_id(1)

    def x_copy(blk):
        return pltpu.make_async_copy(x_hbm.at[pl.ds(pl.multiple_of(blk * tm, tm), tm)], x_ref, x_sem)

    @pl.when((i == 0) & (j == 0))
    def _():
        x_copy(0).start()

    @pl.when(j == 0)
    def _():
        x_copy(i).wait()

    @pl.when((j == 1) & (i + 1 < n_row_blocks))
    def _():
        x_copy(i + 1).start()

    def qk_tile(make_h):
        w = w_ref[...].astype(BF16)
        for r in range(tm // row_group):
            rows = slice(r * row_group, (r + 1) * row_group)
            if make_h:
                for c in range(row_group // chunk):
                    rr = slice(r * row_group + c * chunk, r * row_group + (c + 1) * chunk)
                    h = _norm_mod(x_ref[rr, :], g_ref[...], sh_ref[...], sc_ref[...])
                    h_scr[rr, :] = h.astype(BF16)
            acc = _bdot(h_scr[rows, :], w)
            cos = cos_ref[rows, :]
            sin = sin_ref[rows, :]
            for gi in range(tn // HEAD_DIM):
                cols = slice(gi * HEAD_DIM, (gi + 1) * HEAD_DIM)
                a = acc[:, cols]
                ms = jnp.mean(a * a, axis=-1, keepdims=True)
                a = a * lax.rsqrt(ms + NORM_EPS) * cg_ref[:, cols]
                a = a * cos + pltpu.roll(a, HEAD_DIM // 2, axis=1) * sin
                o_ref[rows, cols] = a.astype(BF16)

    @pl.when(j == 0)
    def _():
        qk_tile(True)

    @pl.when((j > 0) & (j < n_qk_tiles))
    def _():
        qk_tile(False)

    @pl.when(j >= n_qk_tiles)
    def _():
        o_ref[...] = _bdot(h_scr[...], w_ref[...].astype(BF16)).astype(BF16)


def _qkv_proj(x, gain, mod, mod_row_fn, w, colgain, cos, sin, n_qk_cols, col_start=0):
    r, k = x.shape
    tm, tn = 1024, 512
    n = w.shape[1] - col_start
    j0 = col_start // tn
    npos = cos.shape[0]
    assert n // tn >= 2
    kern = functools.partial(_qkv_kernel, tm=tm, tn=tn, n_qk_tiles=(n_qk_cols - col_start) // tn, chunk=32,
                             row_group=256, n_row_blocks=r // tm)
    return pl.pallas_call(
        kern,
        grid=(r // tm, n // tn),
        in_specs=[
            pl.BlockSpec(memory_space=pl.ANY),
            pl.BlockSpec((1, k), lambda i, j: (0, 0)),
            pl.BlockSpec((None, None, 1, k), lambda i, j: (mod_row_fn(i), 0, 0, 0)),
            pl.BlockSpec((None, None, 1, k), lambda i, j: (mod_row_fn(i), 1, 0, 0)),
            pl.BlockSpec((k, tn), lambda i, j: (0, j0 + j)),
            pl.BlockSpec((1, tn), lambda i, j: (0, j0 + j)),
            pl.BlockSpec((None, tm, HEAD_DIM), lambda i, j: (i % npos, 0, 0)),
            pl.BlockSpec((None, tm, HEAD_DIM), lambda i, j: (i % npos, 0, 0)),
        ],
        out_specs=pl.BlockSpec((tm, tn), lambda i, j: (i, j)),
        out_shape=jax.ShapeDtypeStruct((r, n), BF16),
        scratch_shapes=[pltpu.VMEM((tm, k), BF16), pltpu.VMEM((tm, k), F32), pltpu.SemaphoreType.DMA(())],
        compiler_params=_cparams(("arbitrary", "arbitrary")),
        name="qkv_proj",
    )(x, gain.reshape(1, k), mod, mod, w, colgain, cos, sin)


def _attend(q, kvs, row_block=16, ones_cols=0):
    tq = q.shape[0]
    ss = [_dot_nt(q, k) for k, _ in kvs]
    p_blocks = [[] for _ in kvs]
    l_blocks = []
    for r in range(tq // row_block):
        rows = slice(r * row_block, (r + 1) * row_block)
        sb = [s[rows, :] for s in ss]
        m = functools.reduce(jnp.maximum, [jnp.max(x, axis=-1, keepdims=True) for x in sb])
        pb = [jnp.exp2(x - m) for x in sb]
        if not ones_cols:
            l_blocks.append(functools.reduce(lambda a, b: a + b,
                                             [jnp.sum(p, axis=-1, keepdims=True) for p in pb]))
        for blocks, p in zip(p_blocks, pb):
            blocks.append(p.astype(BF16))
    o = functools.reduce(lambda a, b: a + b,
                         [_bdot(jnp.concatenate(blocks, axis=0), v) for blocks, (_, v) in zip(p_blocks, kvs)])
    if ones_cols:
        dv = o.shape[1] - ones_cols
        return o[:, 0:dv] / o[:, dv:dv + 1]
    return o / jnp.concatenate(l_blocks, axis=0)


def _diff_lambda(lam_ref, lam_init):
    lp = lam_ref[...]
    s1 = jnp.sum(lp[0:1, :] * lp[1:2, :], axis=-1, keepdims=True)
    s2 = jnp.sum(lp[2:3, :] * lp[3:4, :], axis=-1, keepdims=True)
    return jnp.exp(s1) - jnp.exp(s2) + lam_init


def _diff_attn_kernel(*refs, lam_init, n_kv, sub_rows=256):
    lam_ref, subln_ref, q_ref = refs[:3]
    k_refs = refs[3:3 + n_kv]
    v_refs = refs[3 + n_kv:3 + 2 * n_kv]
    o_ref = refs[3 + 2 * n_kv]
    lam = _diff_lambda(lam_ref, lam_init)
    tq = q_ref.shape[0]
    sub = min(tq, sub_rows)
    for qs in range(tq // sub):
        rows = slice(qs * sub, (qs + 1) * sub)
        outs = []
        for t in range(2):
            cols = slice(t * HEAD_DIM, (t + 1) * HEAD_DIM)
            kvs = [(kr[:, cols], vr[...]) for kr, vr in zip(k_refs, v_refs)]
            outs.append(_attend(q_ref[rows, cols], kvs))
        o = outs[0] - lam * outs[1]
        ms = jnp.mean(o * o, axis=-1, keepdims=True)
        o = o * lax.rsqrt(ms + NORM_EPS) * subln_ref[...] * (1.0 - lam_init)
        o_ref[rows, :] = o.astype(BF16)


def _diff_attention(lam_params, subln, qkv_q, kv_sources, nb, nq, lam_init):
    heads = D_MODEL // (2 * HEAD_DIM)
    hw = 2 * HEAD_DIM
    tq = min(nq, 2048)
    nqb = nq // tq
    n_kv = len(kv_sources)
    in_specs = [
        pl.BlockSpec((4, HEAD_DIM), lambda b, h, i: (0, 0)),
        pl.BlockSpec((1, hw), lambda b, h, i: (0, 0)),
        pl.BlockSpec((tq, hw), lambda b, h, i: (b * nqb + i, h)),
    ]
    args = [lam_params, subln.reshape(1, hw), qkv_q]
    for arr, nk in kv_sources:
        in_specs.append(pl.BlockSpec((nk, hw), lambda b, h, i: (b, heads + h)))
        args.append(arr)
    for arr, nk in kv_sources:
        in_specs.append(pl.BlockSpec((nk, hw), lambda b, h, i: (b, 2 * heads + h)))
        args.append(arr)
    kern = functools.partial(_diff_attn_kernel, lam_init=lam_init, n_kv=n_kv)
    return pl.pallas_call(
        kern,
        grid=(nb, heads, nqb),
        in_specs=in_specs,
        out_specs=pl.BlockSpec((tq, hw), lambda b, h, i: (b * nqb + i, h)),
        out_shape=jax.ShapeDtypeStruct((nb * nq, D_MODEL), BF16),
        compiler_params=_cparams(("arbitrary", "arbitrary", "arbitrary")),
        name="diff_attn",
    )(*args)


def _gqa_attn_kernel(q_ref, kl_ref, kc_ref, vl_ref, vc_ref, o_ref, *, group, sub_rows=256):
    def with_ones(v):
        return jnp.concatenate([v, jnp.ones(v.shape, v.dtype)], axis=1)
    kvs = [(kl_ref[...], with_ones(vl_ref[...])), (kc_ref[...], with_ones(vc_ref[...]))]
    tq = q_ref.shape[0]
    sub = min(tq, sub_rows)
    for qs in range(tq // sub):
        rows = slice(qs * sub, (qs + 1) * sub)
        for g in range(group):
            cols = slice(g * HEAD_DIM, (g + 1) * HEAD_DIM)
            o_ref[rows, cols] = _attend(q_ref[rows, cols], kvs, ones_cols=HEAD_DIM).astype(BF16)


def _gqa_attention(qkv_lat, kv_ctx, nb, nq, nc):
    q_heads = D_MODEL // HEAD_DIM
    kv_heads = q_heads // 4
    group = q_heads // kv_heads
    gw = group * HEAD_DIM
    tq = 1024
    nqb = nq // tq
    k_off = D_MODEL // HEAD_DIM
    v_off = k_off + kv_heads
    kern = functools.partial(_gqa_attn_kernel, group=group)
    return pl.pallas_call(
        kern,
        grid=(nb, kv_heads, nqb),
        in_specs=[
            pl.BlockSpec((tq, gw), lambda b, h, i: (b * nqb + i, h)),
            pl.BlockSpec((nq, HEAD_DIM), lambda b, h, i: (b, k_off + h)),
            pl.BlockSpec((nc, HEAD_DIM), lambda b, h, i: (b, h)),
            pl.BlockSpec((nq, HEAD_DIM), lambda b, h, i: (b, v_off + h)),
            pl.BlockSpec((nc, HEAD_DIM), lambda b, h, i: (b, kv_heads + h)),
        ],
        out_specs=pl.BlockSpec((tq, gw), lambda b, h, i: (b * nqb + i, h)),
        out_shape=jax.ShapeDtypeStruct((nb * nq, D_MODEL), BF16),
        compiler_params=_cparams(("arbitrary", "arbitrary", "arbitrary")),
        name="gqa_attn",
    )(qkv_lat, qkv_lat, kv_ctx, qkv_lat, kv_ctx)


def _proj_res_kernel(o_ref, w_ref, res_ref, gate_ref, out_ref):
    acc = _bdot(o_ref[...], w_ref[...].astype(BF16))
    out_ref[...] = res_ref[...] + gate_ref[...] * acc


def _proj_residual(o, w, res, mod, mod_row_fn, gate_idx):
    r, k = o.shape
    n = w.shape[1]
    tm, tn = 1024, 512
    return pl.pallas_call(
        _proj_res_kernel,
        grid=(r // tm, n // tn),
        in_specs=[
            pl.BlockSpec((tm, k), lambda i, j: (i, 0)),
            pl.BlockSpec((k, tn), lambda i, j: (0, j)),
            pl.BlockSpec((tm, tn), lambda i, j: (i, j)),
            pl.BlockSpec((None, None, 1, tn), lambda i, j: (mod_row_fn(i), gate_idx, 0, j)),
        ],
        out_specs=pl.BlockSpec((tm, tn), lambda i, j: (i, j)),
        out_shape=jax.ShapeDtypeStruct((r, n), F32),
        compiler_params=_cparams(("arbitrary", "arbitrary")),
        name="out_proj",
    )(o, w, res, mod)


def _route_kernel(x_ref, g_ref, sh_ref, sc_ref, rt_ref, tri_ref, h_ref, gate_ref, slot_ref, lg_scr,
                  *, tc, n, cap):
    c = pl.program_id(1)
    h = _norm_mod(x_ref[...], g_ref[...], sh_ref[...], sc_ref[...])
    h_ref[...] = h.astype(BF16)
    h_hi, h_lo = _split_bf16(h)
    r_hi, r_lo = _split_bf16(rt_ref[...])
    lg_scr[c] = _dot_nt(r_hi, h_hi) + _dot_nt(r_hi, h_lo) + _dot_nt(r_lo, h_hi)

    @pl.when(c == n // tc - 1)
    def _():
        lg = jnp.concatenate([lg_scr[i] for i in range(n // tc)], axis=1)
        mx = jnp.max(lg, axis=0, keepdims=True)
        ex = jnp.exp(lg - mx)
        aff = ex / jnp.sum(ex, axis=0, keepdims=True)
        bits = pltpu.bitcast(aff, jnp.int32)
        thr = jnp.zeros((N_EXPERTS, 1), jnp.int32)
        for bit in range(30, -1, -1):
            cand = thr | jnp.int32(1 << bit)
            cnt = jnp.sum(jnp.where(bits >= cand, 1.0, 0.0), axis=1, keepdims=True)
            thr = jnp.where(cnt >= cap, cand, thr)
        gt = bits > thr
        eq = bits == thr
        need = cap - jnp.sum(jnp.where(gt, 1.0, 0.0), axis=1, keepdims=True)
        tri = tri_ref[...]
        eq_rank = _bdot(jnp.where(eq, 1.0, 0.0).astype(BF16), tri)
        sel = gt | (eq & (eq_rank < need))
        slot = _bdot(jnp.where(sel, 1.0, 0.0).astype(BF16), tri)
        gate_ref[...] = jnp.where(sel, aff, 0.0)
        slot_ref[...] = jnp.where(sel, slot.astype(jnp.int32), -1)


def _route(x, gain, mod, mod_row_fn, router_t, tri, nb, n, cap):
    k = x.shape[1]
    tc = 256
    nch = n // tc
    kern = functools.partial(_route_kernel, tc=tc, n=n, cap=cap)
    return pl.pallas_call(
        kern,
        grid=(nb, nch),
        in_specs=[
            pl.BlockSpec((tc, k), lambda b, c: (b * nch + c, 0)),
            pl.BlockSpec((1, k), lambda b, c: (0, 0)),
            pl.BlockSpec((None, None, 1, k), lambda b, c: (mod_row_fn(b), 3, 0, 0)),
            pl.BlockSpec((None, None, 1, k), lambda b, c: (mod_row_fn(b), 4, 0, 0)),
            pl.BlockSpec((N_EXPERTS, k), lambda b, c: (0, 0)),
            pl.BlockSpec((n, n), lambda b, c: (0, 0)),
        ],
        out_specs=[
            pl.BlockSpec((tc, k), lambda b, c: (b * nch + c, 0)),
            pl.BlockSpec((None, N_EXPERTS, n), lambda b, c: (b, 0, 0)),
            pl.BlockSpec((None, N_EXPERTS, n), lambda b, c: (b, 0, 0)),
        ],
        out_shape=[
            jax.ShapeDtypeStruct((nb * n, k), BF16),
            jax.ShapeDtypeStruct((nb, N_EXPERTS, n), F32),
            jax.ShapeDtypeStruct((nb, N_EXPERTS, n), jnp.int32),
        ],
        scratch_shapes=[pltpu.VMEM((nch, N_EXPERTS, tc), F32)],
        compiler_params=_cparams(("arbitrary", "arbitrary")),
        name="route",
    )(x, gain.reshape(1, k), mod, mod, router_t, tri)


def _gather_kernel(c1_ref, slot_ref, gate_ref, h_ref, xs_ref, gs_ref, *, cap, n, split):
    onehot = lax.broadcasted_iota(jnp.int32, (cap, n), 0) == slot_ref[...]
    gs_ref[...] = jnp.sum(jnp.where(onehot, gate_ref[...], 0.0), axis=1, keepdims=True)
    p = jnp.where(onehot, 1.0, 0.0).astype(BF16)
    if not split:
        xs_ref[...] = _bdot(p, h_ref[...]).astype(BF16)
        return
    hs, ht = cap // 2, n // 2
    c1 = c1_ref[pl.program_id(0), pl.program_id(1)]

    @pl.when(c1 < hs)
    def _():
        xs_ref[0:hs, :] = (_bdot(p[0:hs, 0:ht], h_ref[0:ht, :]) + _bdot(p[0:hs, ht:n], h_ref[ht:n, :])).astype(BF16)
        xs_ref[hs:cap, :] = _bdot(p[hs:cap, ht:n], h_ref[ht:n, :]).astype(BF16)

    @pl.when(c1 >= hs)
    def _():
        xs_ref[0:hs, :] = _bdot(p[0:hs, 0:ht], h_ref[0:ht, :]).astype(BF16)
        xs_ref[hs:cap, :] = (_bdot(p[hs:cap, ht:n], h_ref[ht:n, :])
                             + _bdot(p[hs:cap, 0:ht], h_ref[0:ht, :])).astype(BF16)


def _gather(slot_t, gate_t, h, nb, n, cap):
    k = h.shape[1]
    split = cap // 2 >= LANES
    kern = functools.partial(_gather_kernel, cap=cap, n=n, split=split)
    c1 = jnp.sum((slot_t[:, :, :n // 2] >= 0).astype(jnp.int32), axis=-1)
    row_spec = pl.BlockSpec((None, None, 1, n), lambda b, e, c: (b, e, 0, 0))
    grid_spec = pltpu.PrefetchScalarGridSpec(
        num_scalar_prefetch=1,
        grid=(nb, N_EXPERTS),
        in_specs=[row_spec, row_spec, pl.BlockSpec((n, k), lambda b, e, c: (b, 0))],
        out_specs=[pl.BlockSpec((None, cap, k), lambda b, e, c: (e, b, 0)),
                   pl.BlockSpec((None, cap, 1), lambda b, e, c: (e, b, 0))],
    )
    return pl.pallas_call(
        kern,
        grid_spec=grid_spec,
        out_shape=[jax.ShapeDtypeStruct((N_EXPERTS, nb * cap, k), BF16),
                   jax.ShapeDtypeStruct((N_EXPERTS, nb * cap, 1), F32)],
        compiler_params=_cparams(("arbitrary", "arbitrary")),
        name="gather",
    )(c1, slot_t.reshape(nb, N_EXPERTS, 1, n), gate_t.reshape(nb, N_EXPERTS, 1, n), h)


def _ffn_kernel(*refs, nf, tf, groups, n_parts, n_experts):
    xs_hbm = refs[0]
    gs_refs = refs[n_parts:2 * n_parts]
    wg_ref, wu_ref, wd_ref = refs[2 * n_parts:2 * n_parts + 3]
    y_refs = refs[2 * n_parts + 3:3 * n_parts + 3]
    hid_scr, xs_buf, xs_sem = refs[3 * n_parts + 3:3 * n_parts + 6]
    xs_refs = (xs_buf,) + tuple(refs[1:n_parts])
    e = pl.program_id(0)
    s = pl.program_id(1)

    def xs_copy(expert):
        return pltpu.make_async_copy(xs_hbm.at[expert], xs_buf, xs_sem)

    @pl.when((e == 0) & (s == 0))
    def _():
        xs_copy(0).start()

    @pl.when(s == 0)
    def _():
        xs_copy(e).wait()

    @pl.when((s == nf) & (e + 1 < n_experts))
    def _():
        xs_copy(e + 1).start()

    @pl.when(s < nf)
    def _():
        wg = wg_ref[...].astype(BF16)
        wu = wu_ref[...].astype(BF16)
        for part, r0, nr, h0 in groups:
            xs = xs_refs[part][r0:r0 + nr, :]
            a = _bdot(xs, wg)
            u = _bdot(xs, wu)
            hid_scr[s, h0:h0 + nr, :] = (_silu(a) * u * gs_refs[part][r0:r0 + nr, :]).astype(BF16)

    @pl.when(s >= nf)
    def _():
        wds = [wd_ref[f * tf:(f + 1) * tf, :].astype(BF16) for f in range(nf)]
        for part, r0, nr, h0 in groups:
            acc = _bdot(hid_scr[0, h0:h0 + nr, :], wds[0])
            for f in range(1, nf):
                acc = acc + _bdot(hid_scr[f, h0:h0 + nr, :], wds[f])
            y_refs[part][r0:r0 + nr, :] = acc.astype(BF16)


def _expert_ffn(parts, wg, wu, wd):
    e, _, k = parts[0][0].shape
    ff = wg.shape[2]
    tf, td = 256, 512
    nf, nd = ff // tf, k // td
    groups, h0 = [], 0
    for pi, (xs, _) in enumerate(parts):
        m = xs.shape[1]
        rg = min(m, 1024)
        for r0 in range(0, m, rg):
            groups.append((pi, r0, rg, h0))
            h0 += rg
    kern = functools.partial(_ffn_kernel, nf=nf, tf=tf, groups=tuple(groups), n_parts=len(parts), n_experts=e)
    row_spec = lambda m, w, **kw: pl.BlockSpec((None, m, w), lambda ei, s: (ei, 0, 0), **kw)
    up_idx = lambda ei, s: (ei, 0, jnp.minimum(s, nf - 1))
    down_idx = lambda ei, s: (ei, 0, jnp.maximum(s - nf, 0))
    return pl.pallas_call(
        kern,
        grid=(e, nf + nd),
        in_specs=([pl.BlockSpec(memory_space=pl.ANY)]
                  + [row_spec(xs.shape[1], k, pipeline_mode=pl.Buffered(1)) for xs, _ in parts[1:]]
                  + [row_spec(xs.shape[1], 1) for xs, _ in parts]
                  + [pl.BlockSpec((None, k, tf), up_idx), pl.BlockSpec((None, k, tf), up_idx),
                     pl.BlockSpec((None, ff, td), down_idx)]),
        out_specs=[pl.BlockSpec((None, xs.shape[1], td), down_idx) for xs, _ in parts],
        out_shape=[jax.ShapeDtypeStruct(xs.shape, BF16) for xs, _ in parts],
        scratch_shapes=[pltpu.VMEM((nf, h0, tf), BF16), pltpu.VMEM(parts[0][0].shape[1:], BF16),
                        pltpu.SemaphoreType.DMA(())],
        compiler_params=_cparams(("arbitrary", "arbitrary")),
        name="expert_ffn",
    )(*[xs for xs, _ in parts], *[gs for _, gs in parts], wg, wu, wd)


def _combine_kernel(hs_ref, slot_ref, y_ref, x_ref, g2_ref, o_ref, acc_scr, *, tt, cap, cap_pad, eg, halves):
    b = pl.program_id(0)
    t = pl.program_id(1)
    e = pl.program_id(2)

    @pl.when(e == 0)
    def _():
        acc_scr[...] = jnp.zeros_like(acc_scr)

    def full_group():
        slot_iota = lax.broadcasted_iota(jnp.int32, (cap_pad, tt), 0)
        onehot_t = jnp.concatenate([jnp.where(slot_iota == slot_ref[i], 1.0, 0.0) for i in range(eg)], axis=0)
        onehot = onehot_t.T[:, 0:eg * cap].astype(BF16)
        acc_scr[...] += _bdot(onehot, y_ref[...].reshape(eg * cap, y_ref.shape[2]))

    if not halves:
        full_group()
    else:
        half = cap // 2
        s0 = hs_ref[b, t, eg * e]
        s1 = hs_ref[b, t, eg * e + 1]
        narrow = (s0 < 2) & (s1 < 2)

        @pl.when(narrow)
        def _():
            slot_iota = lax.broadcasted_iota(jnp.int32, (half, tt), 0)
            onehot_t = jnp.concatenate([jnp.where(slot_iota + s0 * half == slot_ref[0], 1.0, 0.0),
                                        jnp.where(slot_iota + s1 * half == slot_ref[1], 1.0, 0.0)], axis=0)
            y_sel = jnp.concatenate([y_ref[0, pl.ds(pl.multiple_of(s0 * half, half), half), :],
                                     y_ref[1, pl.ds(pl.multiple_of(s1 * half, half), half), :]], axis=0)
            acc_scr[...] += _bdot(onehot_t.T.astype(BF16), y_sel)

        @pl.when(jnp.logical_not(narrow))
        def _():
            full_group()

    @pl.when(e == N_EXPERTS // eg - 1)
    def _():
        o_ref[...] = x_ref[...] + g2_ref[...] * acc_scr[...]


def _combine(slot_t, y, x, mod, mod_row_fn, nb, n, cap):
    k = x.shape[1]
    tt = min(n, 512)
    ntb = n // tt
    cap_pad = max(cap, LANES)
    eg = 2 if cap_pad == cap else 1
    halves = eg == 2 and cap // 2 >= LANES
    cnt = jnp.sum((slot_t >= 0).astype(jnp.int32).reshape(nb, N_EXPERTS, ntb, tt), axis=-1)
    hi = jnp.cumsum(cnt, axis=-1)
    lo = hi - cnt
    half_state = jnp.where(hi <= cap // 2, 0, jnp.where(lo >= cap // 2, 1, 2)).astype(jnp.int32)
    kern = functools.partial(_combine_kernel, tt=tt, cap=cap, cap_pad=cap_pad, eg=eg, halves=halves)
    grid_spec = pltpu.PrefetchScalarGridSpec(
        num_scalar_prefetch=1,
        grid=(nb, ntb, N_EXPERTS // eg),
        in_specs=[
            pl.BlockSpec((None, eg, 1, tt), lambda b, t, e, hs: (b, e, 0, t)),
            pl.BlockSpec((eg, cap, k), lambda b, t, e, hs: (e, b, 0)),
            pl.BlockSpec((tt, k), lambda b, t, e, hs: (b * ntb + t, 0)),
            pl.BlockSpec((None, None, 1, k), lambda b, t, e, hs: (mod_row_fn(b), 5, 0, 0)),
        ],
        out_specs=pl.BlockSpec((tt, k), lambda b, t, e, hs: (b * ntb + t, 0)),
        scratch_shapes=[pltpu.VMEM((tt, k), F32)],
    )
    return pl.pallas_call(
        kern,
        grid_spec=grid_spec,
        out_shape=jax.ShapeDtypeStruct((nb * n, k), F32),
        compiler_params=_cparams(("arbitrary", "arbitrary", "arbitrary")),
        name="combine",
    )(jnp.swapaxes(half_state, 1, 2), slot_t.reshape(nb, N_EXPERTS, 1, n), y, x, mod)


def _ec_moe_residual(streams, gain, mod, router, wg, wu, wd, nb):
    routed = []
    for x, n, mod_row_fn in streams:
        cap = EC_CAPACITY_FACTOR * n // N_EXPERTS
        idx = jnp.arange(n, dtype=jnp.int32)
        tri = (idx[:, None] < idx[None, :]).astype(BF16)
        h, gate_t, slot_t = _route(x, gain, mod, mod_row_fn, router.T, tri, nb, n, cap)
        routed.append((slot_t, cap) + _gather(slot_t, gate_t, h, nb, n, cap))
    ys = _expert_ffn([(xs, gs) for _, _, xs, gs in routed], wg, wu, wd)
    return [_combine(slot_t, y, x, mod, mod_row_fn, nb, n, cap)
            for (x, n, mod_row_fn), (slot_t, cap, _, _), y in zip(streams, routed, ys)]


def _rope_tables(n, tm):
    rows = n // GRID_W
    row = jnp.repeat(jnp.arange(rows, dtype=F32), GRID_W)
    col = jnp.tile(jnp.arange(GRID_W, dtype=F32), rows)
    axis_dim = HEAD_DIM // 2
    inv_freq = ROPE_THETA ** (-jnp.arange(0, axis_dim, 2, dtype=F32) / axis_dim)
    ang = jnp.concatenate([row[:, None] * inv_freq, col[:, None] * inv_freq], axis=-1)
    cos, sin = jnp.cos(ang), jnp.sin(ang)
    cos_full = jnp.concatenate([cos, cos], axis=-1).reshape(n // tm, tm, HEAD_DIM)
    sin_signed = jnp.concatenate([-sin, sin], axis=-1).reshape(n // tm, tm, HEAD_DIM)
    return cos_full, sin_signed


def kernel(x, c, ctx, c_ctx,
           mod_w_0, mod_b_0, norm1_0, norm2_0, wqkv_0, wo_0, qnorm_0, knorm_0,
           lam_q1_0, lam_k1_0, lam_q2_0, lam_k2_0, subln_0, router_0, we_gate_0, we_up_0, we_down_0,
           mod_w_1, mod_b_1, norm1_1, norm2_1, wqkv_1, wo_1, qnorm_1, knorm_1,
           router_1, we_gate_1, we_up_1, we_down_1):
    nb, nq, d = x.shape
    nc = ctx.shape[1]
    tm = 1024
    x_lat = x.reshape(nb * nq, d)
    x_ctx = ctx.reshape(nb * nc, d)
    cond = jnp.zeros((MOD_ROWS, d), F32).at[:nb].set(c).at[nb].set(c_ctx)
    lat_row = lambda i: i // (nq // tm)
    ctx_row = lambda i: nb
    lat_b = lambda b: b
    cos, sin = _rope_tables(nq, tm)
    cos_id = jnp.ones((1, tm, HEAD_DIM), F32)
    sin_id = jnp.zeros((1, tm, HEAD_DIM), F32)
    q_scale = math.log2(math.e) / math.sqrt(HEAD_DIM)

    mod0 = _adaln(cond, mod_w_0, mod_b_0)
    n_heads2 = d // HEAD_DIM
    colgain0 = jnp.concatenate([jnp.tile(qnorm_0 * q_scale, n_heads2), jnp.tile(knorm_0, n_heads2),
                                jnp.ones((d,), F32)]).reshape(1, 3 * d)
    qkv_lat = _qkv_proj(x_lat, norm1_0, mod0, lat_row, wqkv_0, colgain0, cos, sin, 2 * d)
    qkv_ctx = _qkv_proj(x_ctx, norm1_0, mod0, ctx_row, wqkv_0, colgain0, cos_id, sin_id, 2 * d)
    lam_init = 0.8 - 0.6 * math.exp(-0.3 * 0)
    lam_params = jnp.stack([lam_q1_0, lam_k1_0, lam_q2_0, lam_k2_0])
    o_lat = _diff_attention(lam_params, subln_0, qkv_lat, [(qkv_lat, nq), (qkv_ctx, nc)], nb, nq, lam_init)
    o_ctx = _diff_attention(lam_params, subln_0, qkv_ctx, [(qkv_ctx, nc)], nb, nc, lam_init)
    x_lat = _proj_residual(o_lat, wo_0, x_lat, mod0, lat_row, 2)
    x_ctx = _proj_residual(o_ctx, wo_0, x_ctx, mod0, ctx_row, 2)
    x_lat, x_ctx = _ec_moe_residual([(x_lat, nq, lat_b), (x_ctx, nc, ctx_row)], norm2_0, mod0,
                                    router_0, we_gate_0, we_up_0, we_down_0, nb)

    mod1 = _adaln(cond, mod_w_1, mod_b_1)
    kv_dim = (wqkv_1.shape[1] - d) // 2
    colgain1 = jnp.concatenate([jnp.tile(qnorm_1 * q_scale, d // HEAD_DIM),
                                jnp.tile(knorm_1, kv_dim // HEAD_DIM),
                                jnp.ones((kv_dim,), F32)]).reshape(1, d + 2 * kv_dim)
    qkv_lat = _qkv_proj(x_lat, norm1_1, mod1, lat_row, wqkv_1, colgain1, cos, sin, d + kv_dim)
    kv_ctx = _qkv_proj(x_ctx, norm1_1, mod1, ctx_row, wqkv_1, colgain1, cos_id, sin_id, d + kv_dim, col_start=d)
    o_lat = _gqa_attention(qkv_lat, kv_ctx, nb, nq, nc)
    x_lat = _proj_residual(o_lat, wo_1, x_lat, mod1, lat_row, 2)
    (x_lat,) = _ec_moe_residual([(x_lat, nq, lat_b)], norm2_1, mod1,
                                router_1, we_gate_1, we_up_1, we_down_1, nb)
    return x_lat.reshape(nb, nq, d)
```

```python
import functools
import math

import jax
import jax.numpy as jnp
from jax import lax
from jax.experimental import pallas as pl
from jax.experimental.pallas import tpu as pltpu

F32 = jnp.float32
BF16 = jnp.bfloat16

D_MODEL = 4096
GRID_W = 64
HEAD_DIM = 128
N_EXPERTS = 16
EC_CAPACITY_FACTOR = 2
EXPERT_FF = D_MODEL // 4
ROPE_THETA = 10000.0
NORM_EPS = 1e-6
N_MOD = 6
MOD_ROWS = 16
LANES = 128
BF16_ROWS = 16
VMEM_LIMIT = 58 * 1024 * 1024


def _cparams(sem):
    return pltpu.CompilerParams(dimension_semantics=sem, vmem_limit_bytes=VMEM_LIMIT)


def _bdot(a, b):
    return jnp.dot(a, b, preferred_element_type=F32)


def _dot_nt(a, b):
    return lax.dot_general(a, b, (((1,), (1,)), ((), ())), preferred_element_type=F32)


def _silu(x):
    return x * (1.0 / (1.0 + jnp.exp(-x)))


def _split_bf16(x):
    hi = x.astype(BF16)
    lo = (x - hi.astype(F32)).astype(BF16)
    return hi, lo


def _adaln_kernel(c_ref, w_ref, b_ref, o_ref):
    s = _silu(c_ref[...]).astype(BF16)
    o_ref[...] = _bdot(s, w_ref[...].astype(BF16)) + b_ref[...]


def _adaln(cond, w, b):
    n = w.shape[1]
    tn = 512
    out = pl.pallas_call(
        _adaln_kernel,
        grid=(n // tn,),
        in_specs=[
            pl.BlockSpec((MOD_ROWS, D_MODEL), lambda j: (0, 0)),
            pl.BlockSpec((D_MODEL, tn), lambda j: (0, j)),
            pl.BlockSpec((1, tn), lambda j: (0, j)),
        ],
        out_specs=pl.BlockSpec((MOD_ROWS, tn), lambda j: (0, j)),
        out_shape=jax.ShapeDtypeStruct((MOD_ROWS, n), F32),
        compiler_params=_cparams(("arbitrary",)),
        name="adaln",
    )(cond, w, b.reshape(1, n))
    return out.reshape(MOD_ROWS, N_MOD, 1, D_MODEL)


def _norm_mod(xv, g, shift, scale):
    ms = jnp.mean(xv * xv, axis=-1, keepdims=True)
    y = xv * lax.rsqrt(ms + NORM_EPS) * g
    return y * (1.0 + scale) + shift


def _qkv_kernel(x_hbm, g_ref, sh_ref, sc_ref, w_ref, cg_ref, cos_ref, sin_ref, o_ref, h_scr, x_ref, x_sem,
                *, tm, tn, n_qk_tiles, chunk, row_group, n_row_blocks):
    i = pl.program_id(0)
    j = pl.program_id(1)

    def x_copy(blk):
        return pltpu.make_async_copy(x_hbm.at[pl.ds(pl.multiple_of(blk * tm, tm), tm)], x_ref, x_sem)

    @pl.when((i == 0) & (j == 0))
    def _():
        x_copy(0).start()

    @pl.when(j == 0)
    def _():
        x_copy(i).wait()

    @pl.when((j == 1) & (i + 1 < n_row_blocks))
    def _():
        x_copy(i + 1).start()

    def qk_tile(make_h):
        w = w_ref[...].astype(BF16)
        for r in range(tm // row_group):
            rows = slice(r * row_group, (r + 1) * row_group)
            if make_h:
                for c in range(row_group // chunk):
                    rr = slice(r * row_group + c * chunk, r * row_group + (c + 1) * chunk)
                    h = _norm_mod(x_ref[rr, :], g_ref[...], sh_ref[...], sc_ref[...])
                    h_scr[rr, :] = h.astype(BF16)
            acc = _bdot(h_scr[rows, :], w)
            cos = cos_ref[rows, :]
            sin = sin_ref[rows, :]
            for gi in range(tn // HEAD_DIM):
                cols = slice(gi * HEAD_DIM, (gi + 1) * HEAD_DIM)
                a = acc[:, cols]
                ms = jnp.mean(a * a, axis=-1, keepdims=True)
                a = a * lax.rsqrt(ms + NORM_EPS) * cg_ref[:, cols]
                a = a * cos + pltpu.roll(a, HEAD_DIM // 2, axis=1) * sin
                o_ref[rows, cols] = a.astype(BF16)

    @pl.when(j == 0)
    def _():
        qk_tile(True)

    @pl.when((j > 0) & (j < n_qk_tiles))
    def _():
        qk_tile(False)

    @pl.when(j >= n_qk_tiles)
    def _():
        o_ref[...] = _bdot(h_scr[...], w_ref[...].astype(BF16)).astype(BF16)


def _qkv_proj(x, gain, mod, mod_row_fn, w, colgain, cos, sin, n_qk_cols, col_start=0):
    r, k = x.shape
    tm, tn = 1024, 512
    n = w.shape[1] - col_start
    j0 = col_start // tn
    npos = cos.shape[0]
    assert n // tn >= 2
    kern = functools.partial(_qkv_kernel, tm=tm, tn=tn, n_qk_tiles=(n_qk_cols - col_start) // tn, chunk=32,
                             row_group=256, n_row_blocks=r // tm)
    return pl.pallas_call(
        kern,
        grid=(r // tm, n // tn),
        in_specs=[
            pl.BlockSpec(memory_space=pl.ANY),
            pl.BlockSpec((1, k), lambda i, j: (0, 0)),
            pl.BlockSpec((None, None, 1, k), lambda i, j: (mod_row_fn(i), 0, 0, 0)),
            pl.BlockSpec((None, None, 1, k), lambda i, j: (mod_row_fn(i), 1, 0, 0)),
            pl.BlockSpec((k, tn), lambda i, j: (0, j0 + j)),
            pl.BlockSpec((1, tn), lambda i, j: (0, j0 + j)),
            pl.BlockSpec((None, tm, HEAD_DIM), lambda i, j: (i % npos, 0, 0)),
            pl.BlockSpec((None, tm, HEAD_DIM), lambda i, j: (i % npos, 0, 0)),
        ],
        out_specs=pl.BlockSpec((tm, tn), lambda i, j: (i, j)),
        out_shape=jax.ShapeDtypeStruct((r, n), BF16),
        scratch_shapes=[pltpu.VMEM((tm, k), BF16), pltpu.VMEM((tm, k), F32), pltpu.SemaphoreType.DMA(())],
        compiler_params=_cparams(("arbitrary", "arbitrary")),
        name="qkv_proj",
    )(x, gain.reshape(1, k), mod, mod, w, colgain, cos, sin)


def _attend(q, kvs, row_block=16, ones_cols=0):
    tq = q.shape[0]
    ss = [_dot_nt(q, k) for k, _ in kvs]
    p_blocks = [[] for _ in kvs]
    l_blocks = []
    for r in range(tq // row_block):
        rows = slice(r * row_block, (r + 1) * row_block)
        sb = [s[rows, :] for s in ss]
        m = functools.reduce(jnp.maximum, [jnp.max(x, axis=-1, keepdims=True) for x in sb])
        pb = [jnp.exp2(x - m) for x in sb]
        if not ones_cols:
            l_blocks.append(functools.reduce(lambda a, b: a + b,
                                             [jnp.sum(p, axis=-1, keepdims=True) for p in pb]))
        for blocks, p in zip(p_blocks, pb):
            blocks.append(p.astype(BF16))
    o = functools.reduce(lambda a, b: a + b,
                         [_bdot(jnp.concatenate(blocks, axis=0), v) for blocks, (_, v) in zip(p_blocks, kvs)])
    if ones_cols:
        dv = o.shape[1] - ones_cols
        return o[:, 0:dv] / o[:, dv:dv + 1]
    return o / jnp.concatenate(l_blocks, axis=0)


def _diff_lambda(lam_ref, lam_init):
    lp = lam_ref[...]
    s1 = jnp.sum(lp[0:1, :] * lp[1:2, :], axis=-1, keepdims=True)
    s2 = jnp.sum(lp[2:3, :] * lp[3:4, :], axis=-1, keepdims=True)
    return jnp.exp(s1) - jnp.exp(s2) + lam_init


def _diff_attn_kernel(*refs, lam_init, n_kv, sub_rows=256):
    lam_ref, subln_ref, q_ref = refs[:3]
    k_refs = refs[3:3 + n_kv]
    v_refs = refs[3 + n_kv:3 + 2 * n_kv]
    o_ref = refs[3 + 2 * n_kv]
    lam = _diff_lambda(lam_ref, lam_init)
    tq = q_ref.shape[0]
    sub = min(tq, sub_rows)
    for qs in range(tq // sub):
        rows = slice(qs * sub, (qs + 1) * sub)
        outs = []
        for t in range(2):
            cols = slice(t * HEAD_DIM, (t + 1) * HEAD_DIM)
            kvs = [(kr[:, cols], vr[...]) for kr, vr in zip(k_refs, v_refs)]
            outs.append(_attend(q_ref[rows, cols], kvs))
        o = outs[0] - lam * outs[1]
        ms = jnp.mean(o * o, axis=-1, keepdims=True)
        o = o * lax.rsqrt(ms + NORM_EPS) * subln_ref[...] * (1.0 - lam_init)
        o_ref[rows, :] = o.astype(BF16)


def _diff_attention(lam_params, subln, qkv_q, kv_sources, nb, nq, lam_init):
    heads = D_MODEL // (2 * HEAD_DIM)
    hw = 2 * HEAD_DIM
    tq = min(nq, 2048)
    nqb = nq // tq
    n_kv = len(kv_sources)
    in_specs = [
        pl.BlockSpec((4, HEAD_DIM), lambda b, h, i: (0, 0)),
        pl.BlockSpec((1, hw), lambda b, h, i: (0, 0)),
        pl.BlockSpec((tq, hw), lambda b, h, i: (b * nqb + i, h)),
    ]
    args = [lam_params, subln.reshape(1, hw), qkv_q]
    for arr, nk in kv_sources:
        in_specs.append(pl.BlockSpec((nk, hw), lambda b, h, i: (b, heads + h)))
        args.append(arr)
    for arr, nk in kv_sources:
        in_specs.append(pl.BlockSpec((nk, hw), lambda b, h, i: (b, 2 * heads + h)))
        args.append(arr)
    kern = functools.partial(_diff_attn_kernel, lam_init=lam_init, n_kv=n_kv)
    return pl.pallas_call(
        kern,
        grid=(nb, heads, nqb),
        in_specs=in_specs,
        out_specs=pl.BlockSpec((tq, hw), lambda b, h, i: (b * nqb + i, h)),
        out_shape=jax.ShapeDtypeStruct((nb * nq, D_MODEL), BF16),
        compiler_params=_cparams(("arbitrary", "arbitrary", "arbitrary")),
        name="diff_attn",
    )(*args)


def _gqa_attn_kernel(q_ref, kl_ref, kc_ref, vl_ref, vc_ref, o_ref, *, group, sub_rows=256):
    def with_ones(v):
        return jnp.concatenate([v, jnp.ones(v.shape, v.dtype)], axis=1)
    kvs = [(kl_ref[...], with_ones(vl_ref[...])), (kc_ref[...], with_ones(vc_ref[...]))]
    tq = q_ref.shape[0]
    sub = min(tq, sub_rows)
    for qs in range(tq // sub):
        rows = slice(qs * sub, (qs + 1) * sub)
        for g in range(group):
            cols = slice(g * HEAD_DIM, (g + 1) * HEAD_DIM)
            o_ref[rows, cols] = _attend(q_ref[rows, cols], kvs, ones_cols=HEAD_DIM).astype(BF16)


def _gqa_attention(qkv_lat, kv_ctx, nb, nq, nc):
    q_heads = D_MODEL // HEAD_DIM
    kv_heads = q_heads // 4
    group = q_heads // kv_heads
    gw = group * HEAD_DIM
    tq = 1024
    nqb = nq // tq
    k_off = D_MODEL // HEAD_DIM
    v_off = k_off + kv_heads
    kern = functools.partial(_gqa_attn_kernel, group=group)
    return pl.pallas_call(
        kern,
        grid=(nb, kv_heads, nqb),
        in_specs=[
            pl.BlockSpec((tq, gw), lambda b, h, i: (b * nqb + i, h)),
            pl.BlockSpec((nq, HEAD_DIM), lambda b, h, i: (b, k_off + h)),
            pl.BlockSpec((nc, HEAD_DIM), lambda b, h, i: (b, h)),
            pl.BlockSpec((nq, HEAD_DIM), lambda b, h, i: (b, v_off + h)),
            pl.BlockSpec((nc, HEAD_DIM), lambda b, h, i: (b, kv_heads + h)),
        ],
        out_specs=pl.BlockSpec((tq, gw), lambda b, h, i: (b * nqb + i, h)),
        out_shape=jax.ShapeDtypeStruct((nb * nq, D_MODEL), BF16),
        compiler_params=_cparams(("arbitrary", "arbitrary", "arbitrary")),
        name="gqa_attn",
    )(qkv_lat, qkv_lat, kv_ctx, qkv_lat, kv_ctx)


def _proj_res_kernel(o_ref, w_ref, res_ref, gate_ref, out_ref):
    acc = _bdot(o_ref[...], w_ref[...].astype(BF16))
    out_ref[...] = res_ref[...] + gate_ref[...] * acc


def _proj_residual(o, w, res, mod, mod_row_fn, gate_idx):
    r, k = o.shape
    n = w.shape[1]
    tm, tn = 1024, 512
    return pl.pallas_call(
        _proj_res_kernel,
        grid=(r // tm, n // tn),
        in_specs=[
            pl.BlockSpec((tm, k), lambda i, j: (i, 0)),
            pl.BlockSpec((k, tn), lambda i, j: (0, j)),
            pl.BlockSpec((tm, tn), lambda i, j: (i, j)),
            pl.BlockSpec((None, None, 1, tn), lambda i, j: (mod_row_fn(i), gate_idx, 0, j)),
        ],
        out_specs=pl.BlockSpec((tm, tn), lambda i, j: (i, j)),
        out_shape=jax.ShapeDtypeStruct((r, n), F32),
        compiler_params=_cparams(("arbitrary", "arbitrary")),
        name="out_proj",
    )(o, w, res, mod)


def _route_kernel(x_ref, g_ref, sh_ref, sc_ref, rt_ref, tri_ref, h_ref, gate_ref, slot_ref, lg_scr,
                  *, tc, n, cap):
    c = pl.program_id(1)
    h = _norm_mod(x_ref[...], g_ref[...], sh_ref[...], sc_ref[...])
    h_ref[...] = h.astype(BF16)
    h_hi, h_lo = _split_bf16(h)
    r_hi, r_lo = _split_bf16(rt_ref[...])
    lg_scr[c] = _dot_nt(r_hi, h_hi) + _dot_nt(r_hi, h_lo) + _dot_nt(r_lo, h_hi)

    @pl.when(c == n // tc - 1)
    def _():
        lg = jnp.concatenate([lg_scr[i] for i in range(n // tc)], axis=1)
        mx = jnp.max(lg, axis=0, keepdims=True)
        ex = jnp.exp(lg - mx)
        aff = ex / jnp.sum(ex, axis=0, keepdims=True)
        bits = pltpu.bitcast(aff, jnp.int32)
        thr = jnp.zeros((N_EXPERTS, 1), jnp.int32)
        for bit in range(30, -1, -1):
            cand = thr | jnp.int32(1 << bit)
            cnt = jnp.sum(jnp.where(bits >= cand, 1.0, 0.0), axis=1, keepdims=True)
            thr = jnp.where(cnt >= cap, cand, thr)
        gt = bits > thr
        eq = bits == thr
        need = cap - jnp.sum(jnp.where(gt, 1.0, 0.0), axis=1, keepdims=True)
        tri = tri_ref[...]
        eq_rank = _bdot(jnp.where(eq, 1.0, 0.0).astype(BF16), tri)
        sel = gt | (eq & (eq_rank < need))
        slot = _bdot(jnp.where(sel, 1.0, 0.0).astype(BF16), tri)
        gate_ref[...] = jnp.where(sel, aff, 0.0)
        slot_ref[...] = jnp.where(sel, slot.astype(jnp.int32), -1)


def _route(x, gain, mod, mod_row_fn, router_t, tri, nb, n, cap):
    k = x.shape[1]
    tc = 256
    nch = n // tc
    kern = functools.partial(_route_kernel, tc=tc, n=n, cap=cap)
    return pl.pallas_call(
        kern,
        grid=(nb, nch),
        in_specs=[
            pl.BlockSpec((tc, k), lambda b, c: (b * nch + c, 0)),
            pl.BlockSpec((1, k), lambda b, c: (0, 0)),
            pl.BlockSpec((None, None, 1, k), lambda b, c: (mod_row_fn(b), 3, 0, 0)),
            pl.BlockSpec((None, None, 1, k), lambda b, c: (mod_row_fn(b), 4, 0, 0)),
            pl.BlockSpec((N_EXPERTS, k), lambda b, c: (0, 0)),
            pl.BlockSpec((n, n), lambda b, c: (0, 0)),
        ],
        out_specs=[
            pl.BlockSpec((tc, k), lambda b, c: (b * nch + c, 0)),
            pl.BlockSpec((None, N_EXPERTS, n), lambda b, c: (b, 0, 0)),
            pl.BlockSpec((None, N_EXPERTS, n), lambda b, c: (b, 0, 0)),
        ],
        out_shape=[
            jax.ShapeDtypeStruct((nb * n, k), BF16),
            jax.ShapeDtypeStruct((nb, N_EXPERTS, n), F32),
            jax.ShapeDtypeStruct((nb, N_EXPERTS, n), jnp.int32),
        ],
        scratch_shapes=[pltpu.VMEM((nch, N_EXPERTS, tc), F32)],
        compiler_params=_cparams(("arbitrary", "arbitrary")),
        name="route",
    )(x, gain.reshape(1, k), mod, mod, router_t, tri)


def _gather_kernel(c1_ref, slot_ref, gate_ref, h_ref, xs_ref, gs_ref, *, cap, n, split, win):
    onehot = lax.broadcasted_iota(jnp.int32, (cap, n), 0) == slot_ref[...]
    gs_ref[...] = jnp.sum(jnp.where(onehot, gate_ref[...], 0.0), axis=1, keepdims=True)
    p = jnp.where(onehot, 1.0, 0.0).astype(BF16)
    if not split:
        xs_ref[...] = _bdot(p, h_ref[...]).astype(BF16)
        return
    hs, ht = cap // 2, n // 2
    mode = c1_ref[pl.program_id(0), pl.program_id(1)]

    @pl.when(mode == 2)
    def _():
        xs_ref[0:hs, :] = _bdot(p[0:hs, 0:win], h_ref[0:win, :]).astype(BF16)
        xs_ref[hs:cap, :] = _bdot(p[hs:cap, n - win:n], h_ref[n - win:n, :]).astype(BF16)

    @pl.when(mode == 0)
    def _():
        xs_ref[0:hs, :] = (_bdot(p[0:hs, 0:ht], h_ref[0:ht, :]) + _bdot(p[0:hs, ht:n], h_ref[ht:n, :])).astype(BF16)
        xs_ref[hs:cap, :] = _bdot(p[hs:cap, ht:n], h_ref[ht:n, :]).astype(BF16)

    @pl.when(mode == 1)
    def _():
        xs_ref[0:hs, :] = _bdot(p[0:hs, 0:ht], h_ref[0:ht, :]).astype(BF16)
        xs_ref[hs:cap, :] = (_bdot(p[hs:cap, ht:n], h_ref[ht:n, :])
                             + _bdot(p[hs:cap, 0:ht], h_ref[0:ht, :])).astype(BF16)


def _gather(slot_t, gate_t, h, nb, n, cap):
    k = h.shape[1]
    split = cap // 2 >= LANES
    win = 5 * n // 8
    kern = functools.partial(_gather_kernel, cap=cap, n=n, split=split, win=win)
    picked = (slot_t >= 0).astype(jnp.int32)

    def picks_before(t):
        return jnp.sum(picked[:, :, :t], axis=-1)
    windows_fit = (picks_before(win) >= cap // 2) & (picks_before(n - win) <= cap // 2)
    c1 = jnp.where(windows_fit, 2, jnp.where(picks_before(n // 2) < cap // 2, 0, 1)).astype(jnp.int32)
    row_spec = pl.BlockSpec((None, None, 1, n), lambda b, e, c: (b, e, 0, 0))
    grid_spec = pltpu.PrefetchScalarGridSpec(
        num_scalar_prefetch=1,
        grid=(nb, N_EXPERTS),
        in_specs=[row_spec, row_spec, pl.BlockSpec((n, k), lambda b, e, c: (b, 0))],
        out_specs=[pl.BlockSpec((None, cap, k), lambda b, e, c: (e, b, 0)),
                   pl.BlockSpec((None, cap, 1), lambda b, e, c: (e, b, 0))],
    )
    return pl.pallas_call(
        kern,
        grid_spec=grid_spec,
        out_shape=[jax.ShapeDtypeStruct((N_EXPERTS, nb * cap, k), BF16),
                   jax.ShapeDtypeStruct((N_EXPERTS, nb * cap, 1), F32)],
        compiler_params=_cparams(("arbitrary", "arbitrary")),
        name="gather",
    )(c1, slot_t.reshape(nb, N_EXPERTS, 1, n), gate_t.reshape(nb, N_EXPERTS, 1, n), h)


def _ffn_kernel(*refs, nf, tf, groups, n_parts, n_experts):
    xs_hbm = refs[0]
    gs_refs = refs[n_parts:2 * n_parts]
    wg_ref, wu_ref, wd_ref = refs[2 * n_parts:2 * n_parts + 3]
    y_refs = refs[2 * n_parts + 3:3 * n_parts + 3]
    hid_scr, xs_buf, xs_sem = refs[3 * n_parts + 3:3 * n_parts + 6]
    xs_refs = (xs_buf,) + tuple(refs[1:n_parts])
    e = pl.program_id(0)
    s = pl.program_id(1)

    def xs_copy(expert):
        return pltpu.make_async_copy(xs_hbm.at[expert], xs_buf, xs_sem)

    @pl.when((e == 0) & (s == 0))
    def _():
        xs_copy(0).start()

    @pl.when(s == 0)
    def _():
        xs_copy(e).wait()

    @pl.when((s == nf) & (e + 1 < n_experts))
    def _():
        xs_copy(e + 1).start()

    @pl.when(s < nf)
    def _():
        wg = wg_ref[...].astype(BF16)
        wu = wu_ref[...].astype(BF16)
        for part, r0, nr, h0 in groups:
            xs = xs_refs[part][r0:r0 + nr, :]
            a = _bdot(xs, wg)
            u = _bdot(xs, wu)
            hid_scr[s, h0:h0 + nr, :] = (_silu(a) * u * gs_refs[part][r0:r0 + nr, :]).astype(BF16)

    @pl.when(s >= nf)
    def _():
        wds = [wd_ref[f * tf:(f + 1) * tf, :].astype(BF16) for f in range(nf)]
        for part, r0, nr, h0 in groups:
            acc = _bdot(hid_scr[0, h0:h0 + nr, :], wds[0])
            for f in range(1, nf):
                acc = acc + _bdot(hid_scr[f, h0:h0 + nr, :], wds[f])
            y_refs[part][r0:r0 + nr, :] = acc.astype(BF16)


def _expert_ffn(parts, wg, wu, wd):
    e, _, k = parts[0][0].shape
    ff = wg.shape[2]
    tf, td = 256, 512
    nf, nd = ff // tf, k // td
    groups, h0 = [], 0
    for pi, (xs, _) in enumerate(parts):
        m = xs.shape[1]
        rg = min(m, 1024)
        for r0 in range(0, m, rg):
            groups.append((pi, r0, rg, h0))
            h0 += rg
    kern = functools.partial(_ffn_kernel, nf=nf, tf=tf, groups=tuple(groups), n_parts=len(parts), n_experts=e)
    row_spec = lambda m, w, **kw: pl.BlockSpec((None, m, w), lambda ei, s: (ei, 0, 0), **kw)
    up_idx = lambda ei, s: (ei, 0, jnp.minimum(s, nf - 1))
    down_idx = lambda ei, s: (ei, 0, jnp.maximum(s - nf, 0))
    return pl.pallas_call(
        kern,
        grid=(e, nf + nd),
        in_specs=([pl.BlockSpec(memory_space=pl.ANY)]
                  + [row_spec(xs.shape[1], k, pipeline_mode=pl.Buffered(1)) for xs, _ in parts[1:]]
                  + [row_spec(xs.shape[1], 1) for xs, _ in parts]
                  + [pl.BlockSpec((None, k, tf), up_idx), pl.BlockSpec((None, k, tf), up_idx),
                     pl.BlockSpec((None, ff, td), down_idx)]),
        out_specs=[pl.BlockSpec((None, xs.shape[1], td), down_idx) for xs, _ in parts],
        out_shape=[jax.ShapeDtypeStruct(xs.shape, BF16) for xs, _ in parts],
        scratch_shapes=[pltpu.VMEM((nf, h0, tf), BF16), pltpu.VMEM(parts[0][0].shape[1:], BF16),
                        pltpu.SemaphoreType.DMA(())],
        compiler_params=_cparams(("arbitrary", "arbitrary")),
        name="expert_ffn",
    )(*[xs for xs, _ in parts], *[gs for _, gs in parts], wg, wu, wd)


def _combine_kernel(hs_ref, slot_ref, y_ref, x_ref, g2_ref, o_ref, acc_scr, *, tt, cap, cap_pad, eg, halves):
    b = pl.program_id(0)
    t = pl.program_id(1)
    e = pl.program_id(2)

    @pl.when(e == 0)
    def _():
        acc_scr[...] = jnp.zeros_like(acc_scr)

    def full_group():
        slot_iota = lax.broadcasted_iota(jnp.int32, (cap_pad, tt), 0)
        onehot_t = jnp.concatenate([jnp.where(slot_iota == slot_ref[i], 1.0, 0.0) for i in range(eg)], axis=0)
        onehot = onehot_t.T[:, 0:eg * cap].astype(BF16)
        acc_scr[...] += _bdot(onehot, y_ref[...].reshape(eg * cap, y_ref.shape[2]))

    if not halves:
        full_group()
    else:
        half = cap // 2
        s0 = hs_ref[b, t, eg * e]
        s1 = hs_ref[b, t, eg * e + 1]
        narrow = (s0 >= 0) & (s1 >= 0)

        @pl.when(narrow)
        def _():
            slot_iota = lax.broadcasted_iota(jnp.int32, (half, tt), 0)
            onehot_t = jnp.concatenate([jnp.where(slot_iota + s0 == slot_ref[0], 1.0, 0.0),
                                        jnp.where(slot_iota + s1 == slot_ref[1], 1.0, 0.0)], axis=0)
            y_sel = jnp.concatenate([y_ref[0, pl.ds(pl.multiple_of(s0, BF16_ROWS), half), :],
                                     y_ref[1, pl.ds(pl.multiple_of(s1, BF16_ROWS), half), :]], axis=0)
            acc_scr[...] += _bdot(onehot_t.T.astype(BF16), y_sel)

        @pl.when(jnp.logical_not(narrow))
        def _():
            full_group()

    @pl.when(e == N_EXPERTS // eg - 1)
    def _():
        o_ref[...] = x_ref[...] + g2_ref[...] * acc_scr[...]


def _combine(slot_t, y, x, mod, mod_row_fn, nb, n, cap):
    k = x.shape[1]
    tt = min(n, 512)
    ntb = n // tt
    cap_pad = max(cap, LANES)
    eg = 2 if cap_pad == cap else 1
    halves = eg == 2 and cap // 2 >= LANES
    cnt = jnp.sum((slot_t >= 0).astype(jnp.int32).reshape(nb, N_EXPERTS, ntb, tt), axis=-1)
    hi = jnp.cumsum(cnt, axis=-1)
    lo = hi - cnt
    win0 = jnp.clip(lo // BF16_ROWS * BF16_ROWS, 0, cap - cap // 2)
    half_state = jnp.where(hi - win0 <= cap // 2, win0, -1).astype(jnp.int32)
    kern = functools.partial(_combine_kernel, tt=tt, cap=cap, cap_pad=cap_pad, eg=eg, halves=halves)
    grid_spec = pltpu.PrefetchScalarGridSpec(
        num_scalar_prefetch=1,
        grid=(nb, ntb, N_EXPERTS // eg),
        in_specs=[
            pl.BlockSpec((None, eg, 1, tt), lambda b, t, e, hs: (b, e, 0, t)),
            pl.BlockSpec((eg, cap, k), lambda b, t, e, hs: (e, b, 0)),
            pl.BlockSpec((tt, k), lambda b, t, e, hs: (b * ntb + t, 0)),
            pl.BlockSpec((None, None, 1, k), lambda b, t, e, hs: (mod_row_fn(b), 5, 0, 0)),
        ],
        out_specs=pl.BlockSpec((tt, k), lambda b, t, e, hs: (b * ntb + t, 0)),
        scratch_shapes=[pltpu.VMEM((tt, k), F32)],
    )
    return pl.pallas_call(
        kern,
        grid_spec=grid_spec,
        out_shape=jax.ShapeDtypeStruct((nb * n, k), F32),
        compiler_params=_cparams(("arbitrary", "arbitrary", "arbitrary")),
        name="combine",
    )(jnp.swapaxes(half_state, 1, 2), slot_t.reshape(nb, N_EXPERTS, 1, n), y, x, mod)


def _ec_moe_residual(streams, gain, mod, router, wg, wu, wd, nb):
    routed = []
    for x, n, mod_row_fn in streams:
        cap = EC_CAPACITY_FACTOR * n // N_EXPERTS
        idx = jnp.arange(n, dtype=jnp.int32)
        tri = (idx[:, None] < idx[None, :]).astype(BF16)
        h, gate_t, slot_t = _route(x, gain, mod, mod_row_fn, router.T, tri, nb, n, cap)
        routed.append((slot_t, cap) + _gather(slot_t, gate_t, h, nb, n, cap))
    ys = _expert_ffn([(xs, gs) for _, _, xs, gs in routed], wg, wu, wd)
    return [_combine(slot_t, y, x, mod, mod_row_fn, nb, n, cap)
            for (x, n, mod_row_fn), (slot_t, cap, _, _), y in zip(streams, routed, ys)]


def _rope_tables(n, tm):
    rows = n // GRID_W
    row = jnp.repeat(jnp.arange(rows, dtype=F32), GRID_W)
    col = jnp.tile(jnp.arange(GRID_W, dtype=F32), rows)
    axis_dim = HEAD_DIM // 2
    inv_freq = ROPE_THETA ** (-jnp.arange(0, axis_dim, 2, dtype=F32) / axis_dim)
    ang = jnp.concatenate([row[:, None] * inv_freq, col[:, None] * inv_freq], axis=-1)
    cos, sin = jnp.cos(ang), jnp.sin(ang)
    cos_full = jnp.concatenate([cos, cos], axis=-1).reshape(n // tm, tm, HEAD_DIM)
    sin_signed = jnp.concatenate([-sin, sin], axis=-1).reshape(n // tm, tm, HEAD_DIM)
    return cos_full, sin_signed


def kernel(x, c, ctx, c_ctx,
           mod_w_0, mod_b_0, norm1_0, norm2_0, wqkv_0, wo_0, qnorm_0, knorm_0,
           lam_q1_0, lam_k1_0, lam_q2_0, lam_k2_0, subln_0, router_0, we_gate_0, we_up_0, we_down_0,
           mod_w_1, mod_b_1, norm1_1, norm2_1, wqkv_1, wo_1, qnorm_1, knorm_1,
           router_1, we_gate_1, we_up_1, we_down_1):
    nb, nq, d = x.shape
    nc = ctx.shape[1]
    tm = 1024
    x_lat = x.reshape(nb * nq, d)
    x_ctx = ctx.reshape(nb * nc, d)
    cond = jnp.zeros((MOD_ROWS, d), F32).at[:nb].set(c).at[nb].set(c_ctx)
    lat_row = lambda i: i // (nq // tm)
    ctx_row = lambda i: nb
    lat_b = lambda b: b
    cos, sin = _rope_tables(nq, tm)
    cos_id = jnp.ones((1, tm, HEAD_DIM), F32)
    sin_id = jnp.zeros((1, tm, HEAD_DIM), F32)
    q_scale = math.log2(math.e) / math.sqrt(HEAD_DIM)

    mod0 = _adaln(cond, mod_w_0, mod_b_0)
    n_heads2 = d // HEAD_DIM
    colgain0 = jnp.concatenate([jnp.tile(qnorm_0 * q_scale, n_heads2), jnp.tile(knorm_0, n_heads2),
                                jnp.ones((d,), F32)]).reshape(1, 3 * d)
    qkv_lat = _qkv_proj(x_lat, norm1_0, mod0, lat_row, wqkv_0, colgain0, cos, sin, 2 * d)
    qkv_ctx = _qkv_proj(x_ctx, norm1_0, mod0, ctx_row, wqkv_0, colgain0, cos_id, sin_id, 2 * d)
    lam_init = 0.8 - 0.6 * math.exp(-0.3 * 0)
    lam_params = jnp.stack([lam_q1_0, lam_k1_0, lam_q2_0, lam_k2_0])
    o_lat = _diff_attention(lam_params, subln_0, qkv_lat, [(qkv_lat, nq), (qkv_ctx, nc)], nb, nq, lam_init)
    o_ctx = _diff_attention(lam_params, subln_0, qkv_ctx, [(qkv_ctx, nc)], nb, nc, lam_init)
    x_lat = _proj_residual(o_lat, wo_0, x_lat, mod0, lat_row, 2)
    x_ctx = _proj_residual(o_ctx, wo_0, x_ctx, mod0, ctx_row, 2)
    x_lat, x_ctx = _ec_moe_residual([(x_lat, nq, lat_b), (x_ctx, nc, ctx_row)], norm2_0, mod0,
                                    router_0, we_gate_0, we_up_0, we_down_0, nb)

    mod1 = _adaln(cond, mod_w_1, mod_b_1)
    kv_dim = (wqkv_1.shape[1] - d) // 2
    colgain1 = jnp.concatenate([jnp.tile(qnorm_1 * q_scale, d // HEAD_DIM),
                                jnp.tile(knorm_1, kv_dim // HEAD_DIM),
                                jnp.ones((kv_dim,), F32)]).reshape(1, d + 2 * kv_dim)
    qkv_lat = _qkv_proj(x_lat, norm1_1, mod1, lat_row, wqkv_1, colgain1, cos, sin, d + kv_dim)
    kv_ctx = _qkv_proj(x_ctx, norm1_1, mod1, ctx_row, wqkv_1, colgain1, cos_id, sin_id, d + kv_dim, col_start=d)
    o_lat = _gqa_attention(qkv_lat, kv_ctx, nb, nq, nc)
    x_lat = _proj_residual(o_lat, wo_1, x_lat, mod1, lat_row, 2)
    (x_lat,) = _ec_moe_residual([(x_lat, nq, lat_b)], norm2_1, mod1,
                                router_1, we_gate_1, we_up_1, we_down_1, nb)
    return x_lat.reshape(nb, nq, d)
```

```python
import functools
import math

import jax
import jax.numpy as jnp
from jax import lax
from jax.experimental import pallas as pl
from jax.experimental.pallas import tpu as pltpu

F32 = jnp.float32
BF16 = jnp.bfloat16

D_MODEL = 4096
GRID_W = 64
HEAD_DIM = 128
N_EXPERTS = 16
EC_CAPACITY_FACTOR = 2
EXPERT_FF = D_MODEL // 4
ROPE_THETA = 10000.0
NORM_EPS = 1e-6
N_MOD = 6
MOD_ROWS = 16
LANES = 128
BF16_ROWS = 16
VMEM_LIMIT = 58 * 1024 * 1024


def _cparams(sem):
    return pltpu.CompilerParams(dimension_semantics=sem, vmem_limit_bytes=VMEM_LIMIT)


def _bdot(a, b):
    return jnp.dot(a, b, preferred_element_type=F32)


def _dot_nt(a, b):
    return lax.dot_general(a, b, (((1,), (1,)), ((), ())), preferred_element_type=F32)


def _silu(x):
    return x * (1.0 / (1.0 + jnp.exp(-x)))


def _split_bf16(x):
    hi = x.astype(BF16)
    lo = (x - hi.astype(F32)).astype(BF16)
    return hi, lo


def _adaln_kernel(c_ref, w_ref, b_ref, o_ref):
    s = _silu(c_ref[...]).astype(BF16)
    o_ref[...] = _bdot(s, w_ref[...].astype(BF16)) + b_ref[...]


def _adaln(cond, w, b):
    n = w.shape[1]
    tn = 512
    out = pl.pallas_call(
        _adaln_kernel,
        grid=(n // tn,),
        in_specs=[
            pl.BlockSpec((MOD_ROWS, D_MODEL), lambda j: (0, 0)),
            pl.BlockSpec((D_MODEL, tn), lambda j: (0, j)),
            pl.BlockSpec((1, tn), lambda j: (0, j)),
        ],
        out_specs=pl.BlockSpec((MOD_ROWS, tn), lambda j: (0, j)),
        out_shape=jax.ShapeDtypeStruct((MOD_ROWS, n), F32),
        compiler_params=_cparams(("arbitrary",)),
        name="adaln",
    )(cond, w, b.reshape(1, n))
    return out.reshape(MOD_ROWS, N_MOD, 1, D_MODEL)


def _norm_mod(xv, g, shift, scale):
    ms = jnp.mean(xv * xv, axis=-1, keepdims=True)
    y = xv * lax.rsqrt(ms + NORM_EPS) * g
    return y * (1.0 + scale) + shift


def _qkv_kernel(x_hbm, g_ref, sh_ref, sc_ref, w_ref, cg_ref, cos_ref, sin_ref, o_ref, h_scr, x_ref, x_sem,
                *, tm, tn, n_qk_tiles, chunk, row_group, n_row_blocks):
    i = pl.program_id(0)
    j = pl.program_id(1)

    def x_copy(blk):
        return pltpu.make_async_copy(x_hbm.at[pl.ds(pl.multiple_of(blk * tm, tm), tm)], x_ref, x_sem)

    @pl.when((i == 0) & (j == 0))
    def _():
        x_copy(0).start()

    @pl.when(j == 0)
    def _():
        x_copy(i).wait()

    @pl.when((j == 1) & (i + 1 < n_row_blocks))
    def _():
        x_copy(i + 1).start()

    def qk_tile(make_h):
        w = w_ref[...].astype(BF16)
        for r in range(tm // row_group):
            rows = slice(r * row_group, (r + 1) * row_group)
            if make_h:
                for c in range(row_group // chunk):
                    rr = slice(r * row_group + c * chunk, r * row_group + (c + 1) * chunk)
                    h = _norm_mod(x_ref[rr, :], g_ref[...], sh_ref[...], sc_ref[...])
                    h_scr[rr, :] = h.astype(BF16)
            acc = _bdot(h_scr[rows, :], w)
            cos = cos_ref[rows, :]
            sin = sin_ref[rows, :]
            for gi in range(tn // HEAD_DIM):
                cols = slice(gi * HEAD_DIM, (gi + 1) * HEAD_DIM)
                a = acc[:, cols]
                ms = jnp.mean(a * a, axis=-1, keepdims=True)
                a = a * lax.rsqrt(ms + NORM_EPS) * cg_ref[:, cols]
                a = a * cos + pltpu.roll(a, HEAD_DIM // 2, axis=1) * sin
                o_ref[rows, cols] = a.astype(BF16)

    @pl.when(j == 0)
    def _():
        qk_tile(True)

    @pl.when((j > 0) & (j < n_qk_tiles))
    def _():
        qk_tile(False)

    @pl.when(j >= n_qk_tiles)
    def _():
        o_ref[...] = _bdot(h_scr[...], w_ref[...].astype(BF16)).astype(BF16)


def _qkv_proj(x, gain, mod, mod_row_fn, w, colgain, cos, sin, n_qk_cols, col_start=0):
    r, k = x.shape
    tm, tn = 1024, 512
    n = w.shape[1] - col_start
    j0 = col_start // tn
    npos = cos.shape[0]
    assert n // tn >= 2
    kern = functools.partial(_qkv_kernel, tm=tm, tn=tn, n_qk_tiles=(n_qk_cols - col_start) // tn, chunk=32,
                             row_group=256, n_row_blocks=r // tm)
    return pl.pallas_call(
        kern,
        grid=(r // tm, n // tn),
        in_specs=[
            pl.BlockSpec(memory_space=pl.ANY),
            pl.BlockSpec((1, k), lambda i, j: (0, 0)),
            pl.BlockSpec((None, None, 1, k), lambda i, j: (mod_row_fn(i), 0, 0, 0)),
            pl.BlockSpec((None, None, 1, k), lambda i, j: (mod_row_fn(i), 1, 0, 0)),
            pl.BlockSpec((k, tn), lambda i, j: (0, j0 + j)),
            pl.BlockSpec((1, tn), lambda i, j: (0, j0 + j)),
            pl.BlockSpec((None, tm, HEAD_DIM), lambda i, j: (i % npos, 0, 0)),
            pl.BlockSpec((None, tm, HEAD_DIM), lambda i, j: (i % npos, 0, 0)),
        ],
        out_specs=pl.BlockSpec((tm, tn), lambda i, j: (i, j)),
        out_shape=jax.ShapeDtypeStruct((r, n), BF16),
        scratch_shapes=[pltpu.VMEM((tm, k), BF16), pltpu.VMEM((tm, k), F32), pltpu.SemaphoreType.DMA(())],
        compiler_params=_cparams(("arbitrary", "arbitrary")),
        name="qkv_proj",
    )(x, gain.reshape(1, k), mod, mod, w, colgain, cos, sin)


def _attend(q, kvs, row_block=16, ones_cols=0):
    tq = q.shape[0]
    ss = [_dot_nt(q, k) for k, _ in kvs]
    p_blocks = [[] for _ in kvs]
    l_blocks = []
    for r in range(tq // row_block):
        rows = slice(r * row_block, (r + 1) * row_block)
        sb = [s[rows, :] for s in ss]
        m = functools.reduce(jnp.maximum, [jnp.max(x, axis=-1, keepdims=True) for x in sb])
        pb = [jnp.exp2(x - m) for x in sb]
        if not ones_cols:
            l_blocks.append(functools.reduce(lambda a, b: a + b,
                                             [jnp.sum(p, axis=-1, keepdims=True) for p in pb]))
        for blocks, p in zip(p_blocks, pb):
            blocks.append(p.astype(BF16))
    o = functools.reduce(lambda a, b: a + b,
                         [_bdot(jnp.concatenate(blocks, axis=0), v) for blocks, (_, v) in zip(p_blocks, kvs)])
    if ones_cols:
        dv = o.shape[1] - ones_cols
        return o[:, 0:dv] / o[:, dv:dv + 1]
    return o / jnp.concatenate(l_blocks, axis=0)


def _diff_lambda(lam_ref, lam_init):
    lp = lam_ref[...]
    s1 = jnp.sum(lp[0:1, :] * lp[1:2, :], axis=-1, keepdims=True)
    s2 = jnp.sum(lp[2:3, :] * lp[3:4, :], axis=-1, keepdims=True)
    return jnp.exp(s1) - jnp.exp(s2) + lam_init


def _diff_attn_kernel(*refs, lam_init, n_kv, sub_rows=256):
    lam_ref, subln_ref, q_ref = refs[:3]
    k_refs = refs[3:3 + n_kv]
    v_refs = refs[3 + n_kv:3 + 2 * n_kv]
    o_ref = refs[3 + 2 * n_kv]
    lam = _diff_lambda(lam_ref, lam_init)
    tq = q_ref.shape[0]
    sub = min(tq, sub_rows)
    for qs in range(tq // sub):
        rows = slice(qs * sub, (qs + 1) * sub)
        outs = []
        for t in range(2):
            cols = slice(t * HEAD_DIM, (t + 1) * HEAD_DIM)
            kvs = [(kr[:, cols], vr[...]) for kr, vr in zip(k_refs, v_refs)]
            outs.append(_attend(q_ref[rows, cols], kvs))
        o = outs[0] - lam * outs[1]
        ms = jnp.mean(o * o, axis=-1, keepdims=True)
        o = o * lax.rsqrt(ms + NORM_EPS) * subln_ref[...] * (1.0 - lam_init)
        o_ref[rows, :] = o.astype(BF16)


def _diff_attention(lam_params, subln, qkv_q, kv_sources, nb, nq, lam_init):
    heads = D_MODEL // (2 * HEAD_DIM)
    hw = 2 * HEAD_DIM
    tq = min(nq, 2048)
    nqb = nq // tq
    n_kv = len(kv_sources)
    in_specs = [
        pl.BlockSpec((4, HEAD_DIM), lambda b, h, i: (0, 0)),
        pl.BlockSpec((1, hw), lambda b, h, i: (0, 0)),
        pl.BlockSpec((tq, hw), lambda b, h, i: (b * nqb + i, h)),
    ]
    args = [lam_params, subln.reshape(1, hw), qkv_q]
    for arr, nk in kv_sources:
        in_specs.append(pl.BlockSpec((nk, hw), lambda b, h, i: (b, heads + h)))
        args.append(arr)
    for arr, nk in kv_sources:
        in_specs.append(pl.BlockSpec((nk, hw), lambda b, h, i: (b, 2 * heads + h)))
        args.append(arr)
    kern = functools.partial(_diff_attn_kernel, lam_init=lam_init, n_kv=n_kv)
    return pl.pallas_call(
        kern,
        grid=(nb, heads, nqb),
        in_specs=in_specs,
        out_specs=pl.BlockSpec((tq, hw), lambda b, h, i: (b * nqb + i, h)),
        out_shape=jax.ShapeDtypeStruct((nb * nq, D_MODEL), BF16),
        compiler_params=_cparams(("arbitrary", "arbitrary", "arbitrary")),
        name="diff_attn",
    )(*args)


def _gqa_attn_kernel(q_ref, kl_ref, kc_ref, vl_ref, vc_ref, o_ref, *, group, sub_rows=256):
    def with_ones(v):
        return jnp.concatenate([v, jnp.ones(v.shape, v.dtype)], axis=1)
    kvs = [(kl_ref[...], with_ones(vl_ref[...])), (kc_ref[...], with_ones(vc_ref[...]))]
    tq = q_ref.shape[0]
    sub = min(tq, sub_rows)
    for qs in range(tq // sub):
        rows = slice(qs * sub, (qs + 1) * sub)
        for g in range(group):
            cols = slice(g * HEAD_DIM, (g + 1) * HEAD_DIM)
            o_ref[rows, cols] = _attend(q_ref[rows, cols], kvs, ones_cols=HEAD_DIM).astype(BF16)


def _gqa_attention(qkv_lat, kv_ctx, nb, nq, nc):
    q_heads = D_MODEL // HEAD_DIM
    kv_heads = q_heads // 4
    group = q_heads // kv_heads
    gw = group * HEAD_DIM
    tq = 1024
    nqb = nq // tq
    k_off = D_MODEL // HEAD_DIM
    v_off = k_off + kv_heads
    kern = functools.partial(_gqa_attn_kernel, group=group)
    return pl.pallas_call(
        kern,
        grid=(nb, kv_heads, nqb),
        in_specs=[
            pl.BlockSpec((tq, gw), lambda b, h, i: (b * nqb + i, h)),
            pl.BlockSpec((nq, HEAD_DIM), lambda b, h, i: (b, k_off + h)),
            pl.BlockSpec((nc, HEAD_DIM), lambda b, h, i: (b, h)),
            pl.BlockSpec((nq, HEAD_DIM), lambda b, h, i: (b, v_off + h)),
            pl.BlockSpec((nc, HEAD_DIM), lambda b, h, i: (b, kv_heads + h)),
        ],
        out_specs=pl.BlockSpec((tq, gw), lambda b, h, i: (b * nqb + i, h)),
        out_shape=jax.ShapeDtypeStruct((nb * nq, D_MODEL), BF16),
        compiler_params=_cparams(("arbitrary", "arbitrary", "arbitrary")),
        name="gqa_attn",
    )(qkv_lat, qkv_lat, kv_ctx, qkv_lat, kv_ctx)


def _proj_res_kernel(o_ref, w_ref, res_ref, gate_ref, out_ref):
    acc = _bdot(o_ref[...], w_ref[...].astype(BF16))
    out_ref[...] = res_ref[...] + gate_ref[...] * acc


def _proj_residual(o, w, res, mod, mod_row_fn, gate_idx):
    r, k = o.shape
    n = w.shape[1]
    tm, tn = 1024, 512
    return pl.pallas_call(
        _proj_res_kernel,
        grid=(r // tm, n // tn),
        in_specs=[
            pl.BlockSpec((tm, k), lambda i, j: (i, 0)),
            pl.BlockSpec((k, tn), lambda i, j: (0, j)),
            pl.BlockSpec((tm, tn), lambda i, j: (i, j)),
            pl.BlockSpec((None, None, 1, tn), lambda i, j: (mod_row_fn(i), gate_idx, 0, j)),
        ],
        out_specs=pl.BlockSpec((tm, tn), lambda i, j: (i, j)),
        out_shape=jax.ShapeDtypeStruct((r, n), F32),
        compiler_params=_cparams(("arbitrary", "arbitrary")),
        name="out_proj",
    )(o, w, res, mod)


def _route_kernel(x_ref, g_ref, sh_ref, sc_ref, rt_ref, tri_ref, h_ref, gate_ref, slot_ref, lg_scr,
                  *, tc, n, cap):
    c = pl.program_id(1)
    h = _norm_mod(x_ref[...], g_ref[...], sh_ref[...], sc_ref[...])
    h_ref[...] = h.astype(BF16)
    h_hi, h_lo = _split_bf16(h)
    r_hi, r_lo = _split_bf16(rt_ref[...])
    lg_scr[c] = _dot_nt(r_hi, h_hi) + _dot_nt(r_hi, h_lo) + _dot_nt(r_lo, h_hi)

    @pl.when(c == n // tc - 1)
    def _():
        lg = jnp.concatenate([lg_scr[i] for i in range(n // tc)], axis=1)
        mx = jnp.max(lg, axis=0, keepdims=True)
        ex = jnp.exp(lg - mx)
        aff = ex / jnp.sum(ex, axis=0, keepdims=True)
        bits = pltpu.bitcast(aff, jnp.int32)
        thr = jnp.zeros((N_EXPERTS, 1), jnp.int32)
        for bit in range(30, -1, -1):
            cand = thr | jnp.int32(1 << bit)
            cnt = jnp.sum(jnp.where(bits >= cand, 1.0, 0.0), axis=1, keepdims=True)
            thr = jnp.where(cnt >= cap, cand, thr)
        gt = bits > thr
        eq = bits == thr
        need = cap - jnp.sum(jnp.where(gt, 1.0, 0.0), axis=1, keepdims=True)
        tri = tri_ref[...]
        eq_rank = _bdot(jnp.where(eq, 1.0, 0.0).astype(BF16), tri)
        sel = gt | (eq & (eq_rank < need))
        slot = _bdot(jnp.where(sel, 1.0, 0.0).astype(BF16), tri)
        gate_ref[...] = jnp.where(sel, aff, 0.0)
        slot_ref[...] = jnp.where(sel, slot.astype(jnp.int32), -1)


def _route(x, gain, mod, mod_row_fn, router_t, tri, nb, n, cap):
    k = x.shape[1]
    tc = 256
    nch = n // tc
    kern = functools.partial(_route_kernel, tc=tc, n=n, cap=cap)
    return pl.pallas_call(
        kern,
        grid=(nb, nch),
        in_specs=[
            pl.BlockSpec((tc, k), lambda b, c: (b * nch + c, 0)),
            pl.BlockSpec((1, k), lambda b, c: (0, 0)),
            pl.BlockSpec((None, None, 1, k), lambda b, c: (mod_row_fn(b), 3, 0, 0)),
            pl.BlockSpec((None, None, 1, k), lambda b, c: (mod_row_fn(b), 4, 0, 0)),
            pl.BlockSpec((N_EXPERTS, k), lambda b, c: (0, 0)),
            pl.BlockSpec((n, n), lambda b, c: (0, 0)),
        ],
        out_specs=[
            pl.BlockSpec((tc, k), lambda b, c: (b * nch + c, 0)),
            pl.BlockSpec((None, N_EXPERTS, n), lambda b, c: (b, 0, 0)),
            pl.BlockSpec((None, N_EXPERTS, n), lambda b, c: (b, 0, 0)),
        ],
        out_shape=[
            jax.ShapeDtypeStruct((nb * n, k), BF16),
            jax.ShapeDtypeStruct((nb, N_EXPERTS, n), F32),
            jax.ShapeDtypeStruct((nb, N_EXPERTS, n), jnp.int32),
        ],
        scratch_shapes=[pltpu.VMEM((nch, N_EXPERTS, tc), F32)],
        compiler_params=_cparams(("arbitrary", "arbitrary")),
        name="route",
    )(x, gain.reshape(1, k), mod, mod, router_t, tri)


def _gather_kernel(c1_ref, slot_ref, gate_ref, h_ref, xs_ref, gs_ref, *, cap, n, split, win):
    onehot = lax.broadcasted_iota(jnp.int32, (cap, n), 0) == slot_ref[...]
    gs_ref[...] = jnp.sum(jnp.where(onehot, gate_ref[...], 0.0), axis=1, keepdims=True)
    p = jnp.where(onehot, 1.0, 0.0).astype(BF16)
    if not split:
        xs_ref[...] = _bdot(p, h_ref[...]).astype(BF16)
        return
    hs, ht = cap // 2, n // 2
    mode = c1_ref[pl.program_id(0), pl.program_id(1)]

    @pl.when(mode == 2)
    def _():
        xs_ref[0:hs, :] = _bdot(p[0:hs, 0:win], h_ref[0:win, :]).astype(BF16)
        xs_ref[hs:cap, :] = _bdot(p[hs:cap, n - win:n], h_ref[n - win:n, :]).astype(BF16)

    @pl.when(mode == 0)
    def _():
        xs_ref[0:hs, :] = (_bdot(p[0:hs, 0:ht], h_ref[0:ht, :]) + _bdot(p[0:hs, ht:n], h_ref[ht:n, :])).astype(BF16)
        xs_ref[hs:cap, :] = _bdot(p[hs:cap, ht:n], h_ref[ht:n, :]).astype(BF16)

    @pl.when(mode == 1)
    def _():
        xs_ref[0:hs, :] = _bdot(p[0:hs, 0:ht], h_ref[0:ht, :]).astype(BF16)
        xs_ref[hs:cap, :] = (_bdot(p[hs:cap, ht:n], h_ref[ht:n, :])
                             + _bdot(p[hs:cap, 0:ht], h_ref[0:ht, :])).astype(BF16)


def _gather(slot_t, gate_t, h, nb, n, cap):
    k = h.shape[1]
    split = cap // 2 >= LANES
    win = 5 * n // 8
    kern = functools.partial(_gather_kernel, cap=cap, n=n, split=split, win=win)
    picked = (slot_t >= 0).astype(jnp.int32)

    def picks_before(t):
        return jnp.sum(picked[:, :, :t], axis=-1)
    windows_fit = (picks_before(win) >= cap // 2) & (picks_before(n - win) <= cap // 2)
    c1 = jnp.where(windows_fit, 2, jnp.where(picks_before(n // 2) < cap // 2, 0, 1)).astype(jnp.int32)
    row_spec = pl.BlockSpec((None, None, 1, n), lambda b, e, c: (b, e, 0, 0))
    grid_spec = pltpu.PrefetchScalarGridSpec(
        num_scalar_prefetch=1,
        grid=(nb, N_EXPERTS),
        in_specs=[row_spec, row_spec, pl.BlockSpec((n, k), lambda b, e, c: (b, 0))],
        out_specs=[pl.BlockSpec((None, cap, k), lambda b, e, c: (e, b, 0)),
                   pl.BlockSpec((None, cap, 1), lambda b, e, c: (e, b, 0))],
    )
    return pl.pallas_call(
        kern,
        grid_spec=grid_spec,
        out_shape=[jax.ShapeDtypeStruct((N_EXPERTS, nb * cap, k), BF16),
                   jax.ShapeDtypeStruct((N_EXPERTS, nb * cap, 1), F32)],
        compiler_params=_cparams(("arbitrary", "arbitrary")),
        name="gather",
    )(c1, slot_t.reshape(nb, N_EXPERTS, 1, n), gate_t.reshape(nb, N_EXPERTS, 1, n), h)


def _ffn_kernel(*refs, nf, tf, groups, n_parts, n_experts):
    xs_hbm = refs[0]
    gs_refs = refs[n_parts:2 * n_parts]
    wg_ref, wu_ref, wd_ref = refs[2 * n_parts:2 * n_parts + 3]
    y_refs = refs[2 * n_parts + 3:3 * n_parts + 3]
    hid_scr, xs_buf, xs_sem = refs[3 * n_parts + 3:3 * n_parts + 6]
    xs_refs = (xs_buf,) + tuple(refs[1:n_parts])
    e = pl.program_id(0)
    s = pl.program_id(1)

    def xs_copy(expert):
        return pltpu.make_async_copy(xs_hbm.at[expert], xs_buf, xs_sem)

    @pl.when((e == 0) & (s == 0))
    def _():
        xs_copy(0).start()

    @pl.when(s == 0)
    def _():
        xs_copy(e).wait()

    @pl.when((s == nf) & (e + 1 < n_experts))
    def _():
        xs_copy(e + 1).start()

    @pl.when(s < nf)
    def _():
        wg = wg_ref[...].astype(BF16)
        wu = wu_ref[...].astype(BF16)
        for part, r0, nr, h0 in groups:
            xs = xs_refs[part][r0:r0 + nr, :]
            a = _bdot(xs, wg)
            u = _bdot(xs, wu)
            hid_scr[s, h0:h0 + nr, :] = (_silu(a) * u * gs_refs[part][r0:r0 + nr, :]).astype(BF16)

    @pl.when(s >= nf)
    def _():
        wds = [wd_ref[f * tf:(f + 1) * tf, :].astype(BF16) for f in range(nf)]
        for part, r0, nr, h0 in groups:
            acc = _bdot(hid_scr[0, h0:h0 + nr, :], wds[0])
            for f in range(1, nf):
                acc = acc + _bdot(hid_scr[f, h0:h0 + nr, :], wds[f])
            y_refs[part][r0:r0 + nr, :] = acc.astype(BF16)


def _expert_ffn(parts, wg, wu, wd):
    e, _, k = parts[0][0].shape
    ff = wg.shape[2]
    tf, td = 256, 512
    nf, nd = ff // tf, k // td
    groups, h0 = [], 0
    for pi, (xs, _) in enumerate(parts):
        m = xs.shape[1]
        rg = min(m, 1024)
        for r0 in range(0, m, rg):
            groups.append((pi, r0, rg, h0))
            h0 += rg
    kern = functools.partial(_ffn_kernel, nf=nf, tf=tf, groups=tuple(groups), n_parts=len(parts), n_experts=e)
    row_spec = lambda m, w, **kw: pl.BlockSpec((None, m, w), lambda ei, s: (ei, 0, 0), **kw)
    up_idx = lambda ei, s: (ei, 0, jnp.minimum(s, nf - 1))
    down_idx = lambda ei, s: (ei, 0, jnp.maximum(s - nf, 0))
    return pl.pallas_call(
        kern,
        grid=(e, nf + nd),
        in_specs=([pl.BlockSpec(memory_space=pl.ANY)]
                  + [row_spec(xs.shape[1], k, pipeline_mode=pl.Buffered(1)) for xs, _ in parts[1:]]
                  + [row_spec(xs.shape[1], 1) for xs, _ in parts]
                  + [pl.BlockSpec((None, k, tf), up_idx), pl.BlockSpec((None, k, tf), up_idx),
                     pl.BlockSpec((None, ff, td), down_idx)]),
        out_specs=[pl.BlockSpec((None, xs.shape[1], td), down_idx) for xs, _ in parts],
        out_shape=[jax.ShapeDtypeStruct(xs.shape, BF16) for xs, _ in parts],
        scratch_shapes=[pltpu.VMEM((nf, h0, tf), BF16), pltpu.VMEM(parts[0][0].shape[1:], BF16),
                        pltpu.SemaphoreType.DMA(())],
        compiler_params=_cparams(("arbitrary", "arbitrary")),
        name="expert_ffn",
    )(*[xs for xs, _ in parts], *[gs for _, gs in parts], wg, wu, wd)


def _combine_kernel(hs_ref, slot_ref, y_ref, x_ref, g2_ref, o_ref, acc_scr, *, tt, cap, cap_pad, eg):
    e = pl.program_id(2)

    @pl.when(e == 0)
    def _():
        acc_scr[...] = jnp.zeros_like(acc_scr)

    slot_iota = lax.broadcasted_iota(jnp.int32, (cap_pad, tt), 0)
    onehot_t = jnp.concatenate([jnp.where(slot_iota == slot_ref[i], 1.0, 0.0) for i in range(eg)], axis=0)
    onehot = onehot_t.T[:, 0:eg * cap].astype(BF16)
    acc_scr[...] += _bdot(onehot, y_ref[...].reshape(eg * cap, y_ref.shape[2]))

    @pl.when(e == N_EXPERTS // eg - 1)
    def _():
        o_ref[...] = x_ref[...] + g2_ref[...] * acc_scr[...]


def _combine_win_kernel(hs_ref, slot_ref, y_hbm, x_ref, g2_ref, o_ref, acc_scr, ybuf, ysem,
                        *, tt, cap, nb, ntb, ne):
    b = pl.program_id(0)
    t = pl.program_id(1)
    e = pl.program_id(2)
    half = cap // 2

    def for_pieces(bb, ti, ei, fn):
        s0 = hs_ref[bb, ti, 2 * ei]
        s1 = hs_ref[bb, ti, 2 * ei + 1]
        narrow = (s0 >= 0) & (s1 >= 0)

        @pl.when(narrow)
        def _():
            fn(0, 2 * ei, s0)
            fn(1, 2 * ei + 1, s1)

        @pl.when(jnp.logical_not(narrow))
        def _():
            fn(0, 2 * ei, 0)
            fn(1, 2 * ei, half)
            fn(2, 2 * ei + 1, 0)
            fn(3, 2 * ei + 1, half)

    def piece_copy(bb, slot, piece, expert, row):
        src = y_hbm.at[expert, pl.ds(pl.multiple_of(bb * cap + row, BF16_ROWS), half)]
        return pltpu.make_async_copy(src, ybuf.at[slot, piece], ysem.at[slot])

    step = (b * ntb + t) * ne + e
    slot = lax.rem(step, 2)

    @pl.when(step == 0)
    def _():
        for_pieces(b, t, e, lambda p, ex, r: piece_copy(b, 0, p, ex, r).start())

    last_e = e == ne - 1
    last_t = t == ntb - 1
    b_next = jnp.where(last_e & last_t, b + 1, b)
    t_next = jnp.where(last_e, jnp.where(last_t, 0, t + 1), t)
    e_next = jnp.where(last_e, 0, e + 1)

    @pl.when(step + 1 < nb * ntb * ne)
    def _():
        for_pieces(b_next, t_next, e_next, lambda p, ex, r: piece_copy(b_next, 1 - slot, p, ex, r).start())

    for_pieces(b, t, e, lambda p, ex, r: piece_copy(b, slot, p, ex, r).wait())

    @pl.when(e == 0)
    def _():
        acc_scr[...] = jnp.zeros_like(acc_scr)

    s0 = hs_ref[b, t, 2 * e]
    s1 = hs_ref[b, t, 2 * e + 1]
    narrow = (s0 >= 0) & (s1 >= 0)
    k = ybuf.shape[3]

    @pl.when(narrow)
    def _():
        slot_iota = lax.broadcasted_iota(jnp.int32, (half, tt), 0)
        onehot_t = jnp.concatenate([jnp.where(slot_iota + s0 == slot_ref[0], 1.0, 0.0),
                                    jnp.where(slot_iota + s1 == slot_ref[1], 1.0, 0.0)], axis=0)
        acc_scr[...] += _bdot(onehot_t.T.astype(BF16), ybuf[slot, 0:2].reshape(2 * half, k))

    @pl.when(jnp.logical_not(narrow))
    def _():
        slot_iota = lax.broadcasted_iota(jnp.int32, (cap, tt), 0)
        onehot_t = jnp.concatenate([jnp.where(slot_iota == slot_ref[i], 1.0, 0.0) for i in range(2)], axis=0)
        acc_scr[...] += _bdot(onehot_t.T.astype(BF16), ybuf[slot].reshape(4 * half, k))

    @pl.when(e == ne - 1)
    def _():
        o_ref[...] = x_ref[...] + g2_ref[...] * acc_scr[...]


def _combine(slot_t, y, x, mod, mod_row_fn, nb, n, cap):
    k = x.shape[1]
    tt = min(n, 512)
    ntb = n // tt
    cap_pad = max(cap, LANES)
    eg = 2 if cap_pad == cap else 1
    halves = eg == 2 and cap // 2 >= LANES
    cnt = jnp.sum((slot_t >= 0).astype(jnp.int32).reshape(nb, N_EXPERTS, ntb, tt), axis=-1)
    hi = jnp.cumsum(cnt, axis=-1)
    lo = hi - cnt
    win0 = jnp.clip(lo // BF16_ROWS * BF16_ROWS, 0, cap - cap // 2)
    half_state = jnp.where(hi - win0 <= cap // 2, win0, -1).astype(jnp.int32)
    ne = N_EXPERTS // eg
    if halves:
        kern = functools.partial(_combine_win_kernel, tt=tt, cap=cap, nb=nb, ntb=ntb, ne=ne)
        y_spec = pl.BlockSpec(memory_space=pl.ANY)
        scratch = [pltpu.VMEM((tt, k), F32), pltpu.VMEM((2, 2 * eg, cap // 2, k), BF16),
                   pltpu.SemaphoreType.DMA((2,))]
    else:
        kern = functools.partial(_combine_kernel, tt=tt, cap=cap, cap_pad=cap_pad, eg=eg)
        y_spec = pl.BlockSpec((eg, cap, k), lambda b, t, e, hs: (e, b, 0))
        scratch = [pltpu.VMEM((tt, k), F32)]
    grid_spec = pltpu.PrefetchScalarGridSpec(
        num_scalar_prefetch=1,
        grid=(nb, ntb, ne),
        in_specs=[
            pl.BlockSpec((None, eg, 1, tt), lambda b, t, e, hs: (b, e, 0, t)),
            y_spec,
            pl.BlockSpec((tt, k), lambda b, t, e, hs: (b * ntb + t, 0)),
            pl.BlockSpec((None, None, 1, k), lambda b, t, e, hs: (mod_row_fn(b), 5, 0, 0)),
        ],
        out_specs=pl.BlockSpec((tt, k), lambda b, t, e, hs: (b * ntb + t, 0)),
        scratch_shapes=scratch,
    )
    return pl.pallas_call(
        kern,
        grid_spec=grid_spec,
        out_shape=jax.ShapeDtypeStruct((nb * n, k), F32),
        compiler_params=_cparams(("arbitrary", "arbitrary", "arbitrary")),
        name="combine",
    )(jnp.swapaxes(half_state, 1, 2), slot_t.reshape(nb, N_EXPERTS, 1, n), y, x, mod)


def _ec_moe_residual(streams, gain, mod, router, wg, wu, wd, nb):
    routed = []
    for x, n, mod_row_fn in streams:
        cap = EC_CAPACITY_FACTOR * n // N_EXPERTS
        idx = jnp.arange(n, dtype=jnp.int32)
        tri = (idx[:, None] < idx[None, :]).astype(BF16)
        h, gate_t, slot_t = _route(x, gain, mod, mod_row_fn, router.T, tri, nb, n, cap)
        routed.append((slot_t, cap) + _gather(slot_t, gate_t, h, nb, n, cap))
    ys = _expert_ffn([(xs, gs) for _, _, xs, gs in routed], wg, wu, wd)
    return [_combine(slot_t, y, x, mod, mod_row_fn, nb, n, cap)
            for (x, n, mod_row_fn), (slot_t, cap, _, _), y in zip(streams, routed, ys)]


def _rope_tables(n, tm):
    rows = n // GRID_W
    row = jnp.repeat(jnp.arange(rows, dtype=F32), GRID_W)
    col = jnp.tile(jnp.arange(GRID_W, dtype=F32), rows)
    axis_dim = HEAD_DIM // 2
    inv_freq = ROPE_THETA ** (-jnp.arange(0, axis_dim, 2, dtype=F32) / axis_dim)
    ang = jnp.concatenate([row[:, None] * inv_freq, col[:, None] * inv_freq], axis=-1)
    cos, sin = jnp.cos(ang), jnp.sin(ang)
    cos_full = jnp.concatenate([cos, cos], axis=-1).reshape(n // tm, tm, HEAD_DIM)
    sin_signed = jnp.concatenate([-sin, sin], axis=-1).reshape(n // tm, tm, HEAD_DIM)
    return cos_full, sin_signed


def kernel(x, c, ctx, c_ctx,
           mod_w_0, mod_b_0, norm1_0, norm2_0, wqkv_0, wo_0, qnorm_0, knorm_0,
           lam_q1_0, lam_k1_0, lam_q2_0, lam_k2_0, subln_0, router_0, we_gate_0, we_up_0, we_down_0,
           mod_w_1, mod_b_1, norm1_1, norm2_1, wqkv_1, wo_1, qnorm_1, knorm_1,
           router_1, we_gate_1, we_up_1, we_down_1):
    nb, nq, d = x.shape
    nc = ctx.shape[1]
    tm = 1024
    x_lat = x.reshape(nb * nq, d)
    x_ctx = ctx.reshape(nb * nc, d)
    cond = jnp.zeros((MOD_ROWS, d), F32).at[:nb].set(c).at[nb].set(c_ctx)
    lat_row = lambda i: i // (nq // tm)
    ctx_row = lambda i: nb
    lat_b = lambda b: b
    cos, sin = _rope_tables(nq, tm)
    cos_id = jnp.ones((1, tm, HEAD_DIM), F32)
    sin_id = jnp.zeros((1, tm, HEAD_DIM), F32)
    q_scale = math.log2(math.e) / math.sqrt(HEAD_DIM)

    mod0 = _adaln(cond, mod_w_0, mod_b_0)
    n_heads2 = d // HEAD_DIM
    colgain0 = jnp.concatenate([jnp.tile(qnorm_0 * q_scale, n_heads2), jnp.tile(knorm_0, n_heads2),
                                jnp.ones((d,), F32)]).reshape(1, 3 * d)
    qkv_lat = _qkv_proj(x_lat, norm1_0, mod0, lat_row, wqkv_0, colgain0, cos, sin, 2 * d)
    qkv_ctx = _qkv_proj(x_ctx, norm1_0, mod0, ctx_row, wqkv_0, colgain0, cos_id, sin_id, 2 * d)
    lam_init = 0.8 - 0.6 * math.exp(-0.3 * 0)
    lam_params = jnp.stack([lam_q1_0, lam_k1_0, lam_q2_0, lam_k2_0])
    o_lat = _diff_attention(lam_params, subln_0, qkv_lat, [(qkv_lat, nq), (qkv_ctx, nc)], nb, nq, lam_init)
    o_ctx = _diff_attention(lam_params, subln_0, qkv_ctx, [(qkv_ctx, nc)], nb, nc, lam_init)
    x_lat = _proj_residual(o_lat, wo_0, x_lat, mod0, lat_row, 2)
    x_ctx = _proj_residual(o_ctx, wo_0, x_ctx, mod0, ctx_row, 2)
    x_lat, x_ctx = _ec_moe_residual([(x_lat, nq, lat_b), (x_ctx, nc, ctx_row)], norm2_0, mod0,
                                    router_0, we_gate_0, we_up_0, we_down_0, nb)

    mod1 = _adaln(cond, mod_w_1, mod_b_1)
    kv_dim = (wqkv_1.shape[1] - d) // 2
    colgain1 = jnp.concatenate([jnp.tile(qnorm_1 * q_scale, d // HEAD_DIM),
                                jnp.tile(knorm_1, kv_dim // HEAD_DIM),
                                jnp.ones((kv_dim,), F32)]).reshape(1, d + 2 * kv_dim)
    qkv_lat = _qkv_proj(x_lat, norm1_1, mod1, lat_row, wqkv_1, colgain1, cos, sin, d + kv_dim)
    kv_ctx = _qkv_proj(x_ctx, norm1_1, mod1, ctx_row, wqkv_1, colgain1, cos_id, sin_id, d + kv_dim, col_start=d)
    o_lat = _gqa_attention(qkv_lat, kv_ctx, nb, nq, nc)
    x_lat = _proj_residual(o_lat, wo_1, x_lat, mod1, lat_row, 2)
    (x_lat,) = _ec_moe_residual([(x_lat, nq, lat_b)], norm2_1, mod1,
                                router_1, we_gate_1, we_up_1, we_down_1, nb)
    return x_lat.reshape(nb, nq, d)
```

```python
import functools
import math

import jax
import jax.numpy as jnp
from jax import lax
from jax.experimental import pallas as pl
from jax.experimental.pallas import tpu as pltpu

F32 = jnp.float32
BF16 = jnp.bfloat16

D_MODEL = 4096
GRID_W = 64
HEAD_DIM = 128
N_EXPERTS = 16
EC_CAPACITY_FACTOR = 2
EXPERT_FF = D_MODEL // 4
ROPE_THETA = 10000.0
NORM_EPS = 1e-6
N_MOD = 6
MOD_ROWS = 16
LANES = 128
BF16_ROWS = 16
VMEM_LIMIT = 58 * 1024 * 1024


def _cparams(sem):
    return pltpu.CompilerParams(dimension_semantics=sem, vmem_limit_bytes=VMEM_LIMIT)


def _bdot(a, b):
    return jnp.dot(a, b, preferred_element_type=F32)


def _dot_nt(a, b):
    return lax.dot_general(a, b, (((1,), (1,)), ((), ())), preferred_element_type=F32)


def _silu(x):
    return x * (1.0 / (1.0 + jnp.exp(-x)))


def _split_bf16(x):
    hi = x.astype(BF16)
    lo = (x - hi.astype(F32)).astype(BF16)
    return hi, lo


def _adaln_kernel(c_ref, w_ref, b_ref, o_ref):
    s = _silu(c_ref[...]).astype(BF16)
    o_ref[...] = _bdot(s, w_ref[...].astype(BF16)) + b_ref[...]


def _adaln(cond, w, b):
    n = w.shape[1]
    tn = 512
    out = pl.pallas_call(
        _adaln_kernel,
        grid=(n // tn,),
        in_specs=[
            pl.BlockSpec((MOD_ROWS, D_MODEL), lambda j: (0, 0)),
            pl.BlockSpec((D_MODEL, tn), lambda j: (0, j)),
            pl.BlockSpec((1, tn), lambda j: (0, j)),
        ],
        out_specs=pl.BlockSpec((MOD_ROWS, tn), lambda j: (0, j)),
        out_shape=jax.ShapeDtypeStruct((MOD_ROWS, n), F32),
        compiler_params=_cparams(("arbitrary",)),
        name="adaln",
    )(cond, w, b.reshape(1, n))
    return out.reshape(MOD_ROWS, N_MOD, 1, D_MODEL)


def _norm_mod(xv, g, shift, scale):
    ms = jnp.mean(xv * xv, axis=-1, keepdims=True)
    y = xv * lax.rsqrt(ms + NORM_EPS) * g
    return y * (1.0 + scale) + shift


def _qkv_kernel(x_hbm, g_ref, sh_ref, sc_ref, w_ref, cg_ref, cos_ref, sin_ref, o_ref, h_scr, x_ref, x_sem,
                *, tm, tn, n_qk_tiles, chunk, row_group, n_row_blocks):
    i = pl.program_id(0)
    j = pl.program_id(1)

    def x_copy(blk):
        return pltpu.make_async_copy(x_hbm.at[pl.ds(pl.multiple_of(blk * tm, tm), tm)], x_ref, x_sem)

    @pl.when((i == 0) & (j == 0))
    def _():
        x_copy(0).start()

    @pl.when(j == 0)
    def _():
        x_copy(i).wait()

    @pl.when((j == 1) & (i + 1 < n_row_blocks))
    def _():
        x_copy(i + 1).start()

    def qk_tile(make_h):
        w = w_ref[...].astype(BF16)
        for r in range(tm // row_group):
            rows = slice(r * row_group, (r + 1) * row_group)
            if make_h:
                for c in range(row_group // chunk):
                    rr = slice(r * row_group + c * chunk, r * row_group + (c + 1) * chunk)
                    h = _norm_mod(x_ref[rr, :], g_ref[...], sh_ref[...], sc_ref[...])
                    h_scr[rr, :] = h.astype(BF16)
            acc = _bdot(h_scr[rows, :], w)
            cos = cos_ref[rows, :]
            sin = sin_ref[rows, :]
            for gi in range(tn // HEAD_DIM):
                cols = slice(gi * HEAD_DIM, (gi + 1) * HEAD_DIM)
                a = acc[:, cols]
                ms = jnp.mean(a * a, axis=-1, keepdims=True)
                a = a * lax.rsqrt(ms + NORM_EPS) * cg_ref[:, cols]
                a = a * cos + pltpu.roll(a, HEAD_DIM // 2, axis=1) * sin
                o_ref[rows, cols] = a.astype(BF16)

    @pl.when(j == 0)
    def _():
        qk_tile(True)

    @pl.when((j > 0) & (j < n_qk_tiles))
    def _():
        qk_tile(False)

    @pl.when(j >= n_qk_tiles)
    def _():
        o_ref[...] = _bdot(h_scr[...], w_ref[...].astype(BF16)).astype(BF16)


def _qkv_proj(x, gain, mod, mod_row_fn, w, colgain, cos, sin, n_qk_cols, col_start=0):
    r, k = x.shape
    tm, tn = 1024, 512
    n = w.shape[1] - col_start
    j0 = col_start // tn
    npos = cos.shape[0]
    assert n // tn >= 2
    kern = functools.partial(_qkv_kernel, tm=tm, tn=tn, n_qk_tiles=(n_qk_cols - col_start) // tn, chunk=32,
                             row_group=256, n_row_blocks=r // tm)
    return pl.pallas_call(
        kern,
        grid=(r // tm, n // tn),
        in_specs=[
            pl.BlockSpec(memory_space=pl.ANY),
            pl.BlockSpec((1, k), lambda i, j: (0, 0)),
            pl.BlockSpec((None, None, 1, k), lambda i, j: (mod_row_fn(i), 0, 0, 0)),
            pl.BlockSpec((None, None, 1, k), lambda i, j: (mod_row_fn(i), 1, 0, 0)),
            pl.BlockSpec((k, tn), lambda i, j: (0, j0 + j)),
            pl.BlockSpec((1, tn), lambda i, j: (0, j0 + j)),
            pl.BlockSpec((None, tm, HEAD_DIM), lambda i, j: (i % npos, 0, 0)),
            pl.BlockSpec((None, tm, HEAD_DIM), lambda i, j: (i % npos, 0, 0)),
        ],
        out_specs=pl.BlockSpec((tm, tn), lambda i, j: (i, j)),
        out_shape=jax.ShapeDtypeStruct((r, n), BF16),
        scratch_shapes=[pltpu.VMEM((tm, k), BF16), pltpu.VMEM((tm, k), F32), pltpu.SemaphoreType.DMA(())],
        compiler_params=_cparams(("arbitrary", "arbitrary")),
        name="qkv_proj",
    )(x, gain.reshape(1, k), mod, mod, w, colgain, cos, sin)


def _attend(q, kvs, row_block=16, ones_cols=0):
    tq = q.shape[0]
    ss = [_dot_nt(q, k) for k, _ in kvs]
    p_blocks = [[] for _ in kvs]
    l_blocks = []
    for r in range(tq // row_block):
        rows = slice(r * row_block, (r + 1) * row_block)
        sb = [s[rows, :] for s in ss]
        m = functools.reduce(jnp.maximum, [jnp.max(x, axis=-1, keepdims=True) for x in sb])
        pb = [jnp.exp2(x - m) for x in sb]
        if not ones_cols:
            l_blocks.append(functools.reduce(lambda a, b: a + b,
                                             [jnp.sum(p, axis=-1, keepdims=True) for p in pb]))
        for blocks, p in zip(p_blocks, pb):
            blocks.append(p.astype(BF16))
    o = functools.reduce(lambda a, b: a + b,
                         [_bdot(jnp.concatenate(blocks, axis=0), v) for blocks, (_, v) in zip(p_blocks, kvs)])
    if ones_cols:
        dv = o.shape[1] - ones_cols
        return o[:, 0:dv] / o[:, dv:dv + 1]
    return o / jnp.concatenate(l_blocks, axis=0)


def _diff_lambda(lam_ref, lam_init):
    lp = lam_ref[...]
    s1 = jnp.sum(lp[0:1, :] * lp[1:2, :], axis=-1, keepdims=True)
    s2 = jnp.sum(lp[2:3, :] * lp[3:4, :], axis=-1, keepdims=True)
    return jnp.exp(s1) - jnp.exp(s2) + lam_init


def _diff_attn_kernel(*refs, lam_init, n_kv, sub_rows=256):
    lam_ref, subln_ref, q_ref = refs[:3]
    k_refs = refs[3:3 + n_kv]
    v_refs = refs[3 + n_kv:3 + 2 * n_kv]
    o_ref = refs[3 + 2 * n_kv]
    lam = _diff_lambda(lam_ref, lam_init)
    tq = q_ref.shape[0]
    sub = min(tq, sub_rows)
    hw = 2 * HEAD_DIM
    for hh in range(q_ref.shape[1] // hw):
        head = slice(hh * hw, (hh + 1) * hw)
        for qs in range(tq // sub):
            rows = slice(qs * sub, (qs + 1) * sub)
            outs = []
            for t in range(2):
                cols = slice(hh * hw + t * HEAD_DIM, hh * hw + (t + 1) * HEAD_DIM)
                kvs = [(kr[:, cols], vr[:, head]) for kr, vr in zip(k_refs, v_refs)]
                outs.append(_attend(q_ref[rows, cols], kvs))
            o = outs[0] - lam * outs[1]
            ms = jnp.mean(o * o, axis=-1, keepdims=True)
            o = o * lax.rsqrt(ms + NORM_EPS) * subln_ref[...] * (1.0 - lam_init)
            o_ref[rows, head] = o.astype(BF16)


def _diff_attention(lam_params, subln, qkv_q, kv_sources, nb, nq, lam_init):
    heads = D_MODEL // (2 * HEAD_DIM)
    hw = 2 * HEAD_DIM
    tq = min(nq, 2048)
    nqb = nq // tq
    hps = max(1, 1024 // tq)
    hgroups = heads // hps
    bw = hps * hw
    n_kv = len(kv_sources)
    in_specs = [
        pl.BlockSpec((4, HEAD_DIM), lambda b, h, i: (0, 0)),
        pl.BlockSpec((1, hw), lambda b, h, i: (0, 0)),
        pl.BlockSpec((tq, bw), lambda b, h, i: (b * nqb + i, h)),
    ]
    args = [lam_params, subln.reshape(1, hw), qkv_q]
    for arr, nk in kv_sources:
        in_specs.append(pl.BlockSpec((nk, bw), lambda b, h, i: (b, hgroups + h)))
        args.append(arr)
    for arr, nk in kv_sources:
        in_specs.append(pl.BlockSpec((nk, bw), lambda b, h, i: (b, 2 * hgroups + h)))
        args.append(arr)
    kern = functools.partial(_diff_attn_kernel, lam_init=lam_init, n_kv=n_kv)
    return pl.pallas_call(
        kern,
        grid=(nb, hgroups, nqb),
        in_specs=in_specs,
        out_specs=pl.BlockSpec((tq, bw), lambda b, h, i: (b * nqb + i, h)),
        out_shape=jax.ShapeDtypeStruct((nb * nq, D_MODEL), BF16),
        compiler_params=_cparams(("arbitrary", "arbitrary", "arbitrary")),
        name="diff_attn",
    )(*args)


def _gqa_attn_kernel(q_ref, kl_ref, kc_ref, vl_ref, vc_ref, o_ref, *, group, sub_rows=256):
    def with_ones(v):
        return jnp.concatenate([v, jnp.ones(v.shape, v.dtype)], axis=1)
    kvs = [(kl_ref[...], with_ones(vl_ref[...])), (kc_ref[...], with_ones(vc_ref[...]))]
    tq = q_ref.shape[0]
    sub = min(tq, sub_rows)
    for qs in range(tq // sub):
        rows = slice(qs * sub, (qs + 1) * sub)
        for g in range(group):
            cols = slice(g * HEAD_DIM, (g + 1) * HEAD_DIM)
            o_ref[rows, cols] = _attend(q_ref[rows, cols], kvs, ones_cols=HEAD_DIM).astype(BF16)


def _gqa_attention(qkv_lat, kv_ctx, nb, nq, nc):
    q_heads = D_MODEL // HEAD_DIM
    kv_heads = q_heads // 4
    group = q_heads // kv_heads
    gw = group * HEAD_DIM
    tq = 1024
    nqb = nq // tq
    k_off = D_MODEL // HEAD_DIM
    v_off = k_off + kv_heads
    kern = functools.partial(_gqa_attn_kernel, group=group)
    return pl.pallas_call(
        kern,
        grid=(nb, kv_heads, nqb),
        in_specs=[
            pl.BlockSpec((tq, gw), lambda b, h, i: (b * nqb + i, h)),
            pl.BlockSpec((nq, HEAD_DIM), lambda b, h, i: (b, k_off + h)),
            pl.BlockSpec((nc, HEAD_DIM), lambda b, h, i: (b, h)),
            pl.BlockSpec((nq, HEAD_DIM), lambda b, h, i: (b, v_off + h)),
            pl.BlockSpec((nc, HEAD_DIM), lambda b, h, i: (b, kv_heads + h)),
        ],
        out_specs=pl.BlockSpec((tq, gw), lambda b, h, i: (b * nqb + i, h)),
        out_shape=jax.ShapeDtypeStruct((nb * nq, D_MODEL), BF16),
        compiler_params=_cparams(("arbitrary", "arbitrary", "arbitrary")),
        name="gqa_attn",
    )(qkv_lat, qkv_lat, kv_ctx, qkv_lat, kv_ctx)


def _proj_res_kernel(o_ref, w_ref, res_ref, gate_ref, out_ref):
    acc = _bdot(o_ref[...], w_ref[...].astype(BF16))
    out_ref[...] = res_ref[...] + gate_ref[...] * acc


def _proj_residual(o, w, res, mod, mod_row_fn, gate_idx):
    r, k = o.shape
    n = w.shape[1]
    tm, tn = 1024, 512
    return pl.pallas_call(
        _proj_res_kernel,
        grid=(r // tm, n // tn),
        in_specs=[
            pl.BlockSpec((tm, k), lambda i, j: (i, 0)),
            pl.BlockSpec((k, tn), lambda i, j: (0, j)),
            pl.BlockSpec((tm, tn), lambda i, j: (i, j)),
            pl.BlockSpec((None, None, 1, tn), lambda i, j: (mod_row_fn(i), gate_idx, 0, j)),
        ],
        out_specs=pl.BlockSpec((tm, tn), lambda i, j: (i, j)),
        out_shape=jax.ShapeDtypeStruct((r, n), F32),
        compiler_params=_cparams(("arbitrary", "arbitrary")),
        name="out_proj",
    )(o, w, res, mod)


def _route_kernel(x_ref, g_ref, sh_ref, sc_ref, rt_ref, tri_ref, h_ref, gate_ref, slot_ref, lg_scr,
                  *, tc, n, cap):
    c = pl.program_id(1)
    h = _norm_mod(x_ref[...], g_ref[...], sh_ref[...], sc_ref[...])
    h_ref[...] = h.astype(BF16)
    h_hi, h_lo = _split_bf16(h)
    r_hi, r_lo = _split_bf16(rt_ref[...])
    lg_scr[c] = _dot_nt(r_hi, h_hi) + _dot_nt(r_hi, h_lo) + _dot_nt(r_lo, h_hi)

    @pl.when(c == n // tc - 1)
    def _():
        lg = jnp.concatenate([lg_scr[i] for i in range(n // tc)], axis=1)
        mx = jnp.max(lg, axis=0, keepdims=True)
        ex = jnp.exp(lg - mx)
        aff = ex / jnp.sum(ex, axis=0, keepdims=True)
        bits = pltpu.bitcast(aff, jnp.int32)
        thr = jnp.zeros((N_EXPERTS, 1), jnp.int32)
        for bit in range(30, -1, -1):
            cand = thr | jnp.int32(1 << bit)
            cnt = jnp.sum(jnp.where(bits >= cand, 1.0, 0.0), axis=1, keepdims=True)
            thr = jnp.where(cnt >= cap, cand, thr)
        gt = bits > thr
        eq = bits == thr
        need = cap - jnp.sum(jnp.where(gt, 1.0, 0.0), axis=1, keepdims=True)
        tri = tri_ref[...]
        eq_rank = _bdot(jnp.where(eq, 1.0, 0.0).astype(BF16), tri)
        sel = gt | (eq & (eq_rank < need))
        slot = _bdot(jnp.where(sel, 1.0, 0.0).astype(BF16), tri)
        gate_ref[...] = jnp.where(sel, aff, 0.0)
        slot_ref[...] = jnp.where(sel, slot.astype(jnp.int32), -1)


def _route(x, gain, mod, mod_row_fn, router_t, tri, nb, n, cap):
    k = x.shape[1]
    tc = 256
    nch = n // tc
    kern = functools.partial(_route_kernel, tc=tc, n=n, cap=cap)
    return pl.pallas_call(
        kern,
        grid=(nb, nch),
        in_specs=[
            pl.BlockSpec((tc, k), lambda b, c: (b * nch + c, 0)),
            pl.BlockSpec((1, k), lambda b, c: (0, 0)),
            pl.BlockSpec((None, None, 1, k), lambda b, c: (mod_row_fn(b), 3, 0, 0)),
            pl.BlockSpec((None, None, 1, k), lambda b, c: (mod_row_fn(b), 4, 0, 0)),
            pl.BlockSpec((N_EXPERTS, k), lambda b, c: (0, 0)),
            pl.BlockSpec((n, n), lambda b, c: (0, 0)),
        ],
        out_specs=[
            pl.BlockSpec((tc, k), lambda b, c: (b * nch + c, 0)),
            pl.BlockSpec((None, N_EXPERTS, n), lambda b, c: (b, 0, 0)),
            pl.BlockSpec((None, N_EXPERTS, n), lambda b, c: (b, 0, 0)),
        ],
        out_shape=[
            jax.ShapeDtypeStruct((nb * n, k), BF16),
            jax.ShapeDtypeStruct((nb, N_EXPERTS, n), F32),
            jax.ShapeDtypeStruct((nb, N_EXPERTS, n), jnp.int32),
        ],
        scratch_shapes=[pltpu.VMEM((nch, N_EXPERTS, tc), F32)],
        compiler_params=_cparams(("arbitrary", "arbitrary")),
        name="route",
    )(x, gain.reshape(1, k), mod, mod, router_t, tri)


def _gather_kernel(c1_ref, slot_ref, gate_ref, h_ref, xs_ref, gs_ref, *, cap, n, split, win):
    if not split:
        ge = slot_ref.shape[0]
        slot_iota = lax.broadcasted_iota(jnp.int32, (cap, n), 0)
        onehots = [slot_iota == slot_ref[i] for i in range(ge)]
        gates = [jnp.sum(jnp.where(oh, gate_ref[i], 0.0), axis=1, keepdims=True) for i, oh in enumerate(onehots)]
        p = jnp.concatenate([jnp.where(oh, 1.0, 0.0) for oh in onehots], axis=0).astype(BF16)
        xs_ref[...] = _bdot(p, h_ref[...]).astype(BF16).reshape(xs_ref.shape)
        gs_ref[...] = jnp.concatenate(gates, axis=0).reshape(gs_ref.shape)
        return
    onehot = lax.broadcasted_iota(jnp.int32, (cap, n), 0) == slot_ref[0]
    gs_ref[0] = jnp.sum(jnp.where(onehot, gate_ref[0], 0.0), axis=1, keepdims=True)
    p = jnp.where(onehot, 1.0, 0.0).astype(BF16)
    xs_ref = xs_ref.at[0]
    hs, ht = cap // 2, n // 2
    mode = c1_ref[pl.program_id(0), pl.program_id(1)]

    @pl.when(mode == 2)
    def _():
        xs_ref[0:hs, :] = _bdot(p[0:hs, 0:win], h_ref[0:win, :]).astype(BF16)
        xs_ref[hs:cap, :] = _bdot(p[hs:cap, n - win:n], h_ref[n - win:n, :]).astype(BF16)

    @pl.when(mode == 0)
    def _():
        xs_ref[0:hs, :] = (_bdot(p[0:hs, 0:ht], h_ref[0:ht, :]) + _bdot(p[0:hs, ht:n], h_ref[ht:n, :])).astype(BF16)
        xs_ref[hs:cap, :] = _bdot(p[hs:cap, ht:n], h_ref[ht:n, :]).astype(BF16)

    @pl.when(mode == 1)
    def _():
        xs_ref[0:hs, :] = _bdot(p[0:hs, 0:ht], h_ref[0:ht, :]).astype(BF16)
        xs_ref[hs:cap, :] = (_bdot(p[hs:cap, ht:n], h_ref[ht:n, :])
                             + _bdot(p[hs:cap, 0:ht], h_ref[0:ht, :])).astype(BF16)


def _gather(slot_t, gate_t, h, nb, n, cap):
    k = h.shape[1]
    split = cap // 2 >= LANES
    win = 5 * n // 8
    kern = functools.partial(_gather_kernel, cap=cap, n=n, split=split, win=win)
    picked = (slot_t >= 0).astype(jnp.int32)

    def picks_before(t):
        return jnp.sum(picked[:, :, :t], axis=-1)
    windows_fit = (picks_before(win) >= cap // 2) & (picks_before(n - win) <= cap // 2)
    c1 = jnp.where(windows_fit, 2, jnp.where(picks_before(n // 2) < cap // 2, 0, 1)).astype(jnp.int32)
    ge = 1 if split else N_EXPERTS
    row_spec = pl.BlockSpec((None, ge, 1, n), lambda b, e, c: (b, e, 0, 0))
    grid_spec = pltpu.PrefetchScalarGridSpec(
        num_scalar_prefetch=1,
        grid=(nb, N_EXPERTS // ge),
        in_specs=[row_spec, row_spec, pl.BlockSpec((n, k), lambda b, e, c: (b, 0))],
        out_specs=[pl.BlockSpec((ge, cap, k), lambda b, e, c: (e, b, 0)),
                   pl.BlockSpec((ge, cap, 1), lambda b, e, c: (e, b, 0))],
    )
    return pl.pallas_call(
        kern,
        grid_spec=grid_spec,
        out_shape=[jax.ShapeDtypeStruct((N_EXPERTS, nb * cap, k), BF16),
                   jax.ShapeDtypeStruct((N_EXPERTS, nb * cap, 1), F32)],
        compiler_params=_cparams(("arbitrary", "arbitrary")),
        name="gather",
    )(c1, slot_t.reshape(nb, N_EXPERTS, 1, n), gate_t.reshape(nb, N_EXPERTS, 1, n), h)


def _ffn_kernel(*refs, nf, tf, groups, n_parts, n_experts):
    xs_hbm = refs[0]
    gs_refs = refs[n_parts:2 * n_parts]
    wg_ref, wu_ref, wd_ref = refs[2 * n_parts:2 * n_parts + 3]
    y_refs = refs[2 * n_parts + 3:3 * n_parts + 3]
    hid_scr, xs_buf, xs_sem = refs[3 * n_parts + 3:3 * n_parts + 6]
    xs_refs = (xs_buf,) + tuple(refs[1:n_parts])
    e = pl.program_id(0)
    s = pl.program_id(1)

    def xs_copy(expert):
        return pltpu.make_async_copy(xs_hbm.at[expert], xs_buf, xs_sem)

    @pl.when((e == 0) & (s == 0))
    def _():
        xs_copy(0).start()

    @pl.when(s == 0)
    def _():
        xs_copy(e).wait()

    @pl.when((s == nf) & (e + 1 < n_experts))
    def _():
        xs_copy(e + 1).start()

    @pl.when(s < nf)
    def _():
        wg = wg_ref[...].astype(BF16)
        wu = wu_ref[...].astype(BF16)
        for part, r0, nr, h0 in groups:
            xs = xs_refs[part][r0:r0 + nr, :]
            a = _bdot(xs, wg)
            u = _bdot(xs, wu)
            hid_scr[s, h0:h0 + nr, :] = (_silu(a) * u * gs_refs[part][r0:r0 + nr, :]).astype(BF16)

    @pl.when(s >= nf)
    def _():
        wds = [wd_ref[f * tf:(f + 1) * tf, :].astype(BF16) for f in range(nf)]
        for part, r0, nr, h0 in groups:
            acc = _bdot(hid_scr[0, h0:h0 + nr, :], wds[0])
            for f in range(1, nf):
                acc = acc + _bdot(hid_scr[f, h0:h0 + nr, :], wds[f])
            y_refs[part][r0:r0 + nr, :] = acc.astype(BF16)


def _expert_ffn(parts, wg, wu, wd):
    e, _, k = parts[0][0].shape
    ff = wg.shape[2]
    tf, td = 256, 512
    nf, nd = ff // tf, k // td
    groups, h0 = [], 0
    for pi, (xs, _) in enumerate(parts):
        m = xs.shape[1]
        rg = min(m, 1024)
        for r0 in range(0, m, rg):
            groups.append((pi, r0, rg, h0))
            h0 += rg
    kern = functools.partial(_ffn_kernel, nf=nf, tf=tf, groups=tuple(groups), n_parts=len(parts), n_experts=e)
    row_spec = lambda m, w, **kw: pl.BlockSpec((None, m, w), lambda ei, s: (ei, 0, 0), **kw)
    up_idx = lambda ei, s: (ei, 0, jnp.minimum(s, nf - 1))
    down_idx = lambda ei, s: (ei, 0, jnp.maximum(s - nf, 0))
    return pl.pallas_call(
        kern,
        grid=(e, nf + nd),
        in_specs=([pl.BlockSpec(memory_space=pl.ANY)]
                  + [row_spec(xs.shape[1], k, pipeline_mode=pl.Buffered(1)) for xs, _ in parts[1:]]
                  + [row_spec(xs.shape[1], 1) for xs, _ in parts]
                  + [pl.BlockSpec((None, k, tf), up_idx), pl.BlockSpec((None, k, tf), up_idx),
                     pl.BlockSpec((None, ff, td), down_idx)]),
        out_specs=[pl.BlockSpec((None, xs.shape[1], td), down_idx) for xs, _ in parts],
        out_shape=[jax.ShapeDtypeStruct(xs.shape, BF16) for xs, _ in parts],
        scratch_shapes=[pltpu.VMEM((nf, h0, tf), BF16), pltpu.VMEM(parts[0][0].shape[1:], BF16),
                        pltpu.SemaphoreType.DMA(())],
        compiler_params=_cparams(("arbitrary", "arbitrary")),
        name="expert_ffn",
    )(*[xs for xs, _ in parts], *[gs for _, gs in parts], wg, wu, wd)


def _combine_kernel(hs_ref, slot_ref, y_ref, x_ref, g2_ref, o_ref, acc_scr, *, tt, cap, cap_pad, eg):
    e = pl.program_id(2)

    @pl.when(e == 0)
    def _():
        acc_scr[...] = jnp.zeros_like(acc_scr)

    slot_iota = lax.broadcasted_iota(jnp.int32, (cap_pad, tt), 0)
    onehot_t = jnp.concatenate([jnp.where(slot_iota == slot_ref[i], 1.0, 0.0) for i in range(eg)], axis=0)
    onehot = onehot_t.T[:, 0:eg * cap].astype(BF16)
    acc_scr[...] += _bdot(onehot, y_ref[...].reshape(eg * cap, y_ref.shape[2]))

    @pl.when(e == N_EXPERTS // eg - 1)
    def _():
        o_ref[...] = x_ref[...] + g2_ref[...] * acc_scr[...]


def _combine_win_kernel(hs_ref, slot_ref, y_hbm, x_ref, g2_ref, o_ref, acc_scr, ybuf, ysem,
                        *, tt, cap, nb, ntb, ne):
    b = pl.program_id(0)
    t = pl.program_id(1)
    e = pl.program_id(2)
    half = cap // 2

    def for_pieces(bb, ti, ei, fn):
        s0 = hs_ref[bb, ti, 2 * ei]
        s1 = hs_ref[bb, ti, 2 * ei + 1]
        narrow = (s0 >= 0) & (s1 >= 0)

        @pl.when(narrow)
        def _():
            fn(0, 2 * ei, s0)
            fn(1, 2 * ei + 1, s1)

        @pl.when(jnp.logical_not(narrow))
        def _():
            fn(0, 2 * ei, 0)
            fn(1, 2 * ei, half)
            fn(2, 2 * ei + 1, 0)
            fn(3, 2 * ei + 1, half)

    def piece_copy(bb, slot, piece, expert, row):
        src = y_hbm.at[expert, pl.ds(pl.multiple_of(bb * cap + row, BF16_ROWS), half)]
        return pltpu.make_async_copy(src, ybuf.at[slot, piece], ysem.at[slot])

    step = (b * ntb + t) * ne + e
    slot = lax.rem(step, 2)

    @pl.when(step == 0)
    def _():
        for_pieces(b, t, e, lambda p, ex, r: piece_copy(b, 0, p, ex, r).start())

    last_e = e == ne - 1
    last_t = t == ntb - 1
    b_next = jnp.where(last_e & last_t, b + 1, b)
    t_next = jnp.where(last_e, jnp.where(last_t, 0, t + 1), t)
    e_next = jnp.where(last_e, 0, e + 1)

    @pl.when(step + 1 < nb * ntb * ne)
    def _():
        for_pieces(b_next, t_next, e_next, lambda p, ex, r: piece_copy(b_next, 1 - slot, p, ex, r).start())

    for_pieces(b, t, e, lambda p, ex, r: piece_copy(b, slot, p, ex, r).wait())

    @pl.when(e == 0)
    def _():
        acc_scr[...] = jnp.zeros_like(acc_scr)

    s0 = hs_ref[b, t, 2 * e]
    s1 = hs_ref[b, t, 2 * e + 1]
    narrow = (s0 >= 0) & (s1 >= 0)
    k = ybuf.shape[3]

    @pl.when(narrow)
    def _():
        slot_iota = lax.broadcasted_iota(jnp.int32, (half, tt), 0)
        onehot_t = jnp.concatenate([jnp.where(slot_iota + s0 == slot_ref[0], 1.0, 0.0),
                                    jnp.where(slot_iota + s1 == slot_ref[1], 1.0, 0.0)], axis=0)
        acc_scr[...] += _bdot(onehot_t.T.astype(BF16), ybuf[slot, 0:2].reshape(2 * half, k))

    @pl.when(jnp.logical_not(narrow))
    def _():
        slot_iota = lax.broadcasted_iota(jnp.int32, (cap, tt), 0)
        onehot_t = jnp.concatenate([jnp.where(slot_iota == slot_ref[i], 1.0, 0.0) for i in range(2)], axis=0)
        acc_scr[...] += _bdot(onehot_t.T.astype(BF16), ybuf[slot].reshape(4 * half, k))

    @pl.when(e == ne - 1)
    def _():
        o_ref[...] = x_ref[...] + g2_ref[...] * acc_scr[...]


def _combine(slot_t, y, x, mod, mod_row_fn, nb, n, cap):
    k = x.shape[1]
    tt = min(n, 512)
    ntb = n // tt
    halves = cap // 2 >= LANES
    eg = 2 if halves else min(N_EXPERTS, max(1, 256 // cap))
    cap_pad = cap if (eg * cap) % LANES == 0 else max(cap, LANES)
    cnt = jnp.sum((slot_t >= 0).astype(jnp.int32).reshape(nb, N_EXPERTS, ntb, tt), axis=-1)
    hi = jnp.cumsum(cnt, axis=-1)
    lo = hi - cnt
    win0 = jnp.clip(lo // BF16_ROWS * BF16_ROWS, 0, cap - cap // 2)
    half_state = jnp.where(hi - win0 <= cap // 2, win0, -1).astype(jnp.int32)
    ne = N_EXPERTS // eg
    if halves:
        kern = functools.partial(_combine_win_kernel, tt=tt, cap=cap, nb=nb, ntb=ntb, ne=ne)
        y_spec = pl.BlockSpec(memory_space=pl.ANY)
        scratch = [pltpu.VMEM((tt, k), F32), pltpu.VMEM((2, 2 * eg, cap // 2, k), BF16),
                   pltpu.SemaphoreType.DMA((2,))]
    else:
        kern = functools.partial(_combine_kernel, tt=tt, cap=cap, cap_pad=cap_pad, eg=eg)
        y_spec = pl.BlockSpec((eg, cap, k), lambda b, t, e, hs: (e, b, 0))
        scratch = [pltpu.VMEM((tt, k), F32)]
    grid_spec = pltpu.PrefetchScalarGridSpec(
        num_scalar_prefetch=1,
        grid=(nb, ntb, ne),
        in_specs=[
            pl.BlockSpec((None, eg, 1, tt), lambda b, t, e, hs: (b, e, 0, t)),
            y_spec,
            pl.BlockSpec((tt, k), lambda b, t, e, hs: (b * ntb + t, 0)),
            pl.BlockSpec((None, None, 1, k), lambda b, t, e, hs: (mod_row_fn(b), 5, 0, 0)),
        ],
        out_specs=pl.BlockSpec((tt, k), lambda b, t, e, hs: (b * ntb + t, 0)),
        scratch_shapes=scratch,
    )
    return pl.pallas_call(
        kern,
        grid_spec=grid_spec,
        out_shape=jax.ShapeDtypeStruct((nb * n, k), F32),
        compiler_params=_cparams(("arbitrary", "arbitrary", "arbitrary")),
        name="combine",
    )(jnp.swapaxes(half_state, 1, 2), slot_t.reshape(nb, N_EXPERTS, 1, n), y, x, mod)


def _ec_moe_residual(streams, gain, mod, router, wg, wu, wd, nb):
    routed = []
    for x, n, mod_row_fn in streams:
        cap = EC_CAPACITY_FACTOR * n // N_EXPERTS
        idx = jnp.arange(n, dtype=jnp.int32)
        tri = (idx[:, None] < idx[None, :]).astype(BF16)
        h, gate_t, slot_t = _route(x, gain, mod, mod_row_fn, router.T, tri, nb, n, cap)
        routed.append((slot_t, cap) + _gather(slot_t, gate_t, h, nb, n, cap))
    ys = _expert_ffn([(xs, gs) for _, _, xs, gs in routed], wg, wu, wd)
    return [_combine(slot_t, y, x, mod, mod_row_fn, nb, n, cap)
            for (x, n, mod_row_fn), (slot_t, cap, _, _), y in zip(streams, routed, ys)]


def _rope_tables(n, tm):
    rows = n // GRID_W
    row = jnp.repeat(jnp.arange(rows, dtype=F32), GRID_W)
    col = jnp.tile(jnp.arange(GRID_W, dtype=F32), rows)
    axis_dim = HEAD_DIM // 2
    inv_freq = ROPE_THETA ** (-jnp.arange(0, axis_dim, 2, dtype=F32) / axis_dim)
    ang = jnp.concatenate([row[:, None] * inv_freq, col[:, None] * inv_freq], axis=-1)
    cos, sin = jnp.cos(ang), jnp.sin(ang)
    cos_full = jnp.concatenate([cos, cos], axis=-1).reshape(n // tm, tm, HEAD_DIM)
    sin_signed = jnp.concatenate([-sin, sin], axis=-1).reshape(n // tm, tm, HEAD_DIM)
    return cos_full, sin_signed


def kernel(x, c, ctx, c_ctx,
           mod_w_0, mod_b_0, norm1_0, norm2_0, wqkv_0, wo_0, qnorm_0, knorm_0,
           lam_q1_0, lam_k1_0, lam_q2_0, lam_k2_0, subln_0, router_0, we_gate_0, we_up_0, we_down_0,
           mod_w_1, mod_b_1, norm1_1, norm2_1, wqkv_1, wo_1, qnorm_1, knorm_1,
           router_1, we_gate_1, we_up_1, we_down_1):
    nb, nq, d = x.shape
    nc = ctx.shape[1]
    tm = 1024
    x_lat = x.reshape(nb * nq, d)
    x_ctx = ctx.reshape(nb * nc, d)
    cond = jnp.zeros((MOD_ROWS, d), F32).at[:nb].set(c).at[nb].set(c_ctx)
    lat_row = lambda i: i // (nq // tm)
    ctx_row = lambda i: nb
    lat_b = lambda b: b
    cos, sin = _rope_tables(nq, tm)
    cos_id = jnp.ones((1, tm, HEAD_DIM), F32)
    sin_id = jnp.zeros((1, tm, HEAD_DIM), F32)
    q_scale = math.log2(math.e) / math.sqrt(HEAD_DIM)

    mod0 = _adaln(cond, mod_w_0, mod_b_0)
    n_heads2 = d // HEAD_DIM
    colgain0 = jnp.concatenate([jnp.tile(qnorm_0 * q_scale, n_heads2), jnp.tile(knorm_0, n_heads2),
                                jnp.ones((d,), F32)]).reshape(1, 3 * d)
    qkv_lat = _qkv_proj(x_lat, norm1_0, mod0, lat_row, wqkv_0, colgain0, cos, sin, 2 * d)
    qkv_ctx = _qkv_proj(x_ctx, norm1_0, mod0, ctx_row, wqkv_0, colgain0, cos_id, sin_id, 2 * d)
    lam_init = 0.8 - 0.6 * math.exp(-0.3 * 0)
    lam_params = jnp.stack([lam_q1_0, lam_k1_0, lam_q2_0, lam_k2_0])
    o_lat = _diff_attention(lam_params, subln_0, qkv_lat, [(qkv_lat, nq), (qkv_ctx, nc)], nb, nq, lam_init)
    o_ctx = _diff_attention(lam_params, subln_0, qkv_ctx, [(qkv_ctx, nc)], nb, nc, lam_init)
    x_lat = _proj_residual(o_lat, wo_0, x_lat, mod0, lat_row, 2)
    x_ctx = _proj_residual(o_ctx, wo_0, x_ctx, mod0, ctx_row, 2)
    x_lat, x_ctx = _ec_moe_residual([(x_lat, nq, lat_b), (x_ctx, nc, ctx_row)], norm2_0, mod0,
                                    router_0, we_gate_0, we_up_0, we_down_0, nb)

    mod1 = _adaln(cond, mod_w_1, mod_b_1)
    kv_dim = (wqkv_1.shape[1] - d) // 2
    colgain1 = jnp.concatenate([jnp.tile(qnorm_1 * q_scale, d // HEAD_DIM),
                                jnp.tile(knorm_1, kv_dim // HEAD_DIM),
                                jnp.ones((kv_dim,), F32)]).reshape(1, d + 2 * kv_dim)
    qkv_lat = _qkv_proj(x_lat, norm1_1, mod1, lat_row, wqkv_1, colgain1, cos, sin, d + kv_dim)
    kv_ctx = _qkv_proj(x_ctx, norm1_1, mod1, ctx_row, wqkv_1, colgain1, cos_id, sin_id, d + kv_dim, col_start=d)
    o_lat = _gqa_attention(qkv_lat, kv_ctx, nb, nq, nc)
    x_lat = _proj_residual(o_lat, wo_1, x_lat, mod1, lat_row, 2)
    (x_lat,) = _ec_moe_residual([(x_lat, nq, lat_b)], norm2_1, mod1,
                                router_1, we_gate_1, we_up_1, we_down_1, nb)
    return x_lat.reshape(nb, nq, d)
```

```python
import functools
import math

import jax
import jax.numpy as jnp
from jax import lax
from jax.experimental import pallas as pl
from jax.experimental.pallas import tpu as pltpu

F32 = jnp.float32
BF16 = jnp.bfloat16

D_MODEL = 4096
GRID_W = 64
HEAD_DIM = 128
N_EXPERTS = 16
EC_CAPACITY_FACTOR = 2
EXPERT_FF = D_MODEL // 4
ROPE_THETA = 10000.0
NORM_EPS = 1e-6
N_MOD = 6
MOD_ROWS = 16
LANES = 128
BF16_ROWS = 16
VMEM_LIMIT = 58 * 1024 * 1024


def _cparams(sem):
    return pltpu.CompilerParams(dimension_semantics=sem, vmem_limit_bytes=VMEM_LIMIT)


def _bdot(a, b):
    return jnp.dot(a, b, preferred_element_type=F32)


def _dot_nt(a, b):
    return lax.dot_general(a, b, (((1,), (1,)), ((), ())), preferred_element_type=F32)


def _silu(x):
    return x * (1.0 / (1.0 + jnp.exp(-x)))


def _split_bf16(x):
    hi = x.astype(BF16)
    lo = (x - hi.astype(F32)).astype(BF16)
    return hi, lo


def _adaln_kernel(c_ref, w_ref, b_ref, o_ref):
    s = _silu(c_ref[...]).astype(BF16)
    o_ref[...] = _bdot(s, w_ref[...].astype(BF16)) + b_ref[...]


def _adaln(cond, w, b):
    n = w.shape[1]
    tn = 512
    out = pl.pallas_call(
        _adaln_kernel,
        grid=(n // tn,),
        in_specs=[
            pl.BlockSpec((MOD_ROWS, D_MODEL), lambda j: (0, 0)),
            pl.BlockSpec((D_MODEL, tn), lambda j: (0, j)),
            pl.BlockSpec((1, tn), lambda j: (0, j)),
        ],
        out_specs=pl.BlockSpec((MOD_ROWS, tn), lambda j: (0, j)),
        out_shape=jax.ShapeDtypeStruct((MOD_ROWS, n), F32),
        compiler_params=_cparams(("arbitrary",)),
        name="adaln",
    )(cond, w, b.reshape(1, n))
    return out.reshape(MOD_ROWS, N_MOD, 1, D_MODEL)


def _norm_mod(xv, g, shift, scale):
    ms = jnp.mean(xv * xv, axis=-1, keepdims=True)
    y = xv * lax.rsqrt(ms + NORM_EPS) * g
    return y * (1.0 + scale) + shift


def _qkv_kernel(x_hbm, g_ref, sh_ref, sc_ref, w_ref, cg_ref, cos_ref, sin_ref, o_ref, h_scr, x_ref, x_sem,
                *, tm, tn, n_qk_tiles, chunk, row_group, n_row_blocks):
    i = pl.program_id(0)
    j = pl.program_id(1)

    def x_copy(blk):
        return pltpu.make_async_copy(x_hbm.at[pl.ds(pl.multiple_of(blk * tm, tm), tm)], x_ref, x_sem)

    @pl.when((i == 0) & (j == 0))
    def _():
        x_copy(0).start()

    @pl.when(j == 0)
    def _():
        x_copy(i).wait()

    @pl.when((j == 1) & (i + 1 < n_row_blocks))
    def _():
        x_copy(i + 1).start()

    def qk_tile(make_h):
        w = w_ref[...].astype(BF16)
        for r in range(tm // row_group):
            rows = slice(r * row_group, (r + 1) * row_group)
            if make_h:
                for c in range(row_group // chunk):
                    rr = slice(r * row_group + c * chunk, r * row_group + (c + 1) * chunk)
                    h = _norm_mod(x_ref[rr, :], g_ref[...], sh_ref[...], sc_ref[...])
                    h_scr[rr, :] = h.astype(BF16)
            acc = _bdot(h_scr[rows, :], w)
            cos = cos_ref[rows, :]
            sin = sin_ref[rows, :]
            for gi in range(tn // HEAD_DIM):
                cols = slice(gi * HEAD_DIM, (gi + 1) * HEAD_DIM)
                a = acc[:, cols]
                ms = jnp.mean(a * a, axis=-1, keepdims=True)
                a = a * lax.rsqrt(ms + NORM_EPS) * cg_ref[:, cols]
                a = a * cos + pltpu.roll(a, HEAD_DIM // 2, axis=1) * sin
                o_ref[rows, cols] = a.astype(BF16)

    @pl.when(j == 0)
    def _():
        qk_tile(True)

    @pl.when((j > 0) & (j < n_qk_tiles))
    def _():
        qk_tile(False)

    @pl.when(j >= n_qk_tiles)
    def _():
        o_ref[...] = _bdot(h_scr[...], w_ref[...].astype(BF16)).astype(BF16)


def _qkv_proj(x, gain, mod, mod_row_fn, w, colgain, cos, sin, n_qk_cols, col_start=0):
    r, k = x.shape
    tm, tn = 1024, 512
    n = w.shape[1] - col_start
    j0 = col_start // tn
    npos = cos.shape[0]
    assert n // tn >= 2
    kern = functools.partial(_qkv_kernel, tm=tm, tn=tn, n_qk_tiles=(n_qk_cols - col_start) // tn, chunk=32,
                             row_group=256, n_row_blocks=r // tm)
    return pl.pallas_call(
        kern,
        grid=(r // tm, n // tn),
        in_specs=[
            pl.BlockSpec(memory_space=pl.ANY),
            pl.BlockSpec((1, k), lambda i, j: (0, 0)),
            pl.BlockSpec((None, None, 1, k), lambda i, j: (mod_row_fn(i), 0, 0, 0)),
            pl.BlockSpec((None, None, 1, k), lambda i, j: (mod_row_fn(i), 1, 0, 0)),
            pl.BlockSpec((k, tn), lambda i, j: (0, j0 + j)),
            pl.BlockSpec((1, tn), lambda i, j: (0, j0 + j)),
            pl.BlockSpec((None, tm, HEAD_DIM), lambda i, j: (i % npos, 0, 0)),
            pl.BlockSpec((None, tm, HEAD_DIM), lambda i, j: (i % npos, 0, 0)),
        ],
        out_specs=pl.BlockSpec((tm, tn), lambda i, j: (i, j)),
        out_shape=jax.ShapeDtypeStruct((r, n), BF16),
        scratch_shapes=[pltpu.VMEM((tm, k), BF16), pltpu.VMEM((tm, k), F32), pltpu.SemaphoreType.DMA(())],
        compiler_params=_cparams(("arbitrary", "arbitrary")),
        name="qkv_proj",
    )(x, gain.reshape(1, k), mod, mod, w, colgain, cos, sin)


def _attend(q, kvs, row_block=16, ones_cols=0):
    tq = q.shape[0]
    ss = [_dot_nt(q, k) for k, _ in kvs]
    p_blocks = [[] for _ in kvs]
    l_blocks = []
    for r in range(tq // row_block):
        rows = slice(r * row_block, (r + 1) * row_block)
        sb = [s[rows, :] for s in ss]
        m = functools.reduce(jnp.maximum, [jnp.max(x, axis=-1, keepdims=True) for x in sb])
        pb = [jnp.exp2(x - m) for x in sb]
        if not ones_cols:
            l_blocks.append(functools.reduce(lambda a, b: a + b,
                                             [jnp.sum(p, axis=-1, keepdims=True) for p in pb]))
        for blocks, p in zip(p_blocks, pb):
            blocks.append(p.astype(BF16))
    o = functools.reduce(lambda a, b: a + b,
                         [_bdot(jnp.concatenate(blocks, axis=0), v) for blocks, (_, v) in zip(p_blocks, kvs)])
    if ones_cols:
        dv = o.shape[1] - ones_cols
        return o[:, 0:dv] / o[:, dv:dv + 1]
    return o / jnp.concatenate(l_blocks, axis=0)


def _diff_lambda(lam_ref, lam_init):
    lp = lam_ref[...]
    s1 = jnp.sum(lp[0:1, :] * lp[1:2, :], axis=-1, keepdims=True)
    s2 = jnp.sum(lp[2:3, :] * lp[3:4, :], axis=-1, keepdims=True)
    return jnp.exp(s1) - jnp.exp(s2) + lam_init


def _diff_attn_kernel(*refs, lam_init, n_kv, sub_rows=256):
    lam_ref, subln_ref, q_ref = refs[:3]
    k_refs = refs[3:3 + n_kv]
    v_refs = refs[3 + n_kv:3 + 2 * n_kv]
    o_ref = refs[3 + 2 * n_kv]
    lam = _diff_lambda(lam_ref, lam_init)
    tq = q_ref.shape[0]
    sub = min(tq, sub_rows)
    hw = 2 * HEAD_DIM
    for hh in range(q_ref.shape[1] // hw):
        head = slice(hh * hw, (hh + 1) * hw)
        for qs in range(tq // sub):
            rows = slice(qs * sub, (qs + 1) * sub)
            outs = []
            for t in range(2):
                cols = slice(hh * hw + t * HEAD_DIM, hh * hw + (t + 1) * HEAD_DIM)
                kvs = [(kr[:, cols], vr[:, head]) for kr, vr in zip(k_refs, v_refs)]
                outs.append(_attend(q_ref[rows, cols], kvs))
            o = outs[0] - lam * outs[1]
            ms = jnp.mean(o * o, axis=-1, keepdims=True)
            o = o * lax.rsqrt(ms + NORM_EPS) * subln_ref[...] * (1.0 - lam_init)
            o_ref[rows, head] = o.astype(BF16)


def _diff_attention(lam_params, subln, qkv_q, kv_sources, nb, nq, lam_init):
    heads = D_MODEL // (2 * HEAD_DIM)
    hw = 2 * HEAD_DIM
    tq = min(nq, 2048)
    nqb = nq // tq
    hps = 1
    hgroups = heads // hps
    bw = hps * hw
    n_kv = len(kv_sources)
    in_specs = [
        pl.BlockSpec((4, HEAD_DIM), lambda b, h, i: (0, 0)),
        pl.BlockSpec((1, hw), lambda b, h, i: (0, 0)),
        pl.BlockSpec((tq, bw), lambda b, h, i: (b * nqb + i, h)),
    ]
    args = [lam_params, subln.reshape(1, hw), qkv_q]
    for arr, nk in kv_sources:
        in_specs.append(pl.BlockSpec((nk, bw), lambda b, h, i: (b, hgroups + h)))
        args.append(arr)
    for arr, nk in kv_sources:
        in_specs.append(pl.BlockSpec((nk, bw), lambda b, h, i: (b, 2 * hgroups + h)))
        args.append(arr)
    kern = functools.partial(_diff_attn_kernel, lam_init=lam_init, n_kv=n_kv)
    return pl.pallas_call(
        kern,
        grid=(nb, hgroups, nqb),
        in_specs=in_specs,
        out_specs=pl.BlockSpec((tq, bw), lambda b, h, i: (b * nqb + i, h)),
        out_shape=jax.ShapeDtypeStruct((nb * nq, D_MODEL), BF16),
        compiler_params=_cparams(("arbitrary", "arbitrary", "arbitrary")),
        name="diff_attn",
    )(*args)


def _gqa_attn_kernel(q_ref, kl_ref, kc_ref, vl_ref, vc_ref, o_ref, *, group, sub_rows=256):
    def with_ones(v):
        return jnp.concatenate([v, jnp.ones(v.shape, v.dtype)], axis=1)
    kvs = [(kl_ref[...], with_ones(vl_ref[...])), (kc_ref[...], with_ones(vc_ref[...]))]
    tq = q_ref.shape[0]
    sub = min(tq, sub_rows)
    for qs in range(tq // sub):
        rows = slice(qs * sub, (qs + 1) * sub)
        for g in range(group):
            cols = slice(g * HEAD_DIM, (g + 1) * HEAD_DIM)
            o_ref[rows, cols] = _attend(q_ref[rows, cols], kvs, ones_cols=HEAD_DIM).astype(BF16)


def _gqa_attention(qkv_lat, kv_ctx, nb, nq, nc):
    q_heads = D_MODEL // HEAD_DIM
    kv_heads = q_heads // 4
    group = q_heads // kv_heads
    gw = group * HEAD_DIM
    tq = 2048
    nqb = nq // tq
    k_off = D_MODEL // HEAD_DIM
    v_off = k_off + kv_heads
    kern = functools.partial(_gqa_attn_kernel, group=group)
    return pl.pallas_call(
        kern,
        grid=(nb, kv_heads, nqb),
        in_specs=[
            pl.BlockSpec((tq, gw), lambda b, h, i: (b * nqb + i, h)),
            pl.BlockSpec((nq, HEAD_DIM), lambda b, h, i: (b, k_off + h)),
            pl.BlockSpec((nc, HEAD_DIM), lambda b, h, i: (b, h)),
            pl.BlockSpec((nq, HEAD_DIM), lambda b, h, i: (b, v_off + h)),
            pl.BlockSpec((nc, HEAD_DIM), lambda b, h, i: (b, kv_heads + h)),
        ],
        out_specs=pl.BlockSpec((tq, gw), lambda b, h, i: (b * nqb + i, h)),
        out_shape=jax.ShapeDtypeStruct((nb * nq, D_MODEL), BF16),
        compiler_params=_cparams(("arbitrary", "arbitrary", "arbitrary")),
        name="gqa_attn",
    )(qkv_lat, qkv_lat, kv_ctx, qkv_lat, kv_ctx)


def _proj_res_kernel(o_ref, w_ref, res_ref, gate_ref, out_ref):
    acc = _bdot(o_ref[...], w_ref[...].astype(BF16))
    out_ref[...] = res_ref[...] + gate_ref[...] * acc


def _proj_residual(o, w, res, mod, mod_row_fn, gate_idx):
    r, k = o.shape
    n = w.shape[1]
    tm, tn = 1024, 512
    return pl.pallas_call(
        _proj_res_kernel,
        grid=(r // tm, n // tn),
        in_specs=[
            pl.BlockSpec((tm, k), lambda i, j: (i, 0)),
            pl.BlockSpec((k, tn), lambda i, j: (0, j)),
            pl.BlockSpec((tm, tn), lambda i, j: (i, j)),
            pl.BlockSpec((None, None, 1, tn), lambda i, j: (mod_row_fn(i), gate_idx, 0, j)),
        ],
        out_specs=pl.BlockSpec((tm, tn), lambda i, j: (i, j)),
        out_shape=jax.ShapeDtypeStruct((r, n), F32),
        compiler_params=_cparams(("arbitrary", "arbitrary")),
        name="out_proj",
    )(o, w, res, mod)


def _route_kernel(x_ref, g_ref, sh_ref, sc_ref, rt_ref, tri_ref, h_ref, gate_ref, slot_ref, lg_scr,
                  *, tc, n, cap):
    c = pl.program_id(1)
    h = _norm_mod(x_ref[...], g_ref[...], sh_ref[...], sc_ref[...])
    h_ref[...] = h.astype(BF16)
    h_hi, h_lo = _split_bf16(h)
    r_hi, r_lo = _split_bf16(rt_ref[...])
    lg_scr[c] = _dot_nt(r_hi, h_hi) + _dot_nt(r_hi, h_lo) + _dot_nt(r_lo, h_hi)

    @pl.when(c == n // tc - 1)
    def _():
        lg = jnp.concatenate([lg_scr[i] for i in range(n // tc)], axis=1)
        mx = jnp.max(lg, axis=0, keepdims=True)
        ex = jnp.exp(lg - mx)
        aff = ex / jnp.sum(ex, axis=0, keepdims=True)
        bits = pltpu.bitcast(aff, jnp.int32)
        thr = jnp.zeros((N_EXPERTS, 1), jnp.int32)
        for bit in range(30, -1, -1):
            cand = thr | jnp.int32(1 << bit)
            cnt = jnp.sum(jnp.where(bits >= cand, 1.0, 0.0), axis=1, keepdims=True)
            thr = jnp.where(cnt >= cap, cand, thr)
        gt = bits > thr
        eq = bits == thr
        need = cap - jnp.sum(jnp.where(gt, 1.0, 0.0), axis=1, keepdims=True)
        tri = tri_ref[...]
        eq_rank = _bdot(jnp.where(eq, 1.0, 0.0).astype(BF16), tri)
        sel = gt | (eq & (eq_rank < need))
        slot = _bdot(jnp.where(sel, 1.0, 0.0).astype(BF16), tri)
        gate_ref[...] = jnp.where(sel, aff, 0.0)
        slot_ref[...] = jnp.where(sel, slot.astype(jnp.int32), -1)


def _route(x, gain, mod, mod_row_fn, router_t, tri, nb, n, cap):
    k = x.shape[1]
    tc = 256
    nch = n // tc
    kern = functools.partial(_route_kernel, tc=tc, n=n, cap=cap)
    return pl.pallas_call(
        kern,
        grid=(nb, nch),
        in_specs=[
            pl.BlockSpec((tc, k), lambda b, c: (b * nch + c, 0)),
            pl.BlockSpec((1, k), lambda b, c: (0, 0)),
            pl.BlockSpec((None, None, 1, k), lambda b, c: (mod_row_fn(b), 3, 0, 0)),
            pl.BlockSpec((None, None, 1, k), lambda b, c: (mod_row_fn(b), 4, 0, 0)),
            pl.BlockSpec((N_EXPERTS, k), lambda b, c: (0, 0)),
            pl.BlockSpec((n, n), lambda b, c: (0, 0)),
        ],
        out_specs=[
            pl.BlockSpec((tc, k), lambda b, c: (b * nch + c, 0)),
            pl.BlockSpec((None, N_EXPERTS, n), lambda b, c: (b, 0, 0)),
            pl.BlockSpec((None, N_EXPERTS, n), lambda b, c: (b, 0, 0)),
        ],
        out_shape=[
            jax.ShapeDtypeStruct((nb * n, k), BF16),
            jax.ShapeDtypeStruct((nb, N_EXPERTS, n), F32),
            jax.ShapeDtypeStruct((nb, N_EXPERTS, n), jnp.int32),
        ],
        scratch_shapes=[pltpu.VMEM((nch, N_EXPERTS, tc), F32)],
        compiler_params=_cparams(("arbitrary", "arbitrary")),
        name="route",
    )(x, gain.reshape(1, k), mod, mod, router_t, tri)


def _gather_kernel(c1_ref, slot_ref, gate_ref, h_ref, xs_ref, gs_ref, *, cap, n, split, win):
    if not split:
        ge = slot_ref.shape[0]
        slot_iota = lax.broadcasted_iota(jnp.int32, (cap, n), 0)
        onehots = [slot_iota == slot_ref[i] for i in range(ge)]
        gates = [jnp.sum(jnp.where(oh, gate_ref[i], 0.0), axis=1, keepdims=True) for i, oh in enumerate(onehots)]
        p = jnp.concatenate([jnp.where(oh, 1.0, 0.0) for oh in onehots], axis=0).astype(BF16)
        xs_ref[...] = _bdot(p, h_ref[...]).astype(BF16).reshape(xs_ref.shape)
        gs_ref[...] = jnp.concatenate(gates, axis=0).reshape(gs_ref.shape)
        return
    onehot = lax.broadcasted_iota(jnp.int32, (cap, n), 0) == slot_ref[0]
    gs_ref[0] = jnp.sum(jnp.where(onehot, gate_ref[0], 0.0), axis=1, keepdims=True)
    p = jnp.where(onehot, 1.0, 0.0).astype(BF16)
    xs_ref = xs_ref.at[0]
    hs, ht = cap // 2, n // 2
    mode = c1_ref[pl.program_id(0), pl.program_id(1)]

    @pl.when(mode == 2)
    def _():
        xs_ref[0:hs, :] = _bdot(p[0:hs, 0:win], h_ref[0:win, :]).astype(BF16)
        xs_ref[hs:cap, :] = _bdot(p[hs:cap, n - win:n], h_ref[n - win:n, :]).astype(BF16)

    @pl.when(mode == 0)
    def _():
        xs_ref[0:hs, :] = (_bdot(p[0:hs, 0:ht], h_ref[0:ht, :]) + _bdot(p[0:hs, ht:n], h_ref[ht:n, :])).astype(BF16)
        xs_ref[hs:cap, :] = _bdot(p[hs:cap, ht:n], h_ref[ht:n, :]).astype(BF16)

    @pl.when(mode == 1)
    def _():
        xs_ref[0:hs, :] = _bdot(p[0:hs, 0:ht], h_ref[0:ht, :]).astype(BF16)
        xs_ref[hs:cap, :] = (_bdot(p[hs:cap, ht:n], h_ref[ht:n, :])
                             + _bdot(p[hs:cap, 0:ht], h_ref[0:ht, :])).astype(BF16)


def _gather(slot_t, gate_t, h, nb, n, cap):
    k = h.shape[1]
    split = cap // 2 >= LANES
    win = 5 * n // 8
    kern = functools.partial(_gather_kernel, cap=cap, n=n, split=split, win=win)
    picked = (slot_t >= 0).astype(jnp.int32)

    def picks_before(t):
        return jnp.sum(picked[:, :, :t], axis=-1)
    windows_fit = (picks_before(win) >= cap // 2) & (picks_before(n - win) <= cap // 2)
    c1 = jnp.where(windows_fit, 2, jnp.where(picks_before(n // 2) < cap // 2, 0, 1)).astype(jnp.int32)
    ge = 1 if split else N_EXPERTS
    row_spec = pl.BlockSpec((None, ge, 1, n), lambda b, e, c: (b, e, 0, 0))
    grid_spec = pltpu.PrefetchScalarGridSpec(
        num_scalar_prefetch=1,
        grid=(nb, N_EXPERTS // ge),
        in_specs=[row_spec, row_spec, pl.BlockSpec((n, k), lambda b, e, c: (b, 0))],
        out_specs=[pl.BlockSpec((ge, cap, k), lambda b, e, c: (e, b, 0)),
                   pl.BlockSpec((ge, cap, 1), lambda b, e, c: (e, b, 0))],
    )
    return pl.pallas_call(
        kern,
        grid_spec=grid_spec,
        out_shape=[jax.ShapeDtypeStruct((N_EXPERTS, nb * cap, k), BF16),
                   jax.ShapeDtypeStruct((N_EXPERTS, nb * cap, 1), F32)],
        compiler_params=_cparams(("arbitrary", "arbitrary")),
        name="gather",
    )(c1, slot_t.reshape(nb, N_EXPERTS, 1, n), gate_t.reshape(nb, N_EXPERTS, 1, n), h)


def _ffn_kernel(*refs, nf, tf, groups, n_parts, n_experts):
    xs_hbm = refs[0]
    gs_refs = refs[n_parts:2 * n_parts]
    wg_ref, wu_ref, wd_ref = refs[2 * n_parts:2 * n_parts + 3]
    y_refs = refs[2 * n_parts + 3:3 * n_parts + 3]
    hid_scr, xs_buf, xs_sem = refs[3 * n_parts + 3:3 * n_parts + 6]
    xs_refs = (xs_buf,) + tuple(refs[1:n_parts])
    e = pl.program_id(0)
    s = pl.program_id(1)

    def xs_copy(expert):
        return pltpu.make_async_copy(xs_hbm.at[expert], xs_buf, xs_sem)

    @pl.when((e == 0) & (s == 0))
    def _():
        xs_copy(0).start()

    @pl.when(s == 0)
    def _():
        xs_copy(e).wait()

    @pl.when((s == nf) & (e + 1 < n_experts))
    def _():
        xs_copy(e + 1).start()

    @pl.when(s < nf)
    def _():
        wg = wg_ref[...].astype(BF16)
        wu = wu_ref[...].astype(BF16)
        for part, r0, nr, h0 in groups:
            xs = xs_refs[part][r0:r0 + nr, :]
            a = _bdot(xs, wg)
            u = _bdot(xs, wu)
            hid_scr[s, h0:h0 + nr, :] = (_silu(a) * u * gs_refs[part][r0:r0 + nr, :]).astype(BF16)

    @pl.when(s >= nf)
    def _():
        wds = [wd_ref[f * tf:(f + 1) * tf, :].astype(BF16) for f in range(nf)]
        for part, r0, nr, h0 in groups:
            acc = _bdot(hid_scr[0, h0:h0 + nr, :], wds[0])
            for f in range(1, nf):
                acc = acc + _bdot(hid_scr[f, h0:h0 + nr, :], wds[f])
            y_refs[part][r0:r0 + nr, :] = acc.astype(BF16)


def _expert_ffn(parts, wg, wu, wd):
    e, _, k = parts[0][0].shape
    ff = wg.shape[2]
    tf, td = 256, 512
    nf, nd = ff // tf, k // td
    groups, h0 = [], 0
    for pi, (xs, _) in enumerate(parts):
        m = xs.shape[1]
        rg = min(m, 1024)
        for r0 in range(0, m, rg):
            groups.append((pi, r0, rg, h0))
            h0 += rg
    kern = functools.partial(_ffn_kernel, nf=nf, tf=tf, groups=tuple(groups), n_parts=len(parts), n_experts=e)
    row_spec = lambda m, w, **kw: pl.BlockSpec((None, m, w), lambda ei, s: (ei, 0, 0), **kw)
    up_idx = lambda ei, s: (ei, 0, jnp.minimum(s, nf - 1))
    down_idx = lambda ei, s: (ei, 0, jnp.maximum(s - nf, 0))
    return pl.pallas_call(
        kern,
        grid=(e, nf + nd),
        in_specs=([pl.BlockSpec(memory_space=pl.ANY)]
                  + [row_spec(xs.shape[1], k, pipeline_mode=pl.Buffered(1)) for xs, _ in parts[1:]]
                  + [row_spec(xs.shape[1], 1) for xs, _ in parts]
                  + [pl.BlockSpec((None, k, tf), up_idx), pl.BlockSpec((None, k, tf), up_idx),
                     pl.BlockSpec((None, ff, td), down_idx)]),
        out_specs=[pl.BlockSpec((None, xs.shape[1], td), down_idx) for xs, _ in parts],
        out_shape=[jax.ShapeDtypeStruct(xs.shape, BF16) for xs, _ in parts],
        scratch_shapes=[pltpu.VMEM((nf, h0, tf), BF16), pltpu.VMEM(parts[0][0].shape[1:], BF16),
                        pltpu.SemaphoreType.DMA(())],
        compiler_params=_cparams(("arbitrary", "arbitrary")),
        name="expert_ffn",
    )(*[xs for xs, _ in parts], *[gs for _, gs in parts], wg, wu, wd)


def _combine_kernel(hs_ref, slot_ref, y_ref, x_ref, g2_ref, o_ref, acc_scr, *, tt, cap, cap_pad, eg):
    e = pl.program_id(2)

    @pl.when(e == 0)
    def _():
        acc_scr[...] = jnp.zeros_like(acc_scr)

    slot_iota = lax.broadcasted_iota(jnp.int32, (cap_pad, tt), 0)
    onehot_t = jnp.concatenate([jnp.where(slot_iota == slot_ref[i], 1.0, 0.0) for i in range(eg)], axis=0)
    onehot = onehot_t.T[:, 0:eg * cap].astype(BF16)
    acc_scr[...] += _bdot(onehot, y_ref[...].reshape(eg * cap, y_ref.shape[2]))

    @pl.when(e == N_EXPERTS // eg - 1)
    def _():
        o_ref[...] = x_ref[...] + g2_ref[...] * acc_scr[...]


def _combine_win_kernel(hs_ref, slot_ref, y_hbm, x_ref, g2_ref, o_ref, acc_scr, ybuf, ysem,
                        *, tt, cap, nb, ntb, ne):
    b = pl.program_id(0)
    t = pl.program_id(1)
    e = pl.program_id(2)
    half = cap // 2

    def for_pieces(bb, ti, ei, fn):
        s0 = hs_ref[bb, ti, 2 * ei]
        s1 = hs_ref[bb, ti, 2 * ei + 1]
        narrow = (s0 >= 0) & (s1 >= 0)

        @pl.when(narrow)
        def _():
            fn(0, 2 * ei, s0)
            fn(1, 2 * ei + 1, s1)

        @pl.when(jnp.logical_not(narrow))
        def _():
            fn(0, 2 * ei, 0)
            fn(1, 2 * ei, half)
            fn(2, 2 * ei + 1, 0)
            fn(3, 2 * ei + 1, half)

    def piece_copy(bb, slot, piece, expert, row):
        src = y_hbm.at[expert, pl.ds(pl.multiple_of(bb * cap + row, BF16_ROWS), half)]
        return pltpu.make_async_copy(src, ybuf.at[slot, piece], ysem.at[slot])

    step = (b * ntb + t) * ne + e
    slot = lax.rem(step, 2)

    @pl.when(step == 0)
    def _():
        for_pieces(b, t, e, lambda p, ex, r: piece_copy(b, 0, p, ex, r).start())

    last_e = e == ne - 1
    last_t = t == ntb - 1
    b_next = jnp.where(last_e & last_t, b + 1, b)
    t_next = jnp.where(last_e, jnp.where(last_t, 0, t + 1), t)
    e_next = jnp.where(last_e, 0, e + 1)

    @pl.when(step + 1 < nb * ntb * ne)
    def _():
        for_pieces(b_next, t_next, e_next, lambda p, ex, r: piece_copy(b_next, 1 - slot, p, ex, r).start())

    for_pieces(b, t, e, lambda p, ex, r: piece_copy(b, slot, p, ex, r).wait())

    @pl.when(e == 0)
    def _():
        acc_scr[...] = jnp.zeros_like(acc_scr)

    s0 = hs_ref[b, t, 2 * e]
    s1 = hs_ref[b, t, 2 * e + 1]
    narrow = (s0 >= 0) & (s1 >= 0)
    k = ybuf.shape[3]

    @pl.when(narrow)
    def _():
        slot_iota = lax.broadcasted_iota(jnp.int32, (half, tt), 0)
        onehot_t = jnp.concatenate([jnp.where(slot_iota + s0 == slot_ref[0], 1.0, 0.0),
                                    jnp.where(slot_iota + s1 == slot_ref[1], 1.0, 0.0)], axis=0)
        acc_scr[...] += _bdot(onehot_t.T.astype(BF16), ybuf[slot, 0:2].reshape(2 * half, k))

    @pl.when(jnp.logical_not(narrow))
    def _():
        slot_iota = lax.broadcasted_iota(jnp.int32, (cap, tt), 0)
        onehot_t = jnp.concatenate([jnp.where(slot_iota == slot_ref[i], 1.0, 0.0) for i in range(2)], axis=0)
        acc_scr[...] += _bdot(onehot_t.T.astype(BF16), ybuf[slot].reshape(4 * half, k))

    @pl.when(e == ne - 1)
    def _():
        o_ref[...] = x_ref[...] + g2_ref[...] * acc_scr[...]


def _combine(slot_t, y, x, mod, mod_row_fn, nb, n, cap):
    k = x.shape[1]
    tt = min(n, 512)
    ntb = n // tt
    halves = cap // 2 >= LANES
    eg = 2 if halves else min(N_EXPERTS, max(1, 256 // cap))
    cap_pad = cap if (eg * cap) % LANES == 0 else max(cap, LANES)
    cnt = jnp.sum((slot_t >= 0).astype(jnp.int32).reshape(nb, N_EXPERTS, ntb, tt), axis=-1)
    hi = jnp.cumsum(cnt, axis=-1)
    lo = hi - cnt
    win0 = jnp.clip(lo // BF16_ROWS * BF16_ROWS, 0, cap - cap // 2)
    half_state = jnp.where(hi - win0 <= cap // 2, win0, -1).astype(jnp.int32)
    ne = N_EXPERTS // eg
    if halves:
        kern = functools.partial(_combine_win_kernel, tt=tt, cap=cap, nb=nb, ntb=ntb, ne=ne)
        y_spec = pl.BlockSpec(memory_space=pl.ANY)
        scratch = [pltpu.VMEM((tt, k), F32), pltpu.VMEM((2, 2 * eg, cap // 2, k), BF16),
                   pltpu.SemaphoreType.DMA((2,))]
    else:
        kern = functools.partial(_combine_kernel, tt=tt, cap=cap, cap_pad=cap_pad, eg=eg)
        y_spec = pl.BlockSpec((eg, cap, k), lambda b, t, e, hs: (e, b, 0))
        scratch = [pltpu.VMEM((tt, k), F32)]
    grid_spec = pltpu.PrefetchScalarGridSpec(
        num_scalar_prefetch=1,
        grid=(nb, ntb, ne),
        in_specs=[
            pl.BlockSpec((None, eg, 1, tt), lambda b, t, e, hs: (b, e, 0, t)),
            y_spec,
            pl.BlockSpec((tt, k), lambda b, t, e, hs: (b * ntb + t, 0)),
            pl.BlockSpec((None, None, 1, k), lambda b, t, e, hs: (mod_row_fn(b), 5, 0, 0)),
        ],
        out_specs=pl.BlockSpec((tt, k), lambda b, t, e, hs: (b * ntb + t, 0)),
        scratch_shapes=scratch,
    )
    return pl.pallas_call(
        kern,
        grid_spec=grid_spec,
        out_shape=jax.ShapeDtypeStruct((nb * n, k), F32),
        compiler_params=_cparams(("arbitrary", "arbitrary", "arbitrary")),
        name="combine",
    )(jnp.swapaxes(half_state, 1, 2), slot_t.reshape(nb, N_EXPERTS, 1, n), y, x, mod)


def _ec_moe_residual(streams, gain, mod, router, wg, wu, wd, nb):
    routed = []
    for x, n, mod_row_fn in streams:
        cap = EC_CAPACITY_FACTOR * n // N_EXPERTS
        idx = jnp.arange(n, dtype=jnp.int32)
        tri = (idx[:, None] < idx[None, :]).astype(BF16)
        h, gate_t, slot_t = _route(x, gain, mod, mod_row_fn, router.T, tri, nb, n, cap)
        routed.append((slot_t, cap) + _gather(slot_t, gate_t, h, nb, n, cap))
    ys = _expert_ffn([(xs, gs) for _, _, xs, gs in routed], wg, wu, wd)
    return [_combine(slot_t, y, x, mod, mod_row_fn, nb, n, cap)
            for (x, n, mod_row_fn), (slot_t, cap, _, _), y in zip(streams, routed, ys)]


def _rope_tables(n, tm):
    rows = n // GRID_W
    row = jnp.repeat(jnp.arange(rows, dtype=F32), GRID_W)
    col = jnp.tile(jnp.arange(GRID_W, dtype=F32), rows)
    axis_dim = HEAD_DIM // 2
    inv_freq = ROPE_THETA ** (-jnp.arange(0, axis_dim, 2, dtype=F32) / axis_dim)
    ang = jnp.concatenate([row[:, None] * inv_freq, col[:, None] * inv_freq], axis=-1)
    cos, sin = jnp.cos(ang), jnp.sin(ang)
    cos_full = jnp.concatenate([cos, cos], axis=-1).reshape(n // tm, tm, HEAD_DIM)
    sin_signed = jnp.concatenate([-sin, sin], axis=-1).reshape(n // tm, tm, HEAD_DIM)
    return cos_full, sin_signed


def kernel(x, c, ctx, c_ctx,
           mod_w_0, mod_b_0, norm1_0, norm2_0, wqkv_0, wo_0, qnorm_0, knorm_0,
           lam_q1_0, lam_k1_0, lam_q2_0, lam_k2_0, subln_0, router_0, we_gate_0, we_up_0, we_down_0,
           mod_w_1, mod_b_1, norm1_1, norm2_1, wqkv_1, wo_1, qnorm_1, knorm_1,
           router_1, we_gate_1, we_up_1, we_down_1):
    nb, nq, d = x.shape
    nc = ctx.shape[1]
    tm = 1024
    x_lat = x.reshape(nb * nq, d)
    x_ctx = ctx.reshape(nb * nc, d)
    cond = jnp.zeros((MOD_ROWS, d), F32).at[:nb].set(c).at[nb].set(c_ctx)
    lat_row = lambda i: i // (nq // tm)
    ctx_row = lambda i: nb
    lat_b = lambda b: b
    cos, sin = _rope_tables(nq, tm)
    cos_id = jnp.ones((1, tm, HEAD_DIM), F32)
    sin_id = jnp.zeros((1, tm, HEAD_DIM), F32)
    q_scale = math.log2(math.e) / math.sqrt(HEAD_DIM)

    mod0 = _adaln(cond, mod_w_0, mod_b_0)
    n_heads2 = d // HEAD_DIM
    colgain0 = jnp.concatenate([jnp.tile(qnorm_0 * q_scale, n_heads2), jnp.tile(knorm_0, n_heads2),
                                jnp.ones((d,), F32)]).reshape(1, 3 * d)
    qkv_lat = _qkv_proj(x_lat, norm1_0, mod0, lat_row, wqkv_0, colgain0, cos, sin, 2 * d)
    qkv_ctx = _qkv_proj(x_ctx, norm1_0, mod0, ctx_row, wqkv_0, colgain0, cos_id, sin_id, 2 * d)
    lam_init = 0.8 - 0.6 * math.exp(-0.3 * 0)
    lam_params = jnp.stack([lam_q1_0, lam_k1_0, lam_q2_0, lam_k2_0])
    o_lat = _diff_attention(lam_params, subln_0, qkv_lat, [(qkv_lat, nq), (qkv_ctx, nc)], nb, nq, lam_init)
    o_ctx = _diff_attention(lam_params, subln_0, qkv_ctx, [(qkv_ctx, nc)], nb, nc, lam_init)
    x_lat = _proj_residual(o_lat, wo_0, x_lat, mod0, lat_row, 2)
    x_ctx = _proj_residual(o_ctx, wo_0, x_ctx, mod0, ctx_row, 2)
    x_lat, x_ctx = _ec_moe_residual([(x_lat, nq, lat_b), (x_ctx, nc, ctx_row)], norm2_0, mod0,
                                    router_0, we_gate_0, we_up_0, we_down_0, nb)

    mod1 = _adaln(cond, mod_w_1, mod_b_1)
    kv_dim = (wqkv_1.shape[1] - d) // 2
    colgain1 = jnp.concatenate([jnp.tile(qnorm_1 * q_scale, d // HEAD_DIM),
                                jnp.tile(knorm_1, kv_dim // HEAD_DIM),
                                jnp.ones((kv_dim,), F32)]).reshape(1, d + 2 * kv_dim)
    qkv_lat = _qkv_proj(x_lat, norm1_1, mod1, lat_row, wqkv_1, colgain1, cos, sin, d + kv_dim)
    kv_ctx = _qkv_proj(x_ctx, norm1_1, mod1, ctx_row, wqkv_1, colgain1, cos_id, sin_id, d + kv_dim, col_start=d)
    o_lat = _gqa_attention(qkv_lat, kv_ctx, nb, nq, nc)
    x_lat = _proj_residual(o_lat, wo_1, x_lat, mod1, lat_row, 2)
    (x_lat,) = _ec_moe_residual([(x_lat, nq, lat_b)], norm2_1, mod1,
                                router_1, we_gate_1, we_up_1, we_down_1, nb)
    return x_lat.reshape(nb, nq, d)
```
